```python
import math
import jax, jax.numpy as jnp
from jax import lax
import numpy as np

D_MODEL = 2048
BATCH = 16
SEQ = 2048
DEPTH = 4

N_META = 16
ATTN_WIDTH = 1024
POOL_WIDTH = D_MODEL - ATTN_WIDTH
DIFF_HEAD_DIM = 64
DIFF_V_DIM = 2 * DIFF_HEAD_DIM
N_DIFF_HEADS = ATTN_WIDTH // DIFF_V_DIM
POOL_WINDOWS = (2, 4, 8, 16)
N_POOL_GROUPS = len(POOL_WINDOWS)
POOL_GROUP_WIDTH = POOL_WIDTH // N_POOL_GROUPS
IN_WIDTH = 3 * ATTN_WIDTH + POOL_WIDTH
D_FF = 5632
ROPE_THETA = 10000.0
Q_BLOCK = 128
NORM_EPS = 1e-6
SUBLN_EPS = 1e-5

kernel_name = "hymba_diffattn_pool_macaron"


def rms_norm(x, g, eps):
    xf = x.astype(jnp.float32)
    y = xf * lax.rsqrt(jnp.mean(xf * xf, axis=-1, keepdims=True) + eps)
    return (y * g.astype(jnp.float32)).astype(x.dtype)


def swiglu(h, w_gate, w_up, w_down):
    return (jax.nn.silu(h @ w_gate) * (h @ w_up)) @ w_down


def rope_tables(length):
    pos = jnp.arange(length, dtype=jnp.float32)
    inv_freq = 1.0 / (ROPE_THETA ** (jnp.arange(0, DIFF_HEAD_DIM, 2, dtype=jnp.float32) / DIFF_HEAD_DIM))
    ang = pos[:, None] * inv_freq[None, :]
    ang = jnp.concatenate([ang, ang], axis=-1)
    return jnp.cos(ang), jnp.sin(ang)


def apply_rope(x, cos, sin):
    xf = x.astype(jnp.float32)
    half = DIFF_HEAD_DIM // 2
    rot = jnp.concatenate([-xf[..., half:], xf[..., :half]], axis=-1)
    c = cos[None, :, None, None, :]
    s = sin[None, :, None, None, :]
    return (xf * c + rot * s).astype(x.dtype)


def diff_attention(q, k, v, lam):
    length = q.shape[1]
    outs = []
    for i in range(length // Q_BLOCK):
        start, end = i * Q_BLOCK, (i + 1) * Q_BLOCK
        qb, kb, vb = q[:, start:end], k[:, :end], v[:, :end]
        s = jnp.einsum('bqhmd,bkhmd->bhmqk', qb, kb).astype(jnp.float32)
        qpos = start + jnp.arange(Q_BLOCK)
        kpos = jnp.arange(end)
        mask = kpos[None, :] <= qpos[:, None]
        s = jnp.where(mask, s, jnp.finfo(jnp.float32).min)
        p = jax.nn.softmax(s, axis=-1)
        a = p[:, :, 0] - lam * p[:, :, 1]
        outs.append(jnp.einsum('bhqk,bkhe->bqhe', a.astype(v.dtype), vb))
    return jnp.concatenate(outs, axis=1)


def causal_multiscale_pool(u, w_pool, pool_scale):
    b, length, _ = u.shape
    uf = u.reshape(b, length, N_POOL_GROUPS, POOL_GROUP_WIDTH).astype(jnp.float32)
    cs = jnp.cumsum(uf, axis=1)
    t = jnp.arange(length)
    means = []
    for g, w in enumerate(POOL_WINDOWS):
        c = cs[:, :, g]
        lagged = jnp.pad(c, ((0, 0), (w, 0), (0, 0)))[:, :length]
        count = jnp.minimum(t + 1, w).astype(jnp.float32)[None, :, None]
        means.append((c - lagged) / count)
    diff = (jnp.stack(means, axis=2) - uf).astype(u.dtype)
    y = jnp.einsum('blgc,gcd->blgd', diff, w_pool).reshape(b, length, POOL_WIDTH)
    return y * pool_scale


def hybrid_mixer(h, w_in, lam_q1, lam_k1, lam_q2, lam_k2, subln_g, w_pool, pool_scale, w_out, cos, sin, lam_init):
    b, length, _ = h.shape
    proj = h @ w_in
    q, k, v, u = jnp.split(proj, [ATTN_WIDTH, 2 * ATTN_WIDTH, 3 * ATTN_WIDTH], axis=-1)
    q = apply_rope(q.reshape(b, length, N_DIFF_HEADS, 2, DIFF_HEAD_DIM), cos, sin) * (DIFF_HEAD_DIM ** -0.5)
    k = apply_rope(k.reshape(b, length, N_DIFF_HEADS, 2, DIFF_HEAD_DIM), cos, sin)
    v = v.reshape(b, length, N_DIFF_HEADS, DIFF_V_DIM)
    lam = (jnp.exp(jnp.sum(lam_q1.astype(jnp.float32) * lam_k1.astype(jnp.float32)))
           - jnp.exp(jnp.sum(lam_q2.astype(jnp.float32) * lam_k2.astype(jnp.float32)))
           + lam_init)
    o = diff_attention(q, k, v, lam)
    o = rms_norm(o, subln_g, SUBLN_EPS) * (1.0 - lam_init)
    attn_out = o.reshape(b, length, ATTN_WIDTH)
    pool_out = causal_multiscale_pool(u, w_pool, pool_scale)
    return jnp.concatenate([attn_out, pool_out], axis=-1) @ w_out


def setup_inputs(seed: int = 0) -> dict:
    key = jax.random.key(seed)
    ks = jax.random.split(key, 24)
    f32 = jnp.float32
    nrm = lambda k, shape, s: jax.random.normal(k, shape, f32) * s
    gain = lambda k, shape: 1.0 + 0.02 * jax.random.normal(k, shape, f32)
    return {
        "x": nrm(ks[0], (BATCH, SEQ, D_MODEL), 1.0),
        "meta_tokens": nrm(ks[1], (N_META, D_MODEL), 1.0),
        "ffn1_norm_g": gain(ks[2], (DEPTH, D_MODEL)),
        "ffn1_w_gate": nrm(ks[3], (DEPTH, D_MODEL, D_FF), D_MODEL ** -0.5),
        "ffn1_w_up": nrm(ks[4], (DEPTH, D_MODEL, D_FF), D_MODEL ** -0.5),
        "ffn1_w_down": nrm(ks[5], (DEPTH, D_FF, D_MODEL), D_FF ** -0.5),
        "mix_norm_g": gain(ks[6], (DEPTH, D_MODEL)),
        "w_in": nrm(ks[7], (DEPTH, D_MODEL, IN_WIDTH), D_MODEL ** -0.5),
        "lam_q1": nrm(ks[8], (DEPTH, DIFF_HEAD_DIM), 0.1),
        "lam_k1": nrm(ks[9], (DEPTH, DIFF_HEAD_DIM), 0.1),
        "lam_q2": nrm(ks[10], (DEPTH, DIFF_HEAD_DIM), 0.1),
        "lam_k2": nrm(ks[11], (DEPTH, DIFF_HEAD_DIM), 0.1),
        "subln_g": gain(ks[12], (DEPTH, DIFF_V_DIM)),
        "w_pool": nrm(ks[13], (DEPTH, N_POOL_GROUPS, POOL_GROUP_WIDTH, POOL_GROUP_WIDTH), POOL_GROUP_WIDTH ** -0.5),
        "pool_scale": gain(ks[14], (DEPTH, POOL_WIDTH)),
        "w_out": nrm(ks[15], (DEPTH, D_MODEL, D_MODEL), D_MODEL ** -0.5),
        "ffn2_norm_g": gain(ks[16], (DEPTH, D_MODEL)),
        "ffn2_w_gate": nrm(ks[17], (DEPTH, D_MODEL, D_FF), D_MODEL ** -0.5),
        "ffn2_w_up": nrm(ks[18], (DEPTH, D_MODEL, D_FF), D_MODEL ** -0.5),
        "ffn2_w_down": nrm(ks[19], (DEPTH, D_FF, D_MODEL), D_FF ** -0.5),
        "final_norm_g": gain(ks[20], (D_MODEL,)),
    }


def reference(x, meta_tokens, ffn1_norm_g, ffn1_w_gate, ffn1_w_up, ffn1_w_down, mix_norm_g, w_in,
              lam_q1, lam_k1, lam_q2, lam_k2, subln_g, w_pool, pool_scale, w_out,
              ffn2_norm_g, ffn2_w_gate, ffn2_w_up, ffn2_w_down, final_norm_g):
    b, seq, d = x.shape
    total = N_META + seq
    pad = (-total) % Q_BLOCK
    meta = jnp.broadcast_to(meta_tokens.astype(x.dtype)[None], (b, N_META, d))
    h = jnp.concatenate([meta, x, jnp.zeros((b, pad, d), x.dtype)], axis=1)
    cos, sin = rope_tables(h.shape[1])
    for layer in range(DEPTH):
        lam_init = 0.8 - 0.6 * math.exp(-0.3 * layer)
        h = h + 0.5 * swiglu(rms_norm(h, ffn1_norm_g[layer], NORM_EPS),
                             ffn1_w_gate[layer], ffn1_w_up[layer], ffn1_w_down[layer])
        h = h + hybrid_mixer(rms_norm(h, mix_norm_g[layer], NORM_EPS), w_in[layer],
                             lam_q1[layer], lam_k1[layer], lam_q2[layer], lam_k2[layer], subln_g[layer],
                             w_pool[layer], pool_scale[layer], w_out[layer], cos, sin, lam_init)
        h = h + 0.5 * swiglu(rms_norm(h, ffn2_norm_g[layer], NORM_EPS),
                             ffn2_w_gate[layer], ffn2_w_up[layer], ffn2_w_down[layer])
    h = rms_norm(h, final_norm_g, NORM_EPS)
    return h[:, N_META:N_META + seq]
```

```python
import functools
import math

import jax
import jax.numpy as jnp
from jax import lax
from jax.experimental import pallas as pl
from jax.experimental.pallas import tpu as pltpu

F32 = jnp.float32
BF16 = jnp.bfloat16

D_MODEL = 2048
N_META = 16
ATTN_WIDTH = 1024
POOL_WIDTH = 1024
HEAD_DIM = 64
V_DIM = 2 * HEAD_DIM
N_HEADS = ATTN_WIDTH // V_DIM
POOL_WINDOWS = (2, 4, 8, 16)
POOL_GROUP_WIDTH = POOL_WIDTH // len(POOL_WINDOWS)
IN_WIDTH = 3 * ATTN_WIDTH + POOL_WIDTH
D_FF = 5632
ROPE_THETA = 10000.0
NORM_EPS = 1e-6
SUBLN_EPS = 1e-5

LANES = 128
V7X_VMEM_BYTES = 64 * 1024 * 1024


def _vmem_limit(estimate_bytes):
    return int(min(estimate_bytes * 5 // 4 + (4 << 20), V7X_VMEM_BYTES * 7 // 8))


def _rms_norm(x, g, eps):
    return x * lax.rsqrt(jnp.mean(x * x, axis=-1, keepdims=True) + eps) * g


def _ffn_kernel(x_ref, g_ref, wg_ref, wu_ref, wd_ref, fg_ref, o_ref, xn_ref, *, apply_final):
    f = pl.program_id(1)

    @pl.when(f == 0)
    def _():
        x = x_ref[...]
        xn_ref[...] = _rms_norm(x, g_ref[...], NORM_EPS).astype(BF16)
        o_ref[...] = x

    xn = xn_ref[...]
    gate = jnp.dot(xn, wg_ref[...], preferred_element_type=F32)
    up = jnp.dot(xn, wu_ref[...], preferred_element_type=F32)
    act = (gate * jax.nn.sigmoid(gate)) * (0.5 * up)
    o_ref[...] += jnp.dot(act.astype(BF16), wd_ref[...], preferred_element_type=F32)

    if apply_final:
        @pl.when(f == pl.num_programs(1) - 1)
        def _():
            o_ref[...] = _rms_norm(o_ref[...], fg_ref[...], NORM_EPS)


def _ffn(x, g, wg, wu, wd, final_g, *, tm, tf, apply_final):
    t = x.shape[0]
    assert t % tm == 0 and D_FF % tf == 0
    est = (2 * 2 * tm * D_MODEL * 4 + tm * D_MODEL * 2 + 2 * 3 * D_MODEL * tf * 2
           + 3 * tm * tf * 4)
    return pl.pallas_call(
        functools.partial(_ffn_kernel, apply_final=apply_final),
        out_shape=jax.ShapeDtypeStruct((t, D_MODEL), F32),
        grid=(t // tm, D_FF // tf),
        in_specs=[
            pl.BlockSpec((tm, D_MODEL), lambda i, f: (i, 0)),
            pl.BlockSpec((1, D_MODEL), lambda i, f: (0, 0)),
            pl.BlockSpec((D_MODEL, tf), lambda i, f: (0, f)),
            pl.BlockSpec((D_MODEL, tf), lambda i, f: (0, f)),
            pl.BlockSpec((tf, D_MODEL), lambda i, f: (f, 0)),
            pl.BlockSpec((1, D_MODEL), lambda i, f: (0, 0)),
        ],
        out_specs=pl.BlockSpec((tm, D_MODEL), lambda i, f: (i, 0)),
        scratch_shapes=[pltpu.VMEM((tm, D_MODEL), BF16)],
        compiler_params=pltpu.CompilerParams(
            dimension_semantics=("parallel", "arbitrary"),
            vmem_limit_bytes=_vmem_limit(est)),
        name="ffn",
    )(x, g, wg, wu, wd, final_g)


def _rope(x, cos, sin_signed):
    lane = lax.broadcasted_iota(jnp.int32, (x.shape[0], LANES), 1)
    upper = (lane % HEAD_DIM) >= (HEAD_DIM // 2)
    outs = []
    for c in range(x.shape[1] // LANES):
        xc = x[:, c * LANES:(c + 1) * LANES]
        from_below = pltpu.roll(xc, HEAD_DIM // 2, axis=1)
        from_above = pltpu.roll(xc, LANES - HEAD_DIM // 2, axis=1)
        rot = jnp.where(upper, from_below, from_above)
        outs.append(xc * cos + rot * sin_signed)
    return jnp.concatenate(outs, axis=1)


def _pool_diff(ubuf_ref, rows, inv_count_fn):
    outs = []
    for g, w in enumerate(POOL_WINDOWS):
        cols = slice(g * POOL_GROUP_WIDTH, (g + 1) * POOL_GROUP_WIDTH)
        cur = ubuf_ref[N_META:N_META + rows, cols]
        tot = cur
        for back in range(1, w):
            tot = tot + ubuf_ref[N_META - back:N_META - back + rows, cols]
        outs.append(tot * inv_count_fn(w) - cur)
    return jnp.concatenate(outs, axis=1)


def _inproj_kernel(*refs, meta):
    if meta:
        (x_ref, g_ref, w_ref, cq_ref, sq_ref, ck_ref, sk_ref,
         o_ref, u_ref, xn_ref, ubuf_ref) = refs
    else:
        (x_ref, g_ref, w_ref, cq_ref, sq_ref, ck_ref, sk_ref, um_ref,
         o_ref, xn_ref, ubuf_ref) = refs
    j = pl.program_id(1)
    n = pl.program_id(2)
    rows = x_ref.shape[0]

    @pl.when(n == 0)
    def _():
        xn_ref[...] = _rms_norm(x_ref[...], g_ref[...], NORM_EPS).astype(BF16)

    acc = jnp.dot(xn_ref[...], w_ref[...], preferred_element_type=F32)

    @pl.when(n == 0)
    def _():
        o_ref[...] = _rope(acc, cq_ref[...], sq_ref[...]).astype(BF16)

    @pl.when(n == 1)
    def _():
        o_ref[...] = _rope(acc, ck_ref[...], sk_ref[...]).astype(BF16)

    @pl.when(n == 2)
    def _():
        o_ref[...] = acc.astype(BF16)

    @pl.when(n == 3)
    def _():
        if meta:
            ubuf_ref[0:N_META, :] = jnp.zeros((N_META, POOL_WIDTH), F32)
            u_ref[...] = acc
            pos = lax.broadcasted_iota(jnp.int32, (rows, 1), 0)

            def inv_count(w):
                return 1.0 / jnp.minimum(pos + 1, w).astype(F32)
        else:
            @pl.when(j == 0)
            def _():
                ubuf_ref[0:N_META, :] = um_ref[...]

            @pl.when(j > 0)
            def _():
                ubuf_ref[0:N_META, :] = ubuf_ref[rows:rows + N_META, :]

            def inv_count(w):
                return 1.0 / w
        ubuf_ref[N_META:N_META + rows, :] = acc
        o_ref[...] = _pool_diff(ubuf_ref, rows, inv_count).astype(BF16)


def _inproj(x, g, w_in, tables, u_meta, *, batch, tl, meta):
    t = x.shape[0]
    seq = t // batch
    nj = seq // tl
    assert seq % tl == 0
    cq, sq, ck, sk = tables
    tok = lambda b, j, n: (b * nj + j, 0)
    tab = lambda b, j, n: (j, 0)
    in_specs = [
        pl.BlockSpec((tl, D_MODEL), tok),
        pl.BlockSpec((1, D_MODEL), lambda b, j, n: (0, 0)),
        pl.BlockSpec((D_MODEL, ATTN_WIDTH), lambda b, j, n: (0, n)),
        pl.BlockSpec((tl, LANES), tab),
        pl.BlockSpec((tl, LANES), tab),
        pl.BlockSpec((tl, LANES), tab),
        pl.BlockSpec((tl, LANES), tab),
    ]
    args = [x, g, w_in, cq, sq, ck, sk]
    proj_shape = jax.ShapeDtypeStruct((t, IN_WIDTH), BF16)
    proj_spec = pl.BlockSpec((tl, ATTN_WIDTH), lambda b, j, n: (b * nj + j, n))
    if meta:
        out_shape = (proj_shape, jax.ShapeDtypeStruct((t, POOL_WIDTH), F32))
        out_specs = (proj_spec, pl.BlockSpec((tl, POOL_WIDTH), lambda b, j, n: (0, 0)))
    else:
        in_specs.append(pl.BlockSpec((N_META, POOL_WIDTH), lambda b, j, n: (0, 0)))
        args.append(u_meta)
        out_shape = proj_shape
        out_specs = proj_spec
    est = (2 * tl * D_MODEL * 4 + 2 * D_MODEL * ATTN_WIDTH * 2 + 2 * tl * ATTN_WIDTH * 2
           + tl * D_MODEL * 2 + 3 * (tl + N_META) * POOL_WIDTH * 4 + 8 * tl * LANES * 4)
    return pl.pallas_call(
        functools.partial(_inproj_kernel, meta=meta),
        out_shape=out_shape,
        grid=(batch, nj, IN_WIDTH // ATTN_WIDTH),
        in_specs=in_specs,
        out_specs=out_specs,
        scratch_shapes=[pltpu.VMEM((tl, D_MODEL), BF16),
                        pltpu.VMEM((tl + N_META, POOL_WIDTH), F32)],
        compiler_params=pltpu.CompilerParams(
            dimension_semantics=("arbitrary", "arbitrary", "arbitrary"),
            vmem_limit_bytes=_vmem_limit(est)),
        name="inproj_meta" if meta else "inproj",
    )(*args)


def _stack_queries(q):
    lane = lax.broadcasted_iota(jnp.int32, q.shape, 1)
    zero = jnp.zeros_like(q)
    return jnp.concatenate([jnp.where(lane < HEAD_DIM, q, zero),
                            jnp.where(lane >= HEAD_DIM, q, zero)], axis=0)


def _scores(qq, k):
    return lax.dot_general(qq, k, (((1,), (1,)), ((), ())), preferred_element_type=F32)


def _lambda(lq1_ref, lk1_ref, lq2_ref, lk2_ref, lam_init):
    s1 = jnp.sum(lq1_ref[...] * lk1_ref[...], axis=-1, keepdims=True)
    s2 = jnp.sum(lq2_ref[...] * lk2_ref[...], axis=-1, keepdims=True)
    return jnp.exp(s1) - jnp.exp(s2) + lam_init


def _finish(acc, l, lam, g, lam_init, tq):
    o = acc[:tq] / l[:tq] - lam * (acc[tq:] / l[tq:])
    return _rms_norm(o, g, SUBLN_EPS) * (1.0 - lam_init)


def _attn_kernel(lq1_ref, lk1_ref, lq2_ref, lk2_ref, g_ref, q_ref, k_ref, v_ref, km_ref, vm_ref,
                 o_ref, *, lam_init, tq, tk):
    i = pl.program_id(2)
    qq = _stack_queries(q_ref[...])

    s = _scores(qq, km_ref[...])
    m = jnp.max(s, axis=-1, keepdims=True)
    p = jnp.exp(s - m)
    l = jnp.sum(p, axis=-1, keepdims=True)
    acc = jnp.dot(p.astype(BF16), vm_ref[...], preferred_element_type=F32)

    def step(s, v, carry):
        m, l, acc = carry
        m_new = jnp.maximum(m, jnp.max(s, axis=-1, keepdims=True))
        alpha = jnp.exp(m - m_new)
        p = jnp.exp(s - m_new)
        l = alpha * l + jnp.sum(p, axis=-1, keepdims=True)
        acc = alpha * acc + jnp.dot(p.astype(BF16), v, preferred_element_type=F32)
        return m_new, l, acc

    n_sub = tq // tk

    def body(jb, carry):
        start = pl.multiple_of(jb * tk, tk)
        s = _scores(qq, k_ref[pl.ds(start, tk), :])
        return step(s, v_ref[pl.ds(start, tk), :], carry)

    carry = lax.fori_loop(0, i * n_sub, body, (m, l, acc))

    row = lax.broadcasted_iota(jnp.int32, (2 * tq, tk), 0) % tq
    col = lax.broadcasted_iota(jnp.int32, (2 * tq, tk), 1)
    for d in range(n_sub):
        start = pl.multiple_of(i * tq + d * tk, tk)
        s = _scores(qq, k_ref[pl.ds(start, tk), :])
        s = jnp.where(col + d * tk <= row, s, jnp.finfo(F32).min)
        carry = step(s, v_ref[pl.ds(start, tk), :], carry)

    m, l, acc = carry
    lam = _lambda(lq1_ref, lk1_ref, lq2_ref, lk2_ref, lam_init)
    o_ref[...] = _finish(acc, l, lam, g_ref[...], lam_init, tq).astype(BF16)


def _attn(lam_params, subln_g, proj, proj_meta, *, batch, lam_init, tq, tk):
    t = proj.shape[0]
    seq = t // batch
    nq = seq // tq
    assert seq % tq == 0 and tq % tk == 0
    small = lambda b, h, i: (0, 0)
    lam_specs = [pl.BlockSpec((1, HEAD_DIM), small)] * 4
    est = 2 * (tq * V_DIM * 2 * 2 + 2 * seq * V_DIM * 2) + 6 * 2 * tq * tk * 4
    return pl.pallas_call(
        functools.partial(_attn_kernel, lam_init=lam_init, tq=tq, tk=tk),
        out_shape=jax.ShapeDtypeStruct((t, ATTN_WIDTH), BF16),
        grid=(batch, N_HEADS, nq),
        in_specs=lam_specs + [
            pl.BlockSpec((1, V_DIM), small),
            pl.BlockSpec((tq, V_DIM), lambda b, h, i: (b * nq + i, h)),
            pl.BlockSpec((seq, V_DIM), lambda b, h, i: (b, N_HEADS + h)),
            pl.BlockSpec((seq, V_DIM), lambda b, h, i: (b, 2 * N_HEADS + h)),
            pl.BlockSpec((N_META, V_DIM), lambda b, h, i: (0, N_HEADS + h)),
            pl.BlockSpec((N_META, V_DIM), lambda b, h, i: (0, 2 * N_HEADS + h)),
        ],
        out_specs=pl.BlockSpec((tq, V_DIM), lambda b, h, i: (b * nq + i, h)),
        compiler_params=pltpu.CompilerParams(
            dimension_semantics=("parallel", "parallel", "arbitrary"),
            vmem_limit_bytes=_vmem_limit(est)),
        name="attn",
    )(*lam_params, subln_g, proj, proj, proj, proj_meta, proj_meta)


def _attn_meta_kernel(lq1_ref, lk1_ref, lq2_ref, lk2_ref, g_ref, q_ref, k_ref, v_ref, o_ref,
                      *, lam_init):
    qq = _stack_queries(q_ref[...])
    s = _scores(qq, k_ref[...])
    row = lax.broadcasted_iota(jnp.int32, s.shape, 0) % N_META
    col = lax.broadcasted_iota(jnp.int32, s.shape, 1)
    s = jnp.where(col <= row, s, jnp.finfo(F32).min)
    m = jnp.max(s, axis=-1, keepdims=True)
    p = jnp.exp(s - m)
    l = jnp.sum(p, axis=-1, keepdims=True)
    acc = jnp.dot(p.astype(BF16), v_ref[...], preferred_element_type=F32)
    lam = _lambda(lq1_ref, lk1_ref, lq2_ref, lk2_ref, lam_init)
    o_ref[...] = _finish(acc, l, lam, g_ref[...], lam_init, N_META).astype(BF16)


def _attn_meta(lam_params, subln_g, proj_meta, *, lam_init):
    small = lambda h: (0, 0)
    return pl.pallas_call(
        functools.partial(_attn_meta_kernel, lam_init=lam_init),
        out_shape=jax.ShapeDtypeStruct((N_META, ATTN_WIDTH), BF16),
        grid=(N_HEADS,),
        in_specs=[pl.BlockSpec((1, HEAD_DIM), small)] * 4 + [
            pl.BlockSpec((1, V_DIM), small),
            pl.BlockSpec((N_META, V_DIM), lambda h: (0, h)),
            pl.BlockSpec((N_META, V_DIM), lambda h: (0, N_HEADS + h)),
            pl.BlockSpec((N_META, V_DIM), lambda h: (0, 2 * N_HEADS + h)),
        ],
        out_specs=pl.BlockSpec((N_META, V_DIM), lambda h: (0, h)),
        compiler_params=pltpu.CompilerParams(dimension_semantics=("arbitrary",)),
        name="attn_meta",
    )(*lam_params, subln_g, proj_meta, proj_meta, proj_meta)


def _outproj_kernel(h_ref, a_ref, d_ref, wp_ref, ps_ref, wo_ref, o_ref):
    diff = d_ref[...]
    pools = []
    for g in range(len(POOL_WINDOWS)):
        cols = slice(g * POOL_GROUP_WIDTH, (g + 1) * POOL_GROUP_WIDTH)
        pools.append(jnp.dot(diff[:, cols], wp_ref[g], preferred_element_type=F32))
    pool = (jnp.concatenate(pools, axis=1) * ps_ref[...]).astype(BF16)
    mix = jnp.concatenate([a_ref[...], pool], axis=1)
    o_ref[...] = h_ref[...] + jnp.dot(mix, wo_ref[...], preferred_element_type=F32)


def _outproj(h, attn, proj, w_pool, pool_scale, w_out, *, tm):
    t = h.shape[0]
    assert t % tm == 0
    est = (2 * 2 * tm * D_MODEL * 4 + 2 * 2 * tm * ATTN_WIDTH * 2 + 2 * D_MODEL * D_MODEL * 2
           + 2 * POOL_WIDTH * POOL_GROUP_WIDTH * 2 + 3 * tm * D_MODEL * 4)
    return pl.pallas_call(
        _outproj_kernel,
        out_shape=jax.ShapeDtypeStruct((t, D_MODEL), F32),
        grid=(t // tm,),
        in_specs=[
            pl.BlockSpec((tm, D_MODEL), lambda i: (i, 0)),
            pl.BlockSpec((tm, ATTN_WIDTH), lambda i: (i, 0)),
            pl.BlockSpec((tm, POOL_WIDTH), lambda i: (i, 3)),
            pl.BlockSpec((len(POOL_WINDOWS), POOL_GROUP_WIDTH, POOL_GROUP_WIDTH),
                         lambda i: (0, 0, 0)),
            pl.BlockSpec((1, POOL_WIDTH), lambda i: (0, 0)),
            pl.BlockSpec((D_MODEL, D_MODEL), lambda i: (0, 0)),
        ],
        out_specs=pl.BlockSpec((tm, D_MODEL), lambda i: (i, 0)),
        compiler_params=pltpu.CompilerParams(
            dimension_semantics=("parallel",),
            vmem_limit_bytes=_vmem_limit(est)),
        name="outproj",
    )(h, attn, proj, w_pool, pool_scale, w_out)


def _rope_tables(length):
    pos = jnp.arange(length, dtype=F32)
    inv_freq = 1.0 / (ROPE_THETA ** (jnp.arange(0, HEAD_DIM, 2, dtype=F32) / HEAD_DIM))
    ang = pos[:, None] * inv_freq[None, :]
    ang = jnp.concatenate([ang, ang, ang, ang], axis=-1)
    sign = jnp.where((jnp.arange(LANES) % HEAD_DIM) < HEAD_DIM // 2, -1.0, 1.0).astype(F32)
    cos, sin = jnp.cos(ang), jnp.sin(ang) * sign
    scale = HEAD_DIM ** -0.5
    return cos * scale, sin * scale, cos, sin


def kernel(x, meta_tokens, ffn1_norm_g, ffn1_w_gate, ffn1_w_up, ffn1_w_down, mix_norm_g, w_in,
           lam_q1, lam_k1, lam_q2, lam_k2, subln_g, w_pool, pool_scale, w_out,
           ffn2_norm_g, ffn2_w_gate, ffn2_w_up, ffn2_w_down, final_norm_g):
    batch, seq, d = x.shape
    depth = w_in.shape[0]
    assert d == D_MODEL and meta_tokens.shape == (N_META, D_MODEL)

    tables = _rope_tables(N_META + seq)
    tab_meta = tuple(tb[:N_META] for tb in tables)
    tab_real = tuple(tb[N_META:] for tb in tables)
    final_g = final_norm_g.reshape(1, D_MODEL)

    h = x.reshape(batch * seq, D_MODEL)
    hm = meta_tokens.astype(x.dtype)

    for layer in range(depth):
        lam_init = 0.8 - 0.6 * math.exp(-0.3 * layer)
        last = layer == depth - 1
        wg1, wu1, wd1 = (w[layer].astype(BF16) for w in (ffn1_w_gate, ffn1_w_up, ffn1_w_down))
        wg2, wu2, wd2 = (w[layer].astype(BF16) for w in (ffn2_w_gate, ffn2_w_up, ffn2_w_down))
        wi = w_in[layer].astype(BF16)
        wo = w_out[layer].astype(BF16)
        wp = w_pool[layer].astype(BF16)
        g1 = ffn1_norm_g[layer].reshape(1, D_MODEL)
        gm = mix_norm_g[layer].reshape(1, D_MODEL)
        g2 = ffn2_norm_g[layer].reshape(1, D_MODEL)
        ps = pool_scale[layer].reshape(1, POOL_WIDTH)
        sg = subln_g[layer].reshape(1, V_DIM)
        lam_params = tuple(p[layer].reshape(1, HEAD_DIM) for p in (lam_q1, lam_k1, lam_q2, lam_k2))

        hm = _ffn(hm, g1, wg1, wu1, wd1, final_g, tm=N_META, tf=512, apply_final=False)
        proj_m, u_m = _inproj(hm, gm, wi, tab_meta, None, batch=1, tl=N_META, meta=True)
        attn_m = _attn_meta(lam_params, sg, proj_m, lam_init=lam_init)
        if not last:
            hm = _outproj(hm, attn_m, proj_m, wp, ps, wo, tm=N_META)
            hm = _ffn(hm, g2, wg2, wu2, wd2, final_g, tm=N_META, tf=512, apply_final=False)

        h = _ffn(h, g1, wg1, wu1, wd1, final_g, tm=512, tf=512, apply_final=False)
        proj = _inproj(h, gm, wi, tab_real, u_m, batch=batch, tl=512, meta=False)
        attn = _attn(lam_params, sg, proj, proj_m, batch=batch, lam_init=lam_init, tq=512, tk=512)
        h = _outproj(h, attn, proj, wp, ps, wo, tm=512)
        h = _ffn(h, g2, wg2, wu2, wd2, final_g, tm=512, tf=512, apply_final=last)

    return h.reshape(batch, seq, D_MODEL)
```

```python
import functools
import math

import jax
import jax.numpy as jnp
from jax import lax
from jax.experimental import pallas as pl
from jax.experimental.pallas import tpu as pltpu

F32 = jnp.float32
BF16 = jnp.bfloat16

D_MODEL = 2048
N_META = 16
ATTN_WIDTH = 1024
POOL_WIDTH = 1024
HEAD_DIM = 64
V_DIM = 2 * HEAD_DIM
N_HEADS = ATTN_WIDTH // V_DIM
POOL_WINDOWS = (2, 4, 8, 16)
POOL_GROUP_WIDTH = POOL_WIDTH // len(POOL_WINDOWS)
IN_WIDTH = 3 * ATTN_WIDTH + POOL_WIDTH
D_FF = 5632
ROPE_THETA = 10000.0
NORM_EPS = 1e-6
SUBLN_EPS = 1e-5

LANES = 128
V7X_VMEM_BYTES = 64 * 1024 * 1024


def _vmem_limit(estimate_bytes):
    return int(min(estimate_bytes * 5 // 4 + (4 << 20), V7X_VMEM_BYTES * 7 // 8))


def _rms_norm(x, g, eps):
    return x * lax.rsqrt(jnp.mean(x * x, axis=-1, keepdims=True) + eps) * g


def _ffn_kernel(x_ref, g_ref, wg_ref, wu_ref, wd_ref, fg_ref, o_ref, xn_ref, *, apply_final):
    f = pl.program_id(1)

    @pl.when(f == 0)
    def _():
        x = x_ref[...]
        xn_ref[...] = _rms_norm(x, g_ref[...], NORM_EPS).astype(BF16)
        o_ref[...] = x

    xn = xn_ref[...]
    gate = jnp.dot(xn, wg_ref[...], preferred_element_type=F32)
    up = jnp.dot(xn, wu_ref[...], preferred_element_type=F32)
    act = (gate * jax.nn.sigmoid(gate)) * (0.5 * up)
    o_ref[...] += jnp.dot(act.astype(BF16), wd_ref[...], preferred_element_type=F32)

    if apply_final:
        @pl.when(f == pl.num_programs(1) - 1)
        def _():
            o_ref[...] = _rms_norm(o_ref[...], fg_ref[...], NORM_EPS)


def _ffn(x, g, wg, wu, wd, final_g, *, tm, tf, apply_final):
    t = x.shape[0]
    assert t % tm == 0 and D_FF % tf == 0
    est = (2 * 2 * tm * D_MODEL * 4 + tm * D_MODEL * 2 + 2 * 3 * D_MODEL * tf * 2
           + 3 * tm * tf * 4)
    return pl.pallas_call(
        functools.partial(_ffn_kernel, apply_final=apply_final),
        out_shape=jax.ShapeDtypeStruct((t, D_MODEL), F32),
        grid=(t // tm, D_FF // tf),
        in_specs=[
            pl.BlockSpec((tm, D_MODEL), lambda i, f: (i, 0)),
            pl.BlockSpec((1, D_MODEL), lambda i, f: (0, 0)),
            pl.BlockSpec((D_MODEL, tf), lambda i, f: (0, f)),
            pl.BlockSpec((D_MODEL, tf), lambda i, f: (0, f)),
            pl.BlockSpec((tf, D_MODEL), lambda i, f: (f, 0)),
            pl.BlockSpec((1, D_MODEL), lambda i, f: (0, 0)),
        ],
        out_specs=pl.BlockSpec((tm, D_MODEL), lambda i, f: (i, 0)),
        scratch_shapes=[pltpu.VMEM((tm, D_MODEL), BF16)],
        compiler_params=pltpu.CompilerParams(
            dimension_semantics=("parallel", "arbitrary"),
            vmem_limit_bytes=_vmem_limit(est)),
        name="ffn",
    )(x, g, wg, wu, wd, final_g)


def _rope(x, cos, sin_signed):
    lane = lax.broadcasted_iota(jnp.int32, (x.shape[0], LANES), 1)
    upper = (lane % HEAD_DIM) >= (HEAD_DIM // 2)
    outs = []
    for c in range(x.shape[1] // LANES):
        xc = x[:, c * LANES:(c + 1) * LANES]
        from_below = pltpu.roll(xc, HEAD_DIM // 2, axis=1)
        from_above = pltpu.roll(xc, LANES - HEAD_DIM // 2, axis=1)
        rot = jnp.where(upper, from_below, from_above)
        outs.append(xc * cos + rot * sin_signed)
    return jnp.concatenate(outs, axis=1)


def _pool_diff(ubuf_ref, rows, inv_count_fn):
    outs = []
    for g, w in enumerate(POOL_WINDOWS):
        cols = slice(g * POOL_GROUP_WIDTH, (g + 1) * POOL_GROUP_WIDTH)
        cur = ubuf_ref[N_META:N_META + rows, cols]
        tot = cur
        for back in range(1, w):
            tot = tot + ubuf_ref[N_META - back:N_META - back + rows, cols]
        outs.append(tot * inv_count_fn(w) - cur)
    return jnp.concatenate(outs, axis=1)


def _inproj_kernel(*refs, meta):
    if meta:
        (x_ref, g_ref, w_ref, cq_ref, sq_ref, ck_ref, sk_ref,
         o_ref, u_ref, xn_ref, ubuf_ref) = refs
    else:
        (x_ref, g_ref, w_ref, cq_ref, sq_ref, ck_ref, sk_ref, um_ref,
         o_ref, xn_ref, ubuf_ref) = refs
    j = pl.program_id(1)
    n = pl.program_id(2)
    rows = x_ref.shape[0]

    @pl.when(n == 0)
    def _():
        xn_ref[...] = _rms_norm(x_ref[...], g_ref[...], NORM_EPS).astype(BF16)

    acc = jnp.dot(xn_ref[...], w_ref[...], preferred_element_type=F32)

    @pl.when(n == 0)
    def _():
        o_ref[...] = _rope(acc, cq_ref[...], sq_ref[...]).astype(BF16)

    @pl.when(n == 1)
    def _():
        o_ref[...] = _rope(acc, ck_ref[...], sk_ref[...]).astype(BF16)

    @pl.when(n == 2)
    def _():
        o_ref[...] = acc.astype(BF16)

    @pl.when(n == 3)
    def _():
        if meta:
            ubuf_ref[0:N_META, :] = jnp.zeros((N_META, POOL_WIDTH), F32)
            u_ref[...] = acc
            pos = lax.broadcasted_iota(jnp.int32, (rows, 1), 0)

            def inv_count(w):
                return 1.0 / jnp.minimum(pos + 1, w).astype(F32)
        else:
            @pl.when(j == 0)
            def _():
                ubuf_ref[0:N_META, :] = um_ref[...]

            @pl.when(j > 0)
            def _():
                ubuf_ref[0:N_META, :] = ubuf_ref[rows:rows + N_META, :]

            def inv_count(w):
                return 1.0 / w
        ubuf_ref[N_META:N_META + rows, :] = acc
        o_ref[...] = _pool_diff(ubuf_ref, rows, inv_count).astype(BF16)


def _inproj(x, g, w_in, tables, u_meta, *, batch, tl, meta):
    t = x.shape[0]
    seq = t // batch
    nj = seq // tl
    assert seq % tl == 0
    cq, sq, ck, sk = tables
    tok = lambda b, j, n: (b * nj + j, 0)
    tab = lambda b, j, n: (j, 0)
    in_specs = [
        pl.BlockSpec((tl, D_MODEL), tok),
        pl.BlockSpec((1, D_MODEL), lambda b, j, n: (0, 0)),
        pl.BlockSpec((D_MODEL, ATTN_WIDTH), lambda b, j, n: (0, n)),
        pl.BlockSpec((tl, LANES), tab),
        pl.BlockSpec((tl, LANES), tab),
        pl.BlockSpec((tl, LANES), tab),
        pl.BlockSpec((tl, LANES), tab),
    ]
    args = [x, g, w_in, cq, sq, ck, sk]
    proj_shape = jax.ShapeDtypeStruct((t, IN_WIDTH), BF16)
    proj_spec = pl.BlockSpec((tl, ATTN_WIDTH), lambda b, j, n: (b * nj + j, n))
    if meta:
        out_shape = (proj_shape, jax.ShapeDtypeStruct((t, POOL_WIDTH), F32))
        out_specs = (proj_spec, pl.BlockSpec((tl, POOL_WIDTH), lambda b, j, n: (0, 0)))
    else:
        in_specs.append(pl.BlockSpec((N_META, POOL_WIDTH), lambda b, j, n: (0, 0)))
        args.append(u_meta)
        out_shape = proj_shape
        out_specs = proj_spec
    est = (2 * tl * D_MODEL * 4 + 2 * D_MODEL * ATTN_WIDTH * 2 + 2 * tl * ATTN_WIDTH * 2
           + tl * D_MODEL * 2 + 3 * (tl + N_META) * POOL_WIDTH * 4 + 8 * tl * LANES * 4)
    return pl.pallas_call(
        functools.partial(_inproj_kernel, meta=meta),
        out_shape=out_shape,
        grid=(batch, nj, IN_WIDTH // ATTN_WIDTH),
        in_specs=in_specs,
        out_specs=out_specs,
        scratch_shapes=[pltpu.VMEM((tl, D_MODEL), BF16),
                        pltpu.VMEM((tl + N_META, POOL_WIDTH), F32)],
        compiler_params=pltpu.CompilerParams(
            dimension_semantics=("arbitrary", "arbitrary", "arbitrary"),
            vmem_limit_bytes=_vmem_limit(est)),
        name="inproj_meta" if meta else "inproj",
    )(*args)


def _stack_queries(q):
    lane = lax.broadcasted_iota(jnp.int32, q.shape, 1)
    zero = jnp.zeros_like(q)
    return jnp.concatenate([jnp.where(lane < HEAD_DIM, q, zero),
                            jnp.where(lane >= HEAD_DIM, q, zero)], axis=0)


def _scores(qq, k):
    return lax.dot_general(qq, k, (((1,), (1,)), ((), ())), preferred_element_type=F32)


def _lambda(lq1_ref, lk1_ref, lq2_ref, lk2_ref, lam_init):
    s1 = jnp.sum(lq1_ref[...] * lk1_ref[...], axis=-1, keepdims=True)
    s2 = jnp.sum(lq2_ref[...] * lk2_ref[...], axis=-1, keepdims=True)
    return jnp.exp(s1) - jnp.exp(s2) + lam_init


def _finish(acc, l, lam, g, lam_init, tq):
    o = acc[:tq] / l[:tq] - lam * (acc[tq:] / l[tq:])
    return _rms_norm(o, g, SUBLN_EPS) * (1.0 - lam_init)


def _attn_kernel(lq1_ref, lk1_ref, lq2_ref, lk2_ref, g_ref, q_ref, k_ref, v_ref, km_ref, vm_ref,
                 o_ref, m_ref, l_ref, acc_ref, *, lam_init, tq, tk, hp):
    nq = q_ref.shape[0] // tq
    n_sub = tq // tk
    lam = _lambda(lq1_ref, lk1_ref, lq2_ref, lk2_ref, lam_init)
    contract_last = (((1,), (1,)), ((), ()))
    contract_first = (((0,), (0,)), ((), ()))
    heads = [slice(h * V_DIM, (h + 1) * V_DIM) for h in range(hp)]

    def q_tile(i, _):
        q0 = pl.multiple_of(i * tq, tq)
        qqs = [_stack_queries(q_ref[pl.ds(q0, tq), hs]) for hs in heads]

        def update(k_src, v_src, rows, mask, first):
            scores = [lax.dot_general(k_src[rows, hs], qqs[h], contract_last,
                                      preferred_element_type=F32)
                      for h, hs in enumerate(heads)]
            for h, hs in enumerate(heads):
                s = scores[h]
                if mask is not None:
                    s = jnp.where(mask, s, jnp.finfo(F32).min)
                blk_max = jnp.max(s, axis=0, keepdims=True)
                m_new = blk_max if first else jnp.maximum(m_ref[h], blk_max)
                p = jnp.exp2(s - m_new)
                pv = lax.dot_general(v_src[rows, hs], p.astype(BF16), contract_first,
                                     preferred_element_type=F32)
                psum = jnp.sum(p, axis=0, keepdims=True)
                if first:
                    l_ref[h] = psum
                    acc_ref[h] = pv
                else:
                    alpha = jnp.exp2(m_ref[h] - m_new)
                    l_ref[h] = alpha * l_ref[h] + psum
                    acc_ref[h] = alpha * acc_ref[h] + pv
                m_ref[h] = m_new

        update(km_ref, vm_ref, slice(None), None, True)

        def body(jb, _):
            update(k_ref, v_ref, pl.ds(pl.multiple_of(jb * tk, tk), tk), None, False)
            return 0

        lax.fori_loop(0, i * n_sub, body, 0)

        key = lax.broadcasted_iota(jnp.int32, (tk, 2 * tq), 0)
        qry = lax.broadcasted_iota(jnp.int32, (tk, 2 * tq), 1)
        qry = jnp.where(qry >= tq, qry - tq, qry)
        for d in range(n_sub):
            update(k_ref, v_ref, pl.ds(pl.multiple_of(q0 + d * tk, tk), tk),
                   key + d * tk <= qry, False)

        for h, hs in enumerate(heads):
            acc = acc_ref[h]
            l = l_ref[h]
            o = acc[:, :tq] / l[:, :tq] - lam * (acc[:, tq:] / l[:, tq:])
            o = o * lax.rsqrt(jnp.mean(o * o, axis=0, keepdims=True) + SUBLN_EPS)
            o = o * g_ref[...] * (1.0 - lam_init)
            o_ref[pl.ds(q0, tq), hs] = o.T.astype(BF16)
        return 0

    lax.fori_loop(0, nq, q_tile, 0)


def _attn(lam_params, subln_g_col, proj, proj_meta, *, batch, lam_init, tq, tk, hp):
    t = proj.shape[0]
    seq = t // batch
    assert seq % tq == 0 and tq % tk == 0 and N_HEADS % hp == 0
    groups = N_HEADS // hp
    width = hp * V_DIM
    small = lambda b, g: (0, 0)
    lam_specs = [pl.BlockSpec((1, HEAD_DIM), small)] * 4
    est = 2 * 4 * seq * width * 2 + hp * (8 * tk * 2 * tq * 4 + 4 * V_DIM * 2 * tq * 4)
    return pl.pallas_call(
        functools.partial(_attn_kernel, lam_init=lam_init, tq=tq, tk=tk, hp=hp),
        out_shape=jax.ShapeDtypeStruct((t, ATTN_WIDTH), BF16),
        grid=(batch, groups),
        in_specs=lam_specs + [
            pl.BlockSpec((V_DIM, 1), small),
            pl.BlockSpec((seq, width), lambda b, g: (b, g)),
            pl.BlockSpec((seq, width), lambda b, g: (b, groups + g)),
            pl.BlockSpec((seq, width), lambda b, g: (b, 2 * groups + g)),
            pl.BlockSpec((N_META, width), lambda b, g: (0, groups + g)),
            pl.BlockSpec((N_META, width), lambda b, g: (0, 2 * groups + g)),
        ],
        out_specs=pl.BlockSpec((seq, width), lambda b, g: (b, g)),
        scratch_shapes=[pltpu.VMEM((hp, 1, 2 * tq), F32), pltpu.VMEM((hp, 1, 2 * tq), F32),
                        pltpu.VMEM((hp, V_DIM, 2 * tq), F32)],
        compiler_params=pltpu.CompilerParams(
            dimension_semantics=("parallel", "parallel"),
            vmem_limit_bytes=_vmem_limit(est)),
        name="attn",
    )(*lam_params, subln_g_col, proj, proj, proj, proj_meta, proj_meta)


def _attn_meta_kernel(lq1_ref, lk1_ref, lq2_ref, lk2_ref, g_ref, q_ref, k_ref, v_ref, o_ref,
                      *, lam_init):
    qq = _stack_queries(q_ref[...])
    s = _scores(qq, k_ref[...])
    row = lax.broadcasted_iota(jnp.int32, s.shape, 0) % N_META
    col = lax.broadcasted_iota(jnp.int32, s.shape, 1)
    s = jnp.where(col <= row, s, jnp.finfo(F32).min)
    m = jnp.max(s, axis=-1, keepdims=True)
    p = jnp.exp2(s - m)
    l = jnp.sum(p, axis=-1, keepdims=True)
    acc = jnp.dot(p.astype(BF16), v_ref[...], preferred_element_type=F32)
    lam = _lambda(lq1_ref, lk1_ref, lq2_ref, lk2_ref, lam_init)
    o_ref[...] = _finish(acc, l, lam, g_ref[...], lam_init, N_META).astype(BF16)


def _attn_meta(lam_params, subln_g, proj_meta, *, lam_init):
    small = lambda h: (0, 0)
    return pl.pallas_call(
        functools.partial(_attn_meta_kernel, lam_init=lam_init),
        out_shape=jax.ShapeDtypeStruct((N_META, ATTN_WIDTH), BF16),
        grid=(N_HEADS,),
        in_specs=[pl.BlockSpec((1, HEAD_DIM), small)] * 4 + [
            pl.BlockSpec((1, V_DIM), small),
            pl.BlockSpec((N_META, V_DIM), lambda h: (0, h)),
            pl.BlockSpec((N_META, V_DIM), lambda h: (0, N_HEADS + h)),
            pl.BlockSpec((N_META, V_DIM), lambda h: (0, 2 * N_HEADS + h)),
        ],
        out_specs=pl.BlockSpec((N_META, V_DIM), lambda h: (0, h)),
        compiler_params=pltpu.CompilerParams(dimension_semantics=("arbitrary",)),
        name="attn_meta",
    )(*lam_params, subln_g, proj_meta, proj_meta, proj_meta)


def _outproj_kernel(h_ref, a_ref, d_ref, wp_ref, ps_ref, wo_ref, o_ref):
    diff = d_ref[...]
    pools = []
    for g in range(len(POOL_WINDOWS)):
        cols = slice(g * POOL_GROUP_WIDTH, (g + 1) * POOL_GROUP_WIDTH)
        pools.append(jnp.dot(diff[:, cols], wp_ref[g], preferred_element_type=F32))
    pool = (jnp.concatenate(pools, axis=1) * ps_ref[...]).astype(BF16)
    mix = jnp.concatenate([a_ref[...], pool], axis=1)
    o_ref[...] = h_ref[...] + jnp.dot(mix, wo_ref[...], preferred_element_type=F32)


def _outproj(h, attn, proj, w_pool, pool_scale, w_out, *, tm):
    t = h.shape[0]
    assert t % tm == 0
    est = (2 * 2 * tm * D_MODEL * 4 + 2 * 2 * tm * ATTN_WIDTH * 2 + 2 * D_MODEL * D_MODEL * 2
           + 2 * POOL_WIDTH * POOL_GROUP_WIDTH * 2 + 3 * tm * D_MODEL * 4)
    return pl.pallas_call(
        _outproj_kernel,
        out_shape=jax.ShapeDtypeStruct((t, D_MODEL), F32),
        grid=(t // tm,),
        in_specs=[
            pl.BlockSpec((tm, D_MODEL), lambda i: (i, 0)),
            pl.BlockSpec((tm, ATTN_WIDTH), lambda i: (i, 0)),
            pl.BlockSpec((tm, POOL_WIDTH), lambda i: (i, 3)),
            pl.BlockSpec((len(POOL_WINDOWS), POOL_GROUP_WIDTH, POOL_GROUP_WIDTH),
                         lambda i: (0, 0, 0)),
            pl.BlockSpec((1, POOL_WIDTH), lambda i: (0, 0)),
            pl.BlockSpec((D_MODEL, D_MODEL), lambda i: (0, 0)),
        ],
        out_specs=pl.BlockSpec((tm, D_MODEL), lambda i: (i, 0)),
        compiler_params=pltpu.CompilerParams(
            dimension_semantics=("parallel",),
            vmem_limit_bytes=_vmem_limit(est)),
        name="outproj",
    )(h, attn, proj, w_pool, pool_scale, w_out)


def _rope_tables(length):
    pos = jnp.arange(length, dtype=F32)
    inv_freq = 1.0 / (ROPE_THETA ** (jnp.arange(0, HEAD_DIM, 2, dtype=F32) / HEAD_DIM))
    ang = pos[:, None] * inv_freq[None, :]
    ang = jnp.concatenate([ang, ang, ang, ang], axis=-1)
    sign = jnp.where((jnp.arange(LANES) % HEAD_DIM) < HEAD_DIM // 2, -1.0, 1.0).astype(F32)
    cos, sin = jnp.cos(ang), jnp.sin(ang) * sign
    scale = HEAD_DIM ** -0.5 * math.log2(math.e)
    return cos * scale, sin * scale, cos, sin


def kernel(x, meta_tokens, ffn1_norm_g, ffn1_w_gate, ffn1_w_up, ffn1_w_down, mix_norm_g, w_in,
           lam_q1, lam_k1, lam_q2, lam_k2, subln_g, w_pool, pool_scale, w_out,
           ffn2_norm_g, ffn2_w_gate, ffn2_w_up, ffn2_w_down, final_norm_g):
    batch, seq, d = x.shape
    depth = w_in.shape[0]
    assert d == D_MODEL and meta_tokens.shape == (N_META, D_MODEL)

    tables = _rope_tables(N_META + seq)
    tab_meta = tuple(tb[:N_META] for tb in tables)
    tab_real = tuple(tb[N_META:] for tb in tables)
    final_g = final_norm_g.reshape(1, D_MODEL)

    h = x.reshape(batch * seq, D_MODEL)
    hm = meta_tokens.astype(x.dtype)

    for layer in range(depth):
        lam_init = 0.8 - 0.6 * math.exp(-0.3 * layer)
        last = layer == depth - 1
        wg1, wu1, wd1 = (w[layer].astype(BF16) for w in (ffn1_w_gate, ffn1_w_up, ffn1_w_down))
        wg2, wu2, wd2 = (w[layer].astype(BF16) for w in (ffn2_w_gate, ffn2_w_up, ffn2_w_down))
        wi = w_in[layer].astype(BF16)
        wo = w_out[layer].astype(BF16)
        wp = w_pool[layer].astype(BF16)
        g1 = ffn1_norm_g[layer].reshape(1, D_MODEL)
        gm = mix_norm_g[layer].reshape(1, D_MODEL)
        g2 = ffn2_norm_g[layer].reshape(1, D_MODEL)
        ps = pool_scale[layer].reshape(1, POOL_WIDTH)
        sg = subln_g[layer].reshape(1, V_DIM)
        lam_params = tuple(p[layer].reshape(1, HEAD_DIM) for p in (lam_q1, lam_k1, lam_q2, lam_k2))

        hm = _ffn(hm, g1, wg1, wu1, wd1, final_g, tm=N_META, tf=512, apply_final=False)
        proj_m, u_m = _inproj(hm, gm, wi, tab_meta, None, batch=1, tl=N_META, meta=True)
        attn_m = _attn_meta(lam_params, sg, proj_m, lam_init=lam_init)
        if not last:
            hm = _outproj(hm, attn_m, proj_m, wp, ps, wo, tm=N_META)
            hm = _ffn(hm, g2, wg2, wu2, wd2, final_g, tm=N_META, tf=512, apply_final=False)

        h = _ffn(h, g1, wg1, wu1, wd1, final_g, tm=512, tf=512, apply_final=False)
        proj = _inproj(h, gm, wi, tab_real, u_m, batch=batch, tl=512, meta=False)
        attn = _attn(lam_params, sg.reshape(V_DIM, 1), proj, proj_m, batch=batch,
                     lam_init=lam_init, tq=256, tk=256, hp=8)
        h = _outproj(h, attn, proj, wp, ps, wo, tm=512)
        h = _ffn(h, g2, wg2, wu2, wd2, final_g, tm=512, tf=512, apply_final=last)

    return h.reshape(batch, seq, D_MODEL)
```

```python
import functools
import math

import jax
import jax.numpy as jnp
from jax import lax
from jax.experimental import pallas as pl
from jax.experimental.pallas import tpu as pltpu

F32 = jnp.float32
BF16 = jnp.bfloat16

D_MODEL = 2048
N_META = 16
ATTN_WIDTH = 1024
POOL_WIDTH = 1024
HEAD_DIM = 64
V_DIM = 2 * HEAD_DIM
N_HEADS = ATTN_WIDTH // V_DIM
POOL_WINDOWS = (2, 4, 8, 16)
POOL_GROUP_WIDTH = POOL_WIDTH // len(POOL_WINDOWS)
IN_WIDTH = 3 * ATTN_WIDTH + POOL_WIDTH
D_FF = 5632
ROPE_THETA = 10000.0
NORM_EPS = 1e-6
SUBLN_EPS = 1e-5

LANES = 128
V7X_VMEM_BYTES = 64 * 1024 * 1024


def _vmem_limit(estimate_bytes):
    return int(min(estimate_bytes * 5 // 4 + (4 << 20), V7X_VMEM_BYTES * 7 // 8))


def _rms_norm(x, g, eps):
    return x * lax.rsqrt(jnp.mean(x * x, axis=-1, keepdims=True) + eps) * g


def _ffn_kernel(x_ref, g_ref, wg_ref, wu_ref, wd_ref, fg_ref, o_ref, xn_ref, *, apply_final):
    f = pl.program_id(1)

    @pl.when(f == 0)
    def _():
        x = x_ref[...]
        xn_ref[...] = _rms_norm(x, g_ref[...], NORM_EPS).astype(BF16)
        o_ref[...] = x

    xn = xn_ref[...]
    gate = jnp.dot(xn, wg_ref[...], preferred_element_type=F32)
    up = jnp.dot(xn, wu_ref[...], preferred_element_type=F32)
    act = (gate * jax.nn.sigmoid(gate)) * (0.5 * up)
    o_ref[...] += jnp.dot(act.astype(BF16), wd_ref[...], preferred_element_type=F32)

    if apply_final:
        @pl.when(f == pl.num_programs(1) - 1)
        def _():
            o_ref[...] = _rms_norm(o_ref[...], fg_ref[...], NORM_EPS)


def _ffn(x, g, wg, wu, wd, final_g, *, tm, tf, apply_final):
    t = x.shape[0]
    assert t % tm == 0 and D_FF % tf == 0
    est = (2 * 2 * tm * D_MODEL * 4 + tm * D_MODEL * 2 + 2 * 3 * D_MODEL * tf * 2
           + 3 * tm * tf * 4)
    return pl.pallas_call(
        functools.partial(_ffn_kernel, apply_final=apply_final),
        out_shape=jax.ShapeDtypeStruct((t, D_MODEL), F32),
        grid=(t // tm, D_FF // tf),
        in_specs=[
            pl.BlockSpec((tm, D_MODEL), lambda i, f: (i, 0)),
            pl.BlockSpec((1, D_MODEL), lambda i, f: (0, 0)),
            pl.BlockSpec((D_MODEL, tf), lambda i, f: (0, f)),
            pl.BlockSpec((D_MODEL, tf), lambda i, f: (0, f)),
            pl.BlockSpec((tf, D_MODEL), lambda i, f: (f, 0)),
            pl.BlockSpec((1, D_MODEL), lambda i, f: (0, 0)),
        ],
        out_specs=pl.BlockSpec((tm, D_MODEL), lambda i, f: (i, 0)),
        scratch_shapes=[pltpu.VMEM((tm, D_MODEL), BF16)],
        compiler_params=pltpu.CompilerParams(
            dimension_semantics=("parallel", "arbitrary"),
            vmem_limit_bytes=_vmem_limit(est)),
        name="ffn",
    )(x, g, wg, wu, wd, final_g)


def _rope(x, cos, sin_signed):
    outs = []
    for c in range(x.shape[1] // LANES):
        xc = x[:, c * LANES:(c + 1) * LANES]
        outs.append(xc * cos + pltpu.roll(xc, LANES // 2, axis=1) * sin_signed)
    return jnp.concatenate(outs, axis=1)


def _pool_diff(hist, cur, inv_count_fn):
    run = jnp.concatenate([hist, cur], axis=0)
    outs = []
    for g, w in enumerate(POOL_WINDOWS):
        run = run + pltpu.roll(run, w // 2, axis=0)
        cols = slice(g * POOL_GROUP_WIDTH, (g + 1) * POOL_GROUP_WIDTH)
        outs.append(run[N_META:, :POOL_GROUP_WIDTH] * inv_count_fn(w) - cur[:, cols])
        run = run[:, POOL_GROUP_WIDTH:]
    return jnp.concatenate(outs, axis=1)


def _inproj_kernel(*refs, meta):
    if meta:
        (x_ref, g_ref, w_ref, cq_ref, sq_ref, ck_ref, sk_ref,
         o_ref, u_ref, xn_ref, hist_ref) = refs
    else:
        (x_ref, g_ref, w_ref, cq_ref, sq_ref, ck_ref, sk_ref, um_ref,
         o_ref, xn_ref, hist_ref) = refs
    j = pl.program_id(1)
    n = pl.program_id(2)
    rows = x_ref.shape[0]

    @pl.when(n == 0)
    def _():
        xn_ref[...] = _rms_norm(x_ref[...], g_ref[...], NORM_EPS).astype(BF16)

    if not meta:
        @pl.when((n == 0) & (j == 0))
        def _():
            hist_ref[...] = um_ref[...]

    def project():
        return jnp.dot(xn_ref[...], w_ref[...], preferred_element_type=F32)

    @pl.when(n == 0)
    def _():
        o_ref[...] = _rope(project(), cq_ref[...], sq_ref[...]).astype(BF16)

    @pl.when(n == 1)
    def _():
        o_ref[...] = _rope(project(), ck_ref[...], sk_ref[...]).astype(BF16)

    @pl.when(n == 2)
    def _():
        o_ref[...] = project().astype(BF16)

    @pl.when(n == 3)
    def _():
        acc = project()
        if meta:
            u_ref[...] = acc
            hist = jnp.zeros((N_META, POOL_WIDTH), F32)
            pos = lax.broadcasted_iota(jnp.int32, (rows, 1), 0)

            def inv_count(w):
                return 1.0 / jnp.minimum(pos + 1, w).astype(F32)
        else:
            hist = hist_ref[...]

            def inv_count(w):
                return 1.0 / w
        o_ref[...] = _pool_diff(hist, acc, inv_count).astype(BF16)
        hist_ref[...] = acc[rows - N_META:, :]


def _inproj(x, g, w_in, tables, u_meta, *, batch, tl, meta):
    t = x.shape[0]
    seq = t // batch
    nj = seq // tl
    assert seq % tl == 0
    cq, sq, ck, sk = tables
    tok = lambda b, j, n: (b * nj + j, 0)
    tab = lambda b, j, n: (j, 0)
    in_specs = [
        pl.BlockSpec((tl, D_MODEL), tok),
        pl.BlockSpec((1, D_MODEL), lambda b, j, n: (0, 0)),
        pl.BlockSpec((D_MODEL, ATTN_WIDTH), lambda b, j, n: (0, n)),
        pl.BlockSpec((tl, LANES), tab),
        pl.BlockSpec((tl, LANES), tab),
        pl.BlockSpec((tl, LANES), tab),
        pl.BlockSpec((tl, LANES), tab),
    ]
    args = [x, g, w_in, cq, sq, ck, sk]
    proj_shape = jax.ShapeDtypeStruct((t, IN_WIDTH), BF16)
    proj_spec = pl.BlockSpec((tl, ATTN_WIDTH), lambda b, j, n: (b * nj + j, n))
    if meta:
        out_shape = (proj_shape, jax.ShapeDtypeStruct((t, POOL_WIDTH), F32))
        out_specs = (proj_spec, pl.BlockSpec((tl, POOL_WIDTH), lambda b, j, n: (0, 0)))
    else:
        in_specs.append(pl.BlockSpec((N_META, POOL_WIDTH), lambda b, j, n: (0, 0)))
        args.append(u_meta)
        out_shape = proj_shape
        out_specs = proj_spec
    est = (2 * tl * D_MODEL * 4 + 2 * D_MODEL * ATTN_WIDTH * 2 + 2 * tl * ATTN_WIDTH * 2
           + tl * D_MODEL * 2 + 3 * (tl + N_META) * POOL_WIDTH * 4 + 8 * tl * LANES * 4)
    return pl.pallas_call(
        functools.partial(_inproj_kernel, meta=meta),
        out_shape=out_shape,
        grid=(batch, nj, IN_WIDTH // ATTN_WIDTH),
        in_specs=in_specs,
        out_specs=out_specs,
        scratch_shapes=[pltpu.VMEM((tl, D_MODEL), BF16),
                        pltpu.VMEM((N_META, POOL_WIDTH), F32)],
        compiler_params=pltpu.CompilerParams(
            dimension_semantics=("arbitrary", "arbitrary", "arbitrary"),
            vmem_limit_bytes=_vmem_limit(est)),
        name="inproj_meta" if meta else "inproj",
    )(*args)


def _stack_queries(q):
    lane = lax.broadcasted_iota(jnp.int32, q.shape, 1)
    first_map = (lane % HEAD_DIM) < HEAD_DIM // 2
    zero = jnp.zeros_like(q)
    return jnp.concatenate([jnp.where(first_map, q, zero), jnp.where(first_map, zero, q)], axis=0)


def _scores(qq, k):
    return lax.dot_general(qq, k, (((1,), (1,)), ((), ())), preferred_element_type=F32)


def _lambda(lq1_ref, lk1_ref, lq2_ref, lk2_ref, lam_init):
    s1 = jnp.sum(lq1_ref[...] * lk1_ref[...], axis=-1, keepdims=True)
    s2 = jnp.sum(lq2_ref[...] * lk2_ref[...], axis=-1, keepdims=True)
    return jnp.exp(s1) - jnp.exp(s2) + lam_init


def _finish(acc, l, lam, g, lam_init, tq):
    o = acc[:tq] / l[:tq] - lam * (acc[tq:] / l[tq:])
    return _rms_norm(o, g, SUBLN_EPS) * (1.0 - lam_init)


def _attn_kernel(lq1_ref, lk1_ref, lq2_ref, lk2_ref, g_ref, q_ref, k_ref, v_ref, km_ref, vm_ref,
                 o_ref, m_ref, l_ref, acc_ref, *, lam_init, tq, tk, hp):
    nq = q_ref.shape[0] // tq
    n_sub = tq // tk
    lam = _lambda(lq1_ref, lk1_ref, lq2_ref, lk2_ref, lam_init)
    contract_last = (((1,), (1,)), ((), ()))
    contract_first = (((0,), (0,)), ((), ()))
    heads = [slice(h * V_DIM, (h + 1) * V_DIM) for h in range(hp)]

    def q_tile(i, _):
        q0 = pl.multiple_of(i * tq, tq)
        qqs = [_stack_queries(q_ref[pl.ds(q0, tq), hs]) for hs in heads]

        def update(k_src, v_src, rows, mask, first):
            scores = [lax.dot_general(k_src[rows, hs], qqs[h], contract_last,
                                      preferred_element_type=F32)
                      for h, hs in enumerate(heads)]
            for h, hs in enumerate(heads):
                s = scores[h]
                if mask is not None:
                    s = jnp.where(mask, s, jnp.finfo(F32).min)
                blk_max = jnp.max(s, axis=0, keepdims=True)
                m_new = blk_max if first else jnp.maximum(m_ref[h], blk_max)
                p = jnp.exp2(s - m_new)
                pv = lax.dot_general(v_src[rows, hs], p.astype(BF16), contract_first,
                                     preferred_element_type=F32)
                psum = jnp.sum(p, axis=0, keepdims=True)
                if first:
                    l_ref[h] = psum
                    acc_ref[h] = pv
                else:
                    alpha = jnp.exp2(m_ref[h] - m_new)
                    l_ref[h] = alpha * l_ref[h] + psum
                    acc_ref[h] = alpha * acc_ref[h] + pv
                m_ref[h] = m_new

        update(km_ref, vm_ref, slice(None), None, True)

        def body(jb, _):
            update(k_ref, v_ref, pl.ds(pl.multiple_of(jb * tk, tk), tk), None, False)
            return 0

        lax.fori_loop(0, i * n_sub, body, 0)

        key = lax.broadcasted_iota(jnp.int32, (tk, 2 * tq), 0)
        qry = lax.broadcasted_iota(jnp.int32, (tk, 2 * tq), 1)
        qry = jnp.where(qry >= tq, qry - tq, qry)
        for d in range(n_sub):
            update(k_ref, v_ref, pl.ds(pl.multiple_of(q0 + d * tk, tk), tk),
                   key + d * tk <= qry, False)

        for h, hs in enumerate(heads):
            acc = acc_ref[h]
            l = l_ref[h]
            o = acc[:, :tq] / l[:, :tq] - lam * (acc[:, tq:] / l[:, tq:])
            o = o * lax.rsqrt(jnp.mean(o * o, axis=0, keepdims=True) + SUBLN_EPS)
            o = o * g_ref[...] * (1.0 - lam_init)
            o_ref[pl.ds(q0, tq), hs] = o.T.astype(BF16)
        return 0

    lax.fori_loop(0, nq, q_tile, 0)


def _attn(lam_params, subln_g_col, proj, proj_meta, *, batch, lam_init, tq, tk, hp):
    t = proj.shape[0]
    seq = t // batch
    assert seq % tq == 0 and tq % tk == 0 and N_HEADS % hp == 0
    groups = N_HEADS // hp
    width = hp * V_DIM
    small = lambda b, g: (0, 0)
    lam_specs = [pl.BlockSpec((1, HEAD_DIM), small)] * 4
    est = 2 * 4 * seq * width * 2 + hp * (8 * tk * 2 * tq * 4 + 4 * V_DIM * 2 * tq * 4)
    return pl.pallas_call(
        functools.partial(_attn_kernel, lam_init=lam_init, tq=tq, tk=tk, hp=hp),
        out_shape=jax.ShapeDtypeStruct((t, ATTN_WIDTH), BF16),
        grid=(batch, groups),
        in_specs=lam_specs + [
            pl.BlockSpec((V_DIM, 1), small),
            pl.BlockSpec((seq, width), lambda b, g: (b, g)),
            pl.BlockSpec((seq, width), lambda b, g: (b, groups + g)),
            pl.BlockSpec((seq, width), lambda b, g: (b, 2 * groups + g)),
            pl.BlockSpec((N_META, width), lambda b, g: (0, groups + g)),
            pl.BlockSpec((N_META, width), lambda b, g: (0, 2 * groups + g)),
        ],
        out_specs=pl.BlockSpec((seq, width), lambda b, g: (b, g)),
        scratch_shapes=[pltpu.VMEM((hp, 1, 2 * tq), F32), pltpu.VMEM((hp, 1, 2 * tq), F32),
                        pltpu.VMEM((hp, V_DIM, 2 * tq), F32)],
        compiler_params=pltpu.CompilerParams(
            dimension_semantics=("parallel", "parallel"),
            vmem_limit_bytes=_vmem_limit(est)),
        name="attn",
    )(*lam_params, subln_g_col, proj, proj, proj, proj_meta, proj_meta)


def _attn_meta_kernel(lq1_ref, lk1_ref, lq2_ref, lk2_ref, g_ref, q_ref, k_ref, v_ref, o_ref,
                      *, lam_init):
    qq = _stack_queries(q_ref[...])
    s = _scores(qq, k_ref[...])
    row = lax.broadcasted_iota(jnp.int32, s.shape, 0) % N_META
    col = lax.broadcasted_iota(jnp.int32, s.shape, 1)
    s = jnp.where(col <= row, s, jnp.finfo(F32).min)
    m = jnp.max(s, axis=-1, keepdims=True)
    p = jnp.exp2(s - m)
    l = jnp.sum(p, axis=-1, keepdims=True)
    acc = jnp.dot(p.astype(BF16), v_ref[...], preferred_element_type=F32)
    lam = _lambda(lq1_ref, lk1_ref, lq2_ref, lk2_ref, lam_init)
    o_ref[...] = _finish(acc, l, lam, g_ref[...], lam_init, N_META).astype(BF16)


def _attn_meta(lam_params, subln_g, proj_meta, *, lam_init):
    small = lambda h: (0, 0)
    return pl.pallas_call(
        functools.partial(_attn_meta_kernel, lam_init=lam_init),
        out_shape=jax.ShapeDtypeStruct((N_META, ATTN_WIDTH), BF16),
        grid=(N_HEADS,),
        in_specs=[pl.BlockSpec((1, HEAD_DIM), small)] * 4 + [
            pl.BlockSpec((1, V_DIM), small),
            pl.BlockSpec((N_META, V_DIM), lambda h: (0, h)),
            pl.BlockSpec((N_META, V_DIM), lambda h: (0, N_HEADS + h)),
            pl.BlockSpec((N_META, V_DIM), lambda h: (0, 2 * N_HEADS + h)),
        ],
        out_specs=pl.BlockSpec((N_META, V_DIM), lambda h: (0, h)),
        compiler_params=pltpu.CompilerParams(dimension_semantics=("arbitrary",)),
        name="attn_meta",
    )(*lam_params, subln_g, proj_meta, proj_meta, proj_meta)


def _outproj_kernel(h_ref, a_ref, d_ref, wp_ref, ps_ref, wo_ref, o_ref):
    diff = d_ref[...]
    pools = []
    for g in range(len(POOL_WINDOWS)):
        cols = slice(g * POOL_GROUP_WIDTH, (g + 1) * POOL_GROUP_WIDTH)
        pools.append(jnp.dot(diff[:, cols], wp_ref[g], preferred_element_type=F32))
    pool = (jnp.concatenate(pools, axis=1) * ps_ref[...]).astype(BF16)
    mix = jnp.concatenate([a_ref[...], pool], axis=1)
    o_ref[...] = h_ref[...] + jnp.dot(mix, wo_ref[...], preferred_element_type=F32)


def _outproj(h, attn, proj, w_pool, pool_scale, w_out, *, tm):
    t = h.shape[0]
    assert t % tm == 0
    est = (2 * 2 * tm * D_MODEL * 4 + 2 * 2 * tm * ATTN_WIDTH * 2 + 2 * D_MODEL * D_MODEL * 2
           + 2 * POOL_WIDTH * POOL_GROUP_WIDTH * 2 + 3 * tm * D_MODEL * 4)
    return pl.pallas_call(
        _outproj_kernel,
        out_shape=jax.ShapeDtypeStruct((t, D_MODEL), F32),
        grid=(t // tm,),
        in_specs=[
            pl.BlockSpec((tm, D_MODEL), lambda i: (i, 0)),
            pl.BlockSpec((tm, ATTN_WIDTH), lambda i: (i, 0)),
            pl.BlockSpec((tm, POOL_WIDTH), lambda i: (i, 3)),
            pl.BlockSpec((len(POOL_WINDOWS), POOL_GROUP_WIDTH, POOL_GROUP_WIDTH),
                         lambda i: (0, 0, 0)),
            pl.BlockSpec((1, POOL_WIDTH), lambda i: (0, 0)),
            pl.BlockSpec((D_MODEL, D_MODEL), lambda i: (0, 0)),
        ],
        out_specs=pl.BlockSpec((tm, D_MODEL), lambda i: (i, 0)),
        compiler_params=pltpu.CompilerParams(
            dimension_semantics=("parallel",),
            vmem_limit_bytes=_vmem_limit(est)),
        name="outproj",
    )(h, attn, proj, w_pool, pool_scale, w_out)


def _to_head_layout(w_in):
    half = HEAD_DIM // 2
    qk = w_in[:, :2 * ATTN_WIDTH].reshape(D_MODEL, 2 * N_HEADS, 2, 2, half)
    qk = qk.transpose(0, 1, 3, 2, 4).reshape(D_MODEL, 2 * ATTN_WIDTH)
    return jnp.concatenate([qk, w_in[:, 2 * ATTN_WIDTH:]], axis=1)


def _rope_tables(length):
    pos = jnp.arange(length, dtype=F32)
    inv_freq = 1.0 / (ROPE_THETA ** (jnp.arange(0, HEAD_DIM, 2, dtype=F32) / HEAD_DIM))
    ang = pos[:, None] * inv_freq[None, :]
    ang = jnp.concatenate([ang, ang, ang, ang], axis=-1)
    sign = jnp.where(jnp.arange(LANES) < LANES // 2, -1.0, 1.0).astype(F32)
    cos, sin = jnp.cos(ang), jnp.sin(ang) * sign
    scale = HEAD_DIM ** -0.5 * math.log2(math.e)
    return cos * scale, sin * scale, cos, sin


def kernel(x, meta_tokens, ffn1_norm_g, ffn1_w_gate, ffn1_w_up, ffn1_w_down, mix_norm_g, w_in,
           lam_q1, lam_k1, lam_q2, lam_k2, subln_g, w_pool, pool_scale, w_out,
           ffn2_norm_g, ffn2_w_gate, ffn2_w_up, ffn2_w_down, final_norm_g):
    batch, seq, d = x.shape
    depth = w_in.shape[0]
    assert d == D_MODEL and meta_tokens.shape == (N_META, D_MODEL)

    tables = _rope_tables(N_META + seq)
    tab_meta = tuple(tb[:N_META] for tb in tables)
    tab_real = tuple(tb[N_META:] for tb in tables)
    final_g = final_norm_g.reshape(1, D_MODEL)

    h = x.reshape(batch * seq, D_MODEL)
    hm = meta_tokens.astype(x.dtype)

    for layer in range(depth):
        lam_init = 0.8 - 0.6 * math.exp(-0.3 * layer)
        last = layer == depth - 1
        wg1, wu1, wd1 = (w[layer].astype(BF16) for w in (ffn1_w_gate, ffn1_w_up, ffn1_w_down))
        wg2, wu2, wd2 = (w[layer].astype(BF16) for w in (ffn2_w_gate, ffn2_w_up, ffn2_w_down))
        wi = _to_head_layout(w_in[layer]).astype(BF16)
        wo = w_out[layer].astype(BF16)
        wp = w_pool[layer].astype(BF16)
        g1 = ffn1_norm_g[layer].reshape(1, D_MODEL)
        gm = mix_norm_g[layer].reshape(1, D_MODEL)
        g2 = ffn2_norm_g[layer].reshape(1, D_MODEL)
        ps = pool_scale[layer].reshape(1, POOL_WIDTH)
        sg = subln_g[layer].reshape(1, V_DIM)
        lam_params = tuple(p[layer].reshape(1, HEAD_DIM) for p in (lam_q1, lam_k1, lam_q2, lam_k2))

        hm = _ffn(hm, g1, wg1, wu1, wd1, final_g, tm=N_META, tf=512, apply_final=False)
        proj_m, u_m = _inproj(hm, gm, wi, tab_meta, None, batch=1, tl=N_META, meta=True)
        attn_m = _attn_meta(lam_params, sg, proj_m, lam_init=lam_init)
        if not last:
            hm = _outproj(hm, attn_m, proj_m, wp, ps, wo, tm=N_META)
            hm = _ffn(hm, g2, wg2, wu2, wd2, final_g, tm=N_META, tf=512, apply_final=False)

        h = _ffn(h, g1, wg1, wu1, wd1, final_g, tm=1024, tf=512, apply_final=False)
        proj = _inproj(h, gm, wi, tab_real, u_m, batch=batch, tl=512, meta=False)
        attn = _attn(lam_params, sg.reshape(V_DIM, 1), proj, proj_m, batch=batch,
                     lam_init=lam_init, tq=256, tk=256, hp=8)
        h = _outproj(h, attn, proj, wp, ps, wo, tm=512)
        h = _ffn(h, g2, wg2, wu2, wd2, final_g, tm=1024, tf=512, apply_final=last)

    return h.reshape(batch, seq, D_MODEL)
```

```python
import functools
import math

import jax
import jax.numpy as jnp
from jax import lax
from jax.experimental import pallas as pl
from jax.experimental.pallas import tpu as pltpu

F32 = jnp.float32
BF16 = jnp.bfloat16

D_MODEL = 2048
N_META = 16
ATTN_WIDTH = 1024
POOL_WIDTH = 1024
HEAD_DIM = 64
V_DIM = 2 * HEAD_DIM
N_HEADS = ATTN_WIDTH // V_DIM
POOL_WINDOWS = (2, 4, 8, 16)
POOL_GROUP_WIDTH = POOL_WIDTH // len(POOL_WINDOWS)
IN_WIDTH = 3 * ATTN_WIDTH + POOL_WIDTH
D_FF = 5632
ROPE_THETA = 10000.0
NORM_EPS = 1e-6
SUBLN_EPS = 1e-5

LANES = 128
V7X_VMEM_BYTES = 64 * 1024 * 1024


def _vmem_limit(estimate_bytes):
    return int(min(estimate_bytes * 5 // 4 + (4 << 20), V7X_VMEM_BYTES * 7 // 8))


def _rms_norm(x, g, eps):
    return x * lax.rsqrt(jnp.mean(x * x, axis=-1, keepdims=True) + eps) * g


def _ffn_kernel(x_ref, g_ref, wg_ref, wu_ref, wd_ref, fg_ref, o_ref, xn_ref, *, apply_final):
    f = pl.program_id(1)

    @pl.when(f == 0)
    def _():
        x = x_ref[...]
        xn_ref[...] = _rms_norm(x, g_ref[...], NORM_EPS).astype(BF16)
        o_ref[...] = x

    xn = xn_ref[...]
    gate = jnp.dot(xn, wg_ref[...], preferred_element_type=F32)
    up = jnp.dot(xn, wu_ref[...], preferred_element_type=F32)
    act = (gate * jax.nn.sigmoid(gate)) * (0.5 * up)
    o_ref[...] += jnp.dot(act.astype(BF16), wd_ref[...], preferred_element_type=F32)

    if apply_final:
        @pl.when(f == pl.num_programs(1) - 1)
        def _():
            o_ref[...] = _rms_norm(o_ref[...], fg_ref[...], NORM_EPS)


def _ffn(x, g, wg, wu, wd, final_g, *, tm, tf, apply_final):
    t = x.shape[0]
    assert t % tm == 0 and D_FF % tf == 0
    est = (2 * 2 * tm * D_MODEL * 4 + tm * D_MODEL * 2 + 2 * 3 * D_MODEL * tf * 2
           + 3 * tm * tf * 4)
    return pl.pallas_call(
        functools.partial(_ffn_kernel, apply_final=apply_final),
        out_shape=jax.ShapeDtypeStruct((t, D_MODEL), F32),
        grid=(t // tm, D_FF // tf),
        in_specs=[
            pl.BlockSpec((tm, D_MODEL), lambda i, f: (i, 0)),
            pl.BlockSpec((1, D_MODEL), lambda i, f: (0, 0)),
            pl.BlockSpec((D_MODEL, tf), lambda i, f: (0, f)),
            pl.BlockSpec((D_MODEL, tf), lambda i, f: (0, f)),
            pl.BlockSpec((tf, D_MODEL), lambda i, f: (f, 0)),
            pl.BlockSpec((1, D_MODEL), lambda i, f: (0, 0)),
        ],
        out_specs=pl.BlockSpec((tm, D_MODEL), lambda i, f: (i, 0)),
        scratch_shapes=[pltpu.VMEM((tm, D_MODEL), BF16)],
        compiler_params=pltpu.CompilerParams(
            dimension_semantics=("parallel", "arbitrary"),
            vmem_limit_bytes=_vmem_limit(est)),
        name="ffn",
    )(x, g, wg, wu, wd, final_g)


def _rope(x, cos, sin_signed):
    outs = []
    for c in range(x.shape[1] // LANES):
        xc = x[:, c * LANES:(c + 1) * LANES]
        outs.append(xc * cos + pltpu.roll(xc, LANES // 2, axis=1) * sin_signed)
    return jnp.concatenate(outs, axis=1)


def _pool_diff(hist, cur, inv_count_fn):
    run = jnp.concatenate([hist, cur], axis=0)
    outs = []
    for g, w in enumerate(POOL_WINDOWS):
        run = run + pltpu.roll(run, w // 2, axis=0)
        cols = slice(g * POOL_GROUP_WIDTH, (g + 1) * POOL_GROUP_WIDTH)
        outs.append(run[N_META:, :POOL_GROUP_WIDTH] * inv_count_fn(w) - cur[:, cols])
        run = run[:, POOL_GROUP_WIDTH:]
    return jnp.concatenate(outs, axis=1)


def _inproj_kernel(*refs, meta):
    if meta:
        (x_ref, g_ref, w_ref, cq_ref, sq_ref, ck_ref, sk_ref,
         o_ref, u_ref, hist_ref) = refs
    else:
        (x_ref, g_ref, w_ref, cq_ref, sq_ref, ck_ref, sk_ref, um_ref,
         o_ref, hist_ref) = refs
    j = pl.program_id(1)
    rows = x_ref.shape[0]
    chunk = lambda n: slice(n * ATTN_WIDTH, (n + 1) * ATTN_WIDTH)

    if not meta:
        @pl.when(j == 0)
        def _():
            hist_ref[...] = um_ref[...]

    xn = _rms_norm(x_ref[...], g_ref[...], NORM_EPS).astype(BF16)

    def project(n):
        return jnp.dot(xn, w_ref[:, chunk(n)], preferred_element_type=F32)

    o_ref[:, chunk(0)] = _rope(project(0), cq_ref[...], sq_ref[...]).astype(BF16)
    o_ref[:, chunk(1)] = _rope(project(1), ck_ref[...], sk_ref[...]).astype(BF16)
    o_ref[:, chunk(2)] = project(2).astype(BF16)

    acc = project(3)
    if meta:
        u_ref[...] = acc
        hist = jnp.zeros((N_META, POOL_WIDTH), F32)
        pos = lax.broadcasted_iota(jnp.int32, (rows, 1), 0)

        def inv_count(w):
            return 1.0 / jnp.minimum(pos + 1, w).astype(F32)
    else:
        hist = hist_ref[...]

        def inv_count(w):
            return 1.0 / w
    o_ref[:, chunk(3)] = _pool_diff(hist, acc, inv_count).astype(BF16)
    hist_ref[...] = acc[rows - N_META:, :]


def _inproj(x, g, w_in, tables, u_meta, *, batch, tl, meta):
    t = x.shape[0]
    seq = t // batch
    nj = seq // tl
    assert seq % tl == 0
    cq, sq, ck, sk = tables
    tok = lambda b, j: (b * nj + j, 0)
    tab = lambda b, j: (j, 0)
    const = lambda b, j: (0, 0)
    in_specs = [
        pl.BlockSpec((tl, D_MODEL), tok),
        pl.BlockSpec((1, D_MODEL), const),
        pl.BlockSpec((D_MODEL, IN_WIDTH), const, pipeline_mode=pl.Buffered(1)),
        pl.BlockSpec((tl, LANES), tab),
        pl.BlockSpec((tl, LANES), tab),
        pl.BlockSpec((tl, LANES), tab),
        pl.BlockSpec((tl, LANES), tab),
    ]
    args = [x, g, w_in, cq, sq, ck, sk]
    proj_shape = jax.ShapeDtypeStruct((t, IN_WIDTH), BF16)
    proj_spec = pl.BlockSpec((tl, IN_WIDTH), tok)
    if meta:
        out_shape = (proj_shape, jax.ShapeDtypeStruct((t, POOL_WIDTH), F32))
        out_specs = (proj_spec, pl.BlockSpec((tl, POOL_WIDTH), const))
    else:
        in_specs.append(pl.BlockSpec((N_META, POOL_WIDTH), const))
        args.append(u_meta)
        out_shape = proj_shape
        out_specs = proj_spec
    est = (2 * tl * D_MODEL * 4 + D_MODEL * IN_WIDTH * 2 + 2 * tl * IN_WIDTH * 2
           + tl * D_MODEL * 2 + 4 * (tl + N_META) * POOL_WIDTH * 4 + 8 * tl * LANES * 4)
    return pl.pallas_call(
        functools.partial(_inproj_kernel, meta=meta),
        out_shape=out_shape,
        grid=(batch, nj),
        in_specs=in_specs,
        out_specs=out_specs,
        scratch_shapes=[pltpu.VMEM((N_META, POOL_WIDTH), F32)],
        compiler_params=pltpu.CompilerParams(
            dimension_semantics=("arbitrary", "arbitrary"),
            vmem_limit_bytes=_vmem_limit(est)),
        name="inproj_meta" if meta else "inproj",
    )(*args)


def _stack_queries(q):
    lane = lax.broadcasted_iota(jnp.int32, q.shape, 1)
    first_map = (lane % HEAD_DIM) < HEAD_DIM // 2
    zero = jnp.zeros_like(q)
    return jnp.concatenate([jnp.where(first_map, q, zero), jnp.where(first_map, zero, q)], axis=0)


def _scores(qq, k):
    return lax.dot_general(qq, k, (((1,), (1,)), ((), ())), preferred_element_type=F32)


def _lambda(lq1_ref, lk1_ref, lq2_ref, lk2_ref, lam_init):
    s1 = jnp.sum(lq1_ref[...] * lk1_ref[...], axis=-1, keepdims=True)
    s2 = jnp.sum(lq2_ref[...] * lk2_ref[...], axis=-1, keepdims=True)
    return jnp.exp(s1) - jnp.exp(s2) + lam_init


def _finish(acc, l, lam, g, lam_init, tq):
    o = acc[:tq] / l[:tq] - lam * (acc[tq:] / l[tq:])
    return _rms_norm(o, g, SUBLN_EPS) * (1.0 - lam_init)


def _attn_kernel(lq1_ref, lk1_ref, lq2_ref, lk2_ref, g_ref, q_ref, k_ref, v_ref, km_ref, vm_ref,
                 o_ref, qq_ref, m_ref, l_ref, acc_ref, s0_ref, s1_ref, *, lam_init, tq, hp):
    nq = q_ref.shape[0] // tq
    lam = _lambda(lq1_ref, lk1_ref, lq2_ref, lk2_ref, lam_init)
    contract_last = (((1,), (1,)), ((), ()))
    contract_first = (((0,), (0,)), ((), ()))
    heads = [slice(h * V_DIM, (h + 1) * V_DIM) for h in range(hp)]

    def block_rows(b):
        return pl.ds(pl.multiple_of(b * tq, tq), tq)

    def scores(k, h):
        return lax.dot_general(k, qq_ref[h], contract_last, preferred_element_type=F32)

    def consume(h, s, v, first):
        blk_max = jnp.max(s, axis=0, keepdims=True)
        m_new = blk_max if first else jnp.maximum(m_ref[h], blk_max)
        p = jnp.exp2(s - m_new)
        pv = lax.dot_general(v, p.astype(BF16), contract_first, preferred_element_type=F32)
        psum = jnp.sum(p, axis=0, keepdims=True)
        if first:
            l_ref[h] = psum
            acc_ref[h] = pv
        else:
            alpha = jnp.exp2(m_ref[h] - m_new)
            l_ref[h] = alpha * l_ref[h] + psum
            acc_ref[h] = alpha * acc_ref[h] + pv
        m_ref[h] = m_new

    def interleave(producers, consumers, lead, per_consumer):
        producers = list(producers)
        for task in producers[:lead]:
            task()
        rest = producers[lead:]
        for c, task in enumerate(consumers):
            task()
            for extra in rest[c * per_consumer:(c + 1) * per_consumer]:
                extra()
        for extra in rest[len(consumers) * per_consumer:]:
            extra()

    def step(b, cur_ref, nxt_ref):
        def prefetch(h, hs):
            def run():
                nxt_ref[h] = scores(k_ref[block_rows(b + 1), hs], h)
            return run

        def use(h, hs):
            return lambda: consume(h, cur_ref[h], v_ref[block_rows(b), hs], False)

        producers = [prefetch(h, hs) for h, hs in enumerate(heads)] if nxt_ref is not None else []
        interleave(producers, [use(h, hs) for h, hs in enumerate(heads)], 2, 1)

    def q_tile(i, _):
        rows = block_rows(i)
        for h, hs in enumerate(heads):
            qq_ref[h] = _stack_queries(q_ref[rows, hs])

        key = lax.broadcasted_iota(jnp.int32, (N_META + tq, 2 * tq), 0) - N_META
        qry = lax.broadcasted_iota(jnp.int32, (N_META + tq, 2 * tq), 1)
        visible = key <= jnp.where(qry >= tq, qry - tq, qry)
        diag = [None] * hp

        def diag_scores(h, hs):
            def run():
                diag[h] = scores(jnp.concatenate([km_ref[:, hs], k_ref[rows, hs]], axis=0), h)
            return run

        def first_scores(h, hs):
            def run():
                s0_ref[h] = scores(k_ref[block_rows(0), hs], h)
            return run

        def use_diag(h, hs):
            return lambda: consume(h, jnp.where(visible, diag[h], jnp.finfo(F32).min),
                                   jnp.concatenate([vm_ref[:, hs], v_ref[rows, hs]], axis=0), True)

        producers = [diag_scores(h, hs) for h, hs in enumerate(heads)]
        producers = producers[:2] + [t for h, hs in enumerate(heads)
                                     for t in (producers[h + 2:h + 3] + [first_scores(h, hs)])]
        interleave(producers, [use_diag(h, hs) for h, hs in enumerate(heads)], 2, 2)

        def body(b, _):
            @pl.when(b % 2 == 0)
            def _():
                step(b, s0_ref, s1_ref)

            @pl.when(b % 2 == 1)
            def _():
                step(b, s1_ref, s0_ref)
            return 0

        lax.fori_loop(0, i - 1, body, 0)

        @pl.when((i >= 1) & (i % 2 == 1))
        def _():
            step(i - 1, s0_ref, None)

        @pl.when((i >= 1) & (i % 2 == 0))
        def _():
            step(i - 1, s1_ref, None)

        for h, hs in enumerate(heads):
            acc = acc_ref[h]
            l = l_ref[h]
            o = acc[:, :tq] / l[:, :tq] - lam * (acc[:, tq:] / l[:, tq:])
            o = o * lax.rsqrt(jnp.mean(o * o, axis=0, keepdims=True) + SUBLN_EPS)
            o = o * g_ref[...] * (1.0 - lam_init)
            o_ref[rows, hs] = o.T.astype(BF16)
        return 0

    lax.fori_loop(0, nq, q_tile, 0)


def _attn(lam_params, subln_g_col, proj, proj_meta, *, batch, lam_init, tq, hp):
    t = proj.shape[0]
    seq = t // batch
    assert seq % tq == 0 and N_HEADS % hp == 0
    groups = N_HEADS // hp
    width = hp * V_DIM
    small = lambda b, g: (0, 0)
    lam_specs = [pl.BlockSpec((1, HEAD_DIM), small)] * 4
    score_buf = pltpu.VMEM((hp, tq, 2 * tq), F32)
    est = (2 * 4 * seq * width * 2 + hp * (2 * tq * V_DIM * 2 + 2 * tq * 2 * tq * 4
                                           + 8 * tq * 2 * tq * 4 // hp + 2 * V_DIM * 2 * tq * 4))
    return pl.pallas_call(
        functools.partial(_attn_kernel, lam_init=lam_init, tq=tq, hp=hp),
        out_shape=jax.ShapeDtypeStruct((t, ATTN_WIDTH), BF16),
        grid=(batch, groups),
        in_specs=lam_specs + [
            pl.BlockSpec((V_DIM, 1), small),
            pl.BlockSpec((seq, width), lambda b, g: (b, g)),
            pl.BlockSpec((seq, width), lambda b, g: (b, groups + g)),
            pl.BlockSpec((seq, width), lambda b, g: (b, 2 * groups + g)),
            pl.BlockSpec((N_META, width), lambda b, g: (0, groups + g)),
            pl.BlockSpec((N_META, width), lambda b, g: (0, 2 * groups + g)),
        ],
        out_specs=pl.BlockSpec((seq, width), lambda b, g: (b, g)),
        scratch_shapes=[pltpu.VMEM((hp, 2 * tq, V_DIM), BF16),
                        pltpu.VMEM((hp, 1, 2 * tq), F32), pltpu.VMEM((hp, 1, 2 * tq), F32),
                        pltpu.VMEM((hp, V_DIM, 2 * tq), F32), score_buf, score_buf],
        compiler_params=pltpu.CompilerParams(
            dimension_semantics=("parallel", "parallel"),
            vmem_limit_bytes=_vmem_limit(est)),
        name="attn",
    )(*lam_params, subln_g_col, proj, proj, proj, proj_meta, proj_meta)


def _attn_meta_kernel(lq1_ref, lk1_ref, lq2_ref, lk2_ref, g_ref, q_ref, k_ref, v_ref, o_ref,
                      *, lam_init):
    qq = _stack_queries(q_ref[...])
    s = _scores(qq, k_ref[...])
    row = lax.broadcasted_iota(jnp.int32, s.shape, 0) % N_META
    col = lax.broadcasted_iota(jnp.int32, s.shape, 1)
    s = jnp.where(col <= row, s, jnp.finfo(F32).min)
    m = jnp.max(s, axis=-1, keepdims=True)
    p = jnp.exp2(s - m)
    l = jnp.sum(p, axis=-1, keepdims=True)
    acc = jnp.dot(p.astype(BF16), v_ref[...], preferred_element_type=F32)
    lam = _lambda(lq1_ref, lk1_ref, lq2_ref, lk2_ref, lam_init)
    o_ref[...] = _finish(acc, l, lam, g_ref[...], lam_init, N_META).astype(BF16)


def _attn_meta(lam_params, subln_g, proj_meta, *, lam_init):
    small = lambda h: (0, 0)
    return pl.pallas_call(
        functools.partial(_attn_meta_kernel, lam_init=lam_init),
        out_shape=jax.ShapeDtypeStruct((N_META, ATTN_WIDTH), BF16),
        grid=(N_HEADS,),
        in_specs=[pl.BlockSpec((1, HEAD_DIM), small)] * 4 + [
            pl.BlockSpec((1, V_DIM), small),
            pl.BlockSpec((N_META, V_DIM), lambda h: (0, h)),
            pl.BlockSpec((N_META, V_DIM), lambda h: (0, N_HEADS + h)),
            pl.BlockSpec((N_META, V_DIM), lambda h: (0, 2 * N_HEADS + h)),
        ],
        out_specs=pl.BlockSpec((N_META, V_DIM), lambda h: (0, h)),
        compiler_params=pltpu.CompilerParams(dimension_semantics=("arbitrary",)),
        name="attn_meta",
    )(*lam_params, subln_g, proj_meta, proj_meta, proj_meta)


def _outproj_kernel(h_ref, a_ref, d_ref, wp_ref, ps_ref, wo_ref, o_ref):
    diff = d_ref[...]
    pools = []
    for g in range(len(POOL_WINDOWS)):
        cols = slice(g * POOL_GROUP_WIDTH, (g + 1) * POOL_GROUP_WIDTH)
        pools.append(jnp.dot(diff[:, cols], wp_ref[g], preferred_element_type=F32))
    pool = (jnp.concatenate(pools, axis=1) * ps_ref[...]).astype(BF16)
    mix = jnp.concatenate([a_ref[...], pool], axis=1)
    o_ref[...] = h_ref[...] + jnp.dot(mix, wo_ref[...], preferred_element_type=F32)


def _outproj(h, attn, proj, w_pool, pool_scale, w_out, *, tm):
    t = h.shape[0]
    assert t % tm == 0
    est = (2 * 2 * tm * D_MODEL * 4 + 2 * 2 * tm * ATTN_WIDTH * 2 + 2 * D_MODEL * D_MODEL * 2
           + 2 * POOL_WIDTH * POOL_GROUP_WIDTH * 2 + 3 * tm * D_MODEL * 4)
    return pl.pallas_call(
        _outproj_kernel,
        out_shape=jax.ShapeDtypeStruct((t, D_MODEL), F32),
        grid=(t // tm,),
        in_specs=[
            pl.BlockSpec((tm, D_MODEL), lambda i: (i, 0)),
            pl.BlockSpec((tm, ATTN_WIDTH), lambda i: (i, 0)),
            pl.BlockSpec((tm, POOL_WIDTH), lambda i: (i, 3)),
            pl.BlockSpec((len(POOL_WINDOWS), POOL_GROUP_WIDTH, POOL_GROUP_WIDTH),
                         lambda i: (0, 0, 0)),
            pl.BlockSpec((1, POOL_WIDTH), lambda i: (0, 0)),
            pl.BlockSpec((D_MODEL, D_MODEL), lambda i: (0, 0)),
        ],
        out_specs=pl.BlockSpec((tm, D_MODEL), lambda i: (i, 0)),
        compiler_params=pltpu.CompilerParams(
            dimension_semantics=("parallel",),
            vmem_limit_bytes=_vmem_limit(est)),
        name="outproj",
    )(h, attn, proj, w_pool, pool_scale, w_out)


def _to_head_layout(w_in):
    half = HEAD_DIM // 2
    qk = w_in[:, :2 * ATTN_WIDTH].reshape(D_MODEL, 2 * N_HEADS, 2, 2, half)
    qk = qk.transpose(0, 1, 3, 2, 4).reshape(D_MODEL, 2 * ATTN_WIDTH)
    return jnp.concatenate([qk, w_in[:, 2 * ATTN_WIDTH:]], axis=1)


def _rope_tables(length):
    pos = jnp.arange(length, dtype=F32)
    inv_freq = 1.0 / (ROPE_THETA ** (jnp.arange(0, HEAD_DIM, 2, dtype=F32) / HEAD_DIM))
    ang = pos[:, None] * inv_freq[None, :]
    ang = jnp.concatenate([ang, ang, ang, ang], axis=-1)
    sign = jnp.where(jnp.arange(LANES) < LANES // 2, -1.0, 1.0).astype(F32)
    cos, sin = jnp.cos(ang), jnp.sin(ang) * sign
    scale = HEAD_DIM ** -0.5 * math.log2(math.e)
    return cos * scale, sin * scale, cos, sin


def kernel(x, meta_tokens, ffn1_norm_g, ffn1_w_gate, ffn1_w_up, ffn1_w_down, mix_norm_g, w_in,
           lam_q1, lam_k1, lam_q2, lam_k2, subln_g, w_pool, pool_scale, w_out,
           ffn2_norm_g, ffn2_w_gate, ffn2_w_up, ffn2_w_down, final_norm_g):
    batch, seq, d = x.shape
    depth = w_in.shape[0]
    assert d == D_MODEL and meta_tokens.shape == (N_META, D_MODEL)

    tables = _rope_tables(N_META + seq)
    tab_meta = tuple(tb[:N_META] for tb in tables)
    tab_real = tuple(tb[N_META:] for tb in tables)
    final_g = final_norm_g.reshape(1, D_MODEL)

    h = x.reshape(batch * seq, D_MODEL)
    hm = meta_tokens.astype(x.dtype)

    for layer in range(depth):
        lam_init = 0.8 - 0.6 * math.exp(-0.3 * layer)
        last = layer == depth - 1
        wg1, wu1, wd1 = (w[layer].astype(BF16) for w in (ffn1_w_gate, ffn1_w_up, ffn1_w_down))
        wg2, wu2, wd2 = (w[layer].astype(BF16) for w in (ffn2_w_gate, ffn2_w_up, ffn2_w_down))
        wi = _to_head_layout(w_in[layer]).astype(BF16)
        wo = w_out[layer].astype(BF16)
        wp = w_pool[layer].astype(BF16)
        g1 = ffn1_norm_g[layer].reshape(1, D_MODEL)
        gm = mix_norm_g[layer].reshape(1, D_MODEL)
        g2 = ffn2_norm_g[layer].reshape(1, D_MODEL)
        ps = pool_scale[layer].reshape(1, POOL_WIDTH)
        sg = subln_g[layer].reshape(1, V_DIM)
        lam_params = tuple(p[layer].reshape(1, HEAD_DIM) for p in (lam_q1, lam_k1, lam_q2, lam_k2))

        hm = _ffn(hm, g1, wg1, wu1, wd1, final_g, tm=N_META, tf=512, apply_final=False)
        proj_m, u_m = _inproj(hm, gm, wi, tab_meta, None, batch=1, tl=N_META, meta=True)
        attn_m = _attn_meta(lam_params, sg, proj_m, lam_init=lam_init)
        if not last:
            hm = _outproj(hm, attn_m, proj_m, wp, ps, wo, tm=N_META)
            hm = _ffn(hm, g2, wg2, wu2, wd2, final_g, tm=N_META, tf=512, apply_final=False)

        h = _ffn(h, g1, wg1, wu1, wd1, final_g, tm=1024, tf=512, apply_final=False)
        proj = _inproj(h, gm, wi, tab_real, u_m, batch=batch, tl=512, meta=False)
        attn = _attn(lam_params, sg.reshape(V_DIM, 1), proj, proj_m, batch=batch,
                     lam_init=lam_init, tq=256, hp=8)
        h = _outproj(h, attn, proj, wp, ps, wo, tm=512)
        h = _ffn(h, g2, wg2, wu2, wd2, final_g, tm=1024, tf=512, apply_final=last)

    return h.reshape(batch, seq, D_MODEL)
```

```python
import functools
import math

import jax
import jax.numpy as jnp
from jax import lax
from jax.experimental import pallas as pl
from jax.experimental.pallas import tpu as pltpu

F32 = jnp.float32
BF16 = jnp.bfloat16

D_MODEL = 2048
N_META = 16
ATTN_WIDTH = 1024
POOL_WIDTH = 1024
HEAD_DIM = 64
V_DIM = 2 * HEAD_DIM
N_HEADS = ATTN_WIDTH // V_DIM
POOL_WINDOWS = (2, 4, 8, 16)
POOL_GROUP_WIDTH = POOL_WIDTH // len(POOL_WINDOWS)
IN_WIDTH = 3 * ATTN_WIDTH + POOL_WIDTH
D_FF = 5632
FFN_CHUNK = 512
ROPE_THETA = 10000.0
NORM_EPS = 1e-6
SUBLN_EPS = 1e-5

LANES = 128
V7X_VMEM_BYTES = 64 * 1024 * 1024


def _vmem_limit(estimate_bytes):
    return int(min(estimate_bytes * 5 // 4 + (4 << 20), V7X_VMEM_BYTES * 7 // 8))


def _rms_norm(x, g, eps):
    return x * lax.rsqrt(jnp.mean(x * x, axis=-1, keepdims=True) + eps) * g


def _ffn_kernel(x_ref, g_ref, wgu_ref, wd_ref, fg_ref, o_ref, xn_ref, *, apply_final):
    f = pl.program_id(1)
    tf = wd_ref.shape[0]

    @pl.when(f == 0)
    def _():
        x = x_ref[...]
        xn_ref[...] = _rms_norm(x, g_ref[...], NORM_EPS).astype(BF16)
        o_ref[...] = x

    gate_up = jnp.dot(xn_ref[...], wgu_ref[...], preferred_element_type=F32)
    gate, up = gate_up[:, :tf], gate_up[:, tf:]
    act = (gate * jax.nn.sigmoid(gate)) * (0.5 * up)
    o_ref[...] += jnp.dot(act.astype(BF16), wd_ref[...], preferred_element_type=F32)

    if apply_final:
        @pl.when(f == pl.num_programs(1) - 1)
        def _():
            o_ref[...] = _rms_norm(o_ref[...], fg_ref[...], NORM_EPS)


def _ffn_weights(w_gate, w_up, w_down, tf):
    nf = D_FF // tf
    gate_up = jnp.concatenate([w_gate.reshape(D_MODEL, nf, tf), w_up.reshape(D_MODEL, nf, tf)], axis=2)
    return (gate_up.transpose(1, 0, 2).astype(BF16), w_down.reshape(nf, tf, D_MODEL).astype(BF16))


def _ffn(x, g, wgu, wd, final_g, *, tm, apply_final):
    t = x.shape[0]
    nf, tf = wd.shape[0], wd.shape[1]
    assert t % tm == 0
    est = (2 * 2 * tm * D_MODEL * 4 + tm * D_MODEL * 2 + 2 * 3 * D_MODEL * tf * 2
           + 3 * tm * tf * 4)
    return pl.pallas_call(
        functools.partial(_ffn_kernel, apply_final=apply_final),
        out_shape=jax.ShapeDtypeStruct((t, D_MODEL), F32),
        grid=(t // tm, nf),
        in_specs=[
            pl.BlockSpec((tm, D_MODEL), lambda i, f: (i, 0)),
            pl.BlockSpec((1, D_MODEL), lambda i, f: (0, 0)),
            pl.BlockSpec((None, D_MODEL, 2 * tf), lambda i, f: (f, 0, 0)),
            pl.BlockSpec((None, tf, D_MODEL), lambda i, f: (f, 0, 0)),
            pl.BlockSpec((1, D_MODEL), lambda i, f: (0, 0)),
        ],
        out_specs=pl.BlockSpec((tm, D_MODEL), lambda i, f: (i, 0)),
        scratch_shapes=[pltpu.VMEM((tm, D_MODEL), BF16)],
        compiler_params=pltpu.CompilerParams(
            dimension_semantics=("parallel", "arbitrary"),
            vmem_limit_bytes=_vmem_limit(est)),
        name="ffn",
    )(x, g, wgu, wd, final_g)


def _rope(x, cos, sin_signed):
    outs = []
    for c in range(x.shape[1] // LANES):
        xc = x[:, c * LANES:(c + 1) * LANES]
        outs.append(xc * cos + pltpu.roll(xc, LANES // 2, axis=1) * sin_signed)
    return jnp.concatenate(outs, axis=1)


def _pool_diff(hist, cur, inv_count_fn):
    run = jnp.concatenate([hist, cur], axis=0)
    outs = []
    for g, w in enumerate(POOL_WINDOWS):
        run = run + pltpu.roll(run, w // 2, axis=0)
        cols = slice(g * POOL_GROUP_WIDTH, (g + 1) * POOL_GROUP_WIDTH)
        outs.append(run[N_META:, :POOL_GROUP_WIDTH] * inv_count_fn(w) - cur[:, cols])
        run = run[:, POOL_GROUP_WIDTH:]
    return jnp.concatenate(outs, axis=1)


def _inproj_kernel(*refs, meta):
    if meta:
        (x_ref, g_ref, w_ref, cq_ref, sq_ref, ck_ref, sk_ref,
         o_ref, u_ref, hist_ref) = refs
    else:
        (x_ref, g_ref, w_ref, cq_ref, sq_ref, ck_ref, sk_ref, um_ref,
         o_ref, hist_ref) = refs
    j = pl.program_id(1)
    rows = x_ref.shape[0]
    chunk = lambda n: slice(n * ATTN_WIDTH, (n + 1) * ATTN_WIDTH)

    if not meta:
        @pl.when(j == 0)
        def _():
            hist_ref[...] = um_ref[...]

    xn = _rms_norm(x_ref[...], g_ref[...], NORM_EPS).astype(BF16)

    def project(n):
        return jnp.dot(xn, w_ref[:, chunk(n)], preferred_element_type=F32)

    o_ref[:, chunk(0)] = _rope(project(0), cq_ref[...], sq_ref[...]).astype(BF16)
    o_ref[:, chunk(1)] = _rope(project(1), ck_ref[...], sk_ref[...]).astype(BF16)
    o_ref[:, chunk(2)] = project(2).astype(BF16)

    acc = project(3)
    if meta:
        u_ref[...] = acc
        hist = jnp.zeros((N_META, POOL_WIDTH), F32)
        pos = lax.broadcasted_iota(jnp.int32, (rows, 1), 0)

        def inv_count(w):
            return 1.0 / jnp.minimum(pos + 1, w).astype(F32)
    else:
        hist = hist_ref[...]

        def inv_count(w):
            return 1.0 / w
    o_ref[:, chunk(3)] = _pool_diff(hist, acc, inv_count).astype(BF16)
    hist_ref[...] = acc[rows - N_META:, :]


def _inproj(x, g, w_in, tables, u_meta, *, batch, tl, meta):
    t = x.shape[0]
    seq = t // batch
    nj = seq // tl
    assert seq % tl == 0
    cq, sq, ck, sk = tables
    tok = lambda b, j: (b * nj + j, 0)
    tab = lambda b, j: (j, 0)
    const = lambda b, j: (0, 0)
    in_specs = [
        pl.BlockSpec((tl, D_MODEL), tok),
        pl.BlockSpec((1, D_MODEL), const),
        pl.BlockSpec((D_MODEL, IN_WIDTH), const, pipeline_mode=pl.Buffered(1)),
        pl.BlockSpec((tl, LANES), tab),
        pl.BlockSpec((tl, LANES), tab),
        pl.BlockSpec((tl, LANES), tab),
        pl.BlockSpec((tl, LANES), tab),
    ]
    args = [x, g, w_in, cq, sq, ck, sk]
    proj_shape = jax.ShapeDtypeStruct((t, IN_WIDTH), BF16)
    proj_spec = pl.BlockSpec((tl, IN_WIDTH), tok)
    if meta:
        out_shape = (proj_shape, jax.ShapeDtypeStruct((t, POOL_WIDTH), F32))
        out_specs = (proj_spec, pl.BlockSpec((tl, POOL_WIDTH), const))
    else:
        in_specs.append(pl.BlockSpec((N_META, POOL_WIDTH), const))
        args.append(u_meta)
        out_shape = proj_shape
        out_specs = proj_spec
    est = (2 * tl * D_MODEL * 4 + D_MODEL * IN_WIDTH * 2 + 2 * tl * IN_WIDTH * 2
           + tl * D_MODEL * 2 + 4 * (tl + N_META) * POOL_WIDTH * 4 + 8 * tl * LANES * 4)
    return pl.pallas_call(
        functools.partial(_inproj_kernel, meta=meta),
        out_shape=out_shape,
        grid=(batch, nj),
        in_specs=in_specs,
        out_specs=out_specs,
        scratch_shapes=[pltpu.VMEM((N_META, POOL_WIDTH), F32)],
        compiler_params=pltpu.CompilerParams(
            dimension_semantics=("arbitrary", "arbitrary"),
            vmem_limit_bytes=_vmem_limit(est)),
        name="inproj_meta" if meta else "inproj",
    )(*args)


def _stack_queries(q):
    lane = lax.broadcasted_iota(jnp.int32, q.shape, 1)
    first_map = (lane % HEAD_DIM) < HEAD_DIM // 2
    zero = jnp.zeros_like(q)
    return jnp.concatenate([jnp.where(first_map, q, zero), jnp.where(first_map, zero, q)], axis=0)


def _scores(qq, k):
    return lax.dot_general(qq, k, (((1,), (1,)), ((), ())), preferred_element_type=F32)


def _lambda(lq1_ref, lk1_ref, lq2_ref, lk2_ref, lam_init):
    s1 = jnp.sum(lq1_ref[...] * lk1_ref[...], axis=-1, keepdims=True)
    s2 = jnp.sum(lq2_ref[...] * lk2_ref[...], axis=-1, keepdims=True)
    return jnp.exp(s1) - jnp.exp(s2) + lam_init


def _finish(acc, l, lam, g, lam_init, tq):
    o = acc[:tq] / l[:tq] - lam * (acc[tq:] / l[tq:])
    return _rms_norm(o, g, SUBLN_EPS) * (1.0 - lam_init)


def _attn_kernel(lq1_ref, lk1_ref, lq2_ref, lk2_ref, g_ref, q_ref, k_ref, v_ref, km_ref, vm_ref,
                 o_ref, qq_ref, m_ref, l_ref, acc_ref, s0_ref, s1_ref, *, lam_init, tq, hp):
    nq = q_ref.shape[0] // tq
    lam = _lambda(lq1_ref, lk1_ref, lq2_ref, lk2_ref, lam_init)
    contract_last = (((1,), (1,)), ((), ()))
    contract_first = (((0,), (0,)), ((), ()))
    heads = [slice(h * V_DIM, (h + 1) * V_DIM) for h in range(hp)]

    def block_rows(b):
        return pl.ds(pl.multiple_of(b * tq, tq), tq)

    def scores(k, h):
        return lax.dot_general(k, qq_ref[h], contract_last, preferred_element_type=F32)

    def consume(h, s, v, first):
        blk_max = jnp.max(s, axis=0, keepdims=True)
        m_new = blk_max if first else jnp.maximum(m_ref[h], blk_max)
        p = jnp.exp2(s - m_new)
        pv = lax.dot_general(v, p.astype(BF16), contract_first, preferred_element_type=F32)
        psum = jnp.sum(p, axis=0, keepdims=True)
        if first:
            l_ref[h] = psum
            acc_ref[h] = pv
        else:
            alpha = jnp.exp2(m_ref[h] - m_new)
            l_ref[h] = alpha * l_ref[h] + psum
            acc_ref[h] = alpha * acc_ref[h] + pv
        m_ref[h] = m_new

    def interleave(producers, consumers, lead, per_consumer):
        producers = list(producers)
        for task in producers[:lead]:
            task()
        rest = producers[lead:]
        for c, task in enumerate(consumers):
            task()
            for extra in rest[c * per_consumer:(c + 1) * per_consumer]:
                extra()
        for extra in rest[len(consumers) * per_consumer:]:
            extra()

    def step(b, cur_ref, nxt_ref):
        def prefetch(h, hs):
            def run():
                nxt_ref[h] = scores(k_ref[block_rows(b + 1), hs], h)
            return run

        def use(h, hs):
            return lambda: consume(h, cur_ref[h], v_ref[block_rows(b), hs], False)

        producers = [prefetch(h, hs) for h, hs in enumerate(heads)] if nxt_ref is not None else []
        interleave(producers, [use(h, hs) for h, hs in enumerate(heads)], 2, 1)

    def q_tile(i, _):
        rows = block_rows(i)
        for h, hs in enumerate(heads):
            qq_ref[h] = _stack_queries(q_ref[rows, hs])

        key = lax.broadcasted_iota(jnp.int32, (N_META + tq, 2 * tq), 0) - N_META
        qry = lax.broadcasted_iota(jnp.int32, (N_META + tq, 2 * tq), 1)
        visible = key <= jnp.where(qry >= tq, qry - tq, qry)
        diag = [None] * hp

        def diag_scores(h, hs):
            def run():
                diag[h] = scores(jnp.concatenate([km_ref[:, hs], k_ref[rows, hs]], axis=0), h)
            return run

        def first_scores(h, hs):
            def run():
                s0_ref[h] = scores(k_ref[block_rows(0), hs], h)
            return run

        def use_diag(h, hs):
            return lambda: consume(h, jnp.where(visible, diag[h], jnp.finfo(F32).min),
                                   jnp.concatenate([vm_ref[:, hs], v_ref[rows, hs]], axis=0), True)

        producers = [diag_scores(h, hs) for h, hs in enumerate(heads)]
        producers = producers[:2] + [t for h, hs in enumerate(heads)
                                     for t in (producers[h + 2:h + 3] + [first_scores(h, hs)])]
        interleave(producers, [use_diag(h, hs) for h, hs in enumerate(heads)], 2, 2)

        def body(b, _):
            @pl.when(b % 2 == 0)
            def _():
                step(b, s0_ref, s1_ref)

            @pl.when(b % 2 == 1)
            def _():
                step(b, s1_ref, s0_ref)
            return 0

        lax.fori_loop(0, i - 1, body, 0)

        @pl.when((i >= 1) & (i % 2 == 1))
        def _():
            step(i - 1, s0_ref, None)

        @pl.when((i >= 1) & (i % 2 == 0))
        def _():
            step(i - 1, s1_ref, None)

        for h, hs in enumerate(heads):
            acc = acc_ref[h]
            l = l_ref[h]
            o = acc[:, :tq] / l[:, :tq] - lam * (acc[:, tq:] / l[:, tq:])
            o = o * lax.rsqrt(jnp.mean(o * o, axis=0, keepdims=True) + SUBLN_EPS)
            o = o * g_ref[...] * (1.0 - lam_init)
            o_ref[rows, hs] = o.T.astype(BF16)
        return 0

    lax.fori_loop(0, nq, q_tile, 0)


def _attn(lam_params, subln_g_col, proj, proj_meta, *, batch, lam_init, tq, hp):
    t = proj.shape[0]
    seq = t // batch
    assert seq % tq == 0 and N_HEADS % hp == 0
    groups = N_HEADS // hp
    width = hp * V_DIM
    small = lambda b, g: (0, 0)
    lam_specs = [pl.BlockSpec((1, HEAD_DIM), small)] * 4
    score_buf = pltpu.VMEM((hp, tq, 2 * tq), F32)
    est = (2 * 4 * seq * width * 2 + hp * (2 * tq * V_DIM * 2 + 2 * tq * 2 * tq * 4
                                           + 8 * tq * 2 * tq * 4 // hp + 2 * V_DIM * 2 * tq * 4))
    return pl.pallas_call(
        functools.partial(_attn_kernel, lam_init=lam_init, tq=tq, hp=hp),
        out_shape=jax.ShapeDtypeStruct((t, ATTN_WIDTH), BF16),
        grid=(batch, groups),
        in_specs=lam_specs + [
            pl.BlockSpec((V_DIM, 1), small),
            pl.BlockSpec((seq, width), lambda b, g: (b, g)),
            pl.BlockSpec((seq, width), lambda b, g: (b, groups + g)),
            pl.BlockSpec((seq, width), lambda b, g: (b, 2 * groups + g)),
            pl.BlockSpec((N_META, width), lambda b, g: (0, groups + g)),
            pl.BlockSpec((N_META, width), lambda b, g: (0, 2 * groups + g)),
        ],
        out_specs=pl.BlockSpec((seq, width), lambda b, g: (b, g)),
        scratch_shapes=[pltpu.VMEM((hp, 2 * tq, V_DIM), BF16),
                        pltpu.VMEM((hp, 1, 2 * tq), F32), pltpu.VMEM((hp, 1, 2 * tq), F32),
                        pltpu.VMEM((hp, V_DIM, 2 * tq), F32), score_buf, score_buf],
        compiler_params=pltpu.CompilerParams(
            dimension_semantics=("parallel", "parallel"),
            vmem_limit_bytes=_vmem_limit(est)),
        name="attn",
    )(*lam_params, subln_g_col, proj, proj, proj, proj_meta, proj_meta)


def _attn_meta_kernel(lq1_ref, lk1_ref, lq2_ref, lk2_ref, g_ref, q_ref, k_ref, v_ref, o_ref,
                      *, lam_init):
    qq = _stack_queries(q_ref[...])
    s = _scores(qq, k_ref[...])
    row = lax.broadcasted_iota(jnp.int32, s.shape, 0) % N_META
    col = lax.broadcasted_iota(jnp.int32, s.shape, 1)
    s = jnp.where(col <= row, s, jnp.finfo(F32).min)
    m = jnp.max(s, axis=-1, keepdims=True)
    p = jnp.exp2(s - m)
    l = jnp.sum(p, axis=-1, keepdims=True)
    acc = jnp.dot(p.astype(BF16), v_ref[...], preferred_element_type=F32)
    lam = _lambda(lq1_ref, lk1_ref, lq2_ref, lk2_ref, lam_init)
    o_ref[...] = _finish(acc, l, lam, g_ref[...], lam_init, N_META).astype(BF16)


def _attn_meta(lam_params, subln_g, proj_meta, *, lam_init):
    small = lambda h: (0, 0)
    return pl.pallas_call(
        functools.partial(_attn_meta_kernel, lam_init=lam_init),
        out_shape=jax.ShapeDtypeStruct((N_META, ATTN_WIDTH), BF16),
        grid=(N_HEADS,),
        in_specs=[pl.BlockSpec((1, HEAD_DIM), small)] * 4 + [
            pl.BlockSpec((1, V_DIM), small),
            pl.BlockSpec((N_META, V_DIM), lambda h: (0, h)),
            pl.BlockSpec((N_META, V_DIM), lambda h: (0, N_HEADS + h)),
            pl.BlockSpec((N_META, V_DIM), lambda h: (0, 2 * N_HEADS + h)),
        ],
        out_specs=pl.BlockSpec((N_META, V_DIM), lambda h: (0, h)),
        compiler_params=pltpu.CompilerParams(dimension_semantics=("arbitrary",)),
        name="attn_meta",
    )(*lam_params, subln_g, proj_meta, proj_meta, proj_meta)


def _outproj_kernel(h_ref, a_ref, d_ref, wp_ref, ps_ref, wo_ref, o_ref):
    diff = d_ref[...]
    pools = []
    for g in range(len(POOL_WINDOWS)):
        cols = slice(g * POOL_GROUP_WIDTH, (g + 1) * POOL_GROUP_WIDTH)
        pools.append(jnp.dot(diff[:, cols], wp_ref[g], preferred_element_type=F32))
    pool = (jnp.concatenate(pools, axis=1) * ps_ref[...]).astype(BF16)
    mix = jnp.concatenate([a_ref[...], pool], axis=1)
    o_ref[...] = h_ref[...] + jnp.dot(mix, wo_ref[...], preferred_element_type=F32)


def _outproj(h, attn, proj, w_pool, pool_scale, w_out, *, tm):
    t = h.shape[0]
    assert t % tm == 0
    est = (2 * 2 * tm * D_MODEL * 4 + 2 * 2 * tm * ATTN_WIDTH * 2 + 2 * D_MODEL * D_MODEL * 2
           + 2 * POOL_WIDTH * POOL_GROUP_WIDTH * 2 + 3 * tm * D_MODEL * 4)
    return pl.pallas_call(
        _outproj_kernel,
        out_shape=jax.ShapeDtypeStruct((t, D_MODEL), F32),
        grid=(t // tm,),
        in_specs=[
            pl.BlockSpec((tm, D_MODEL), lambda i: (i, 0)),
            pl.BlockSpec((tm, ATTN_WIDTH), lambda i: (i, 0)),
            pl.BlockSpec((tm, POOL_WIDTH), lambda i: (i, 3)),
            pl.BlockSpec((len(POOL_WINDOWS), POOL_GROUP_WIDTH, POOL_GROUP_WIDTH),
                         lambda i: (0, 0, 0)),
            pl.BlockSpec((1, POOL_WIDTH), lambda i: (0, 0)),
            pl.BlockSpec((D_MODEL, D_MODEL), lambda i: (0, 0)),
        ],
        out_specs=pl.BlockSpec((tm, D_MODEL), lambda i: (i, 0)),
        compiler_params=pltpu.CompilerParams(
            dimension_semantics=("parallel",),
            vmem_limit_bytes=_vmem_limit(est)),
        name="outproj",
    )(h, attn, proj, w_pool, pool_scale, w_out)


def _to_head_layout(w_in):
    half = HEAD_DIM // 2
    qk = w_in[:, :2 * ATTN_WIDTH].reshape(D_MODEL, 2 * N_HEADS, 2, 2, half)
    qk = qk.transpose(0, 1, 3, 2, 4).reshape(D_MODEL, 2 * ATTN_WIDTH)
    return jnp.concatenate([qk, w_in[:, 2 * ATTN_WIDTH:]], axis=1)


def _rope_tables(length):
    pos = jnp.arange(length, dtype=F32)
    inv_freq = 1.0 / (ROPE_THETA ** (jnp.arange(0, HEAD_DIM, 2, dtype=F32) / HEAD_DIM))
    ang = pos[:, None] * inv_freq[None, :]
    ang = jnp.concatenate([ang, ang, ang, ang], axis=-1)
    sign = jnp.where(jnp.arange(LANES) < LANES // 2, -1.0, 1.0).astype(F32)
    cos, sin = jnp.cos(ang), jnp.sin(ang) * sign
    scale = HEAD_DIM ** -0.5 * math.log2(math.e)
    return cos * scale, sin * scale, cos, sin


def kernel(x, meta_tokens, ffn1_norm_g, ffn1_w_gate, ffn1_w_up, ffn1_w_down, mix_norm_g, w_in,
           lam_q1, lam_k1, lam_q2, lam_k2, subln_g, w_pool, pool_scale, w_out,
           ffn2_norm_g, ffn2_w_gate, ffn2_w_up, ffn2_w_down, final_norm_g):
    batch, seq, d = x.shape
    depth = w_in.shape[0]
    assert d == D_MODEL and meta_tokens.shape == (N_META, D_MODEL)

    tables = _rope_tables(N_META + seq)
    tab_meta = tuple(tb[:N_META] for tb in tables)
    tab_real = tuple(tb[N_META:] for tb in tables)
    final_g = final_norm_g.reshape(1, D_MODEL)

    h = x.reshape(batch * seq, D_MODEL)
    hm = meta_tokens.astype(x.dtype)

    for layer in range(depth):
        lam_init = 0.8 - 0.6 * math.exp(-0.3 * layer)
        last = layer == depth - 1
        wgu1, wd1 = _ffn_weights(ffn1_w_gate[layer], ffn1_w_up[layer], ffn1_w_down[layer], FFN_CHUNK)
        wgu2, wd2 = _ffn_weights(ffn2_w_gate[layer], ffn2_w_up[layer], ffn2_w_down[layer], FFN_CHUNK)
        wi = _to_head_layout(w_in[layer]).astype(BF16)
        wo = w_out[layer].astype(BF16)
        wp = w_pool[layer].astype(BF16)
        g1 = ffn1_norm_g[layer].reshape(1, D_MODEL)
        gm = mix_norm_g[layer].reshape(1, D_MODEL)
        g2 = ffn2_norm_g[layer].reshape(1, D_MODEL)
        ps = pool_scale[layer].reshape(1, POOL_WIDTH)
        sg = subln_g[layer].reshape(1, V_DIM)
        lam_params = tuple(p[layer].reshape(1, HEAD_DIM) for p in (lam_q1, lam_k1, lam_q2, lam_k2))

        hm = _ffn(hm, g1, wgu1, wd1, final_g, tm=N_META, apply_final=False)
        proj_m, u_m = _inproj(hm, gm, wi, tab_meta, None, batch=1, tl=N_META, meta=True)
        attn_m = _attn_meta(lam_params, sg, proj_m, lam_init=lam_init)
        if not last:
            hm = _outproj(hm, attn_m, proj_m, wp, ps, wo, tm=N_META)
            hm = _ffn(hm, g2, wgu2, wd2, final_g, tm=N_META, apply_final=False)

        h = _ffn(h, g1, wgu1, wd1, final_g, tm=1024, apply_final=False)
        proj = _inproj(h, gm, wi, tab_real, u_m, batch=batch, tl=512, meta=False)
        attn = _attn(lam_params, sg.reshape(V_DIM, 1), proj, proj_m, batch=batch,
                     lam_init=lam_init, tq=256, hp=8)
        h = _outproj(h, attn, proj, wp, ps, wo, tm=512)
        h = _ffn(h, g2, wgu2, wd2, final_g, tm=1024, apply_final=last)

    return h.reshape(batch, seq, D_MODEL)
```

```python
import functools
import math

import jax
import jax.numpy as jnp
from jax import lax
from jax.experimental import pallas as pl
from jax.experimental.pallas import tpu as pltpu

F32 = jnp.float32
BF16 = jnp.bfloat16

D_MODEL = 2048
N_META = 16
ATTN_WIDTH = 1024
POOL_WIDTH = 1024
HEAD_DIM = 64
V_DIM = 2 * HEAD_DIM
N_HEADS = ATTN_WIDTH // V_DIM
POOL_WINDOWS = (2, 4, 8, 16)
POOL_GROUP_WIDTH = POOL_WIDTH // len(POOL_WINDOWS)
IN_WIDTH = 3 * ATTN_WIDTH + POOL_WIDTH
D_FF = 5632
ROPE_THETA = 10000.0
NORM_EPS = 1e-6
SUBLN_EPS = 1e-5

LANES = 128
V7X_VMEM_BYTES = 64 * 1024 * 1024


def _vmem_limit(estimate_bytes):
    return int(min(estimate_bytes * 5 // 4 + (4 << 20), V7X_VMEM_BYTES * 7 // 8))


def _rms_norm(x, g, eps):
    return x * lax.rsqrt(jnp.mean(x * x, axis=-1, keepdims=True) + eps) * g


def _ffn_kernel(x_ref, g_ref, wg_ref, wu_ref, wd_ref, fg_ref, o_ref, xn_ref, *, apply_final):
    f = pl.program_id(1)

    @pl.when(f == 0)
    def _():
        x = x_ref[...]
        xn_ref[...] = _rms_norm(x, g_ref[...], NORM_EPS).astype(BF16)
        o_ref[...] = x

    xn = xn_ref[...]
    gate = jnp.dot(xn, wg_ref[...], preferred_element_type=F32)
    up = jnp.dot(xn, wu_ref[...], preferred_element_type=F32)
    act = (gate * jax.nn.sigmoid(gate)) * (0.5 * up)
    o_ref[...] += jnp.dot(act.astype(BF16), wd_ref[...], preferred_element_type=F32)

    if apply_final:
        @pl.when(f == pl.num_programs(1) - 1)
        def _():
            o_ref[...] = _rms_norm(o_ref[...], fg_ref[...], NORM_EPS)


def _ffn(x, g, wg, wu, wd, final_g, *, layer, tm, tf, apply_final):
    t = x.shape[0]
    assert t % tm == 0 and D_FF % tf == 0
    est = (2 * 2 * tm * D_MODEL * 4 + tm * D_MODEL * 2 + 2 * 3 * D_MODEL * tf * 2
           + 3 * tm * tf * 4)
    return pl.pallas_call(
        functools.partial(_ffn_kernel, apply_final=apply_final),
        out_shape=jax.ShapeDtypeStruct((t, D_MODEL), F32),
        grid=(t // tm, D_FF // tf),
        in_specs=[
            pl.BlockSpec((tm, D_MODEL), lambda i, f: (i, 0)),
            pl.BlockSpec((1, D_MODEL), lambda i, f: (0, 0)),
            pl.BlockSpec((None, D_MODEL, tf), lambda i, f: (layer, 0, f)),
            pl.BlockSpec((None, D_MODEL, tf), lambda i, f: (layer, 0, f)),
            pl.BlockSpec((None, tf, D_MODEL), lambda i, f: (layer, f, 0)),
            pl.BlockSpec((1, D_MODEL), lambda i, f: (0, 0)),
        ],
        out_specs=pl.BlockSpec((tm, D_MODEL), lambda i, f: (i, 0)),
        scratch_shapes=[pltpu.VMEM((tm, D_MODEL), BF16)],
        compiler_params=pltpu.CompilerParams(
            dimension_semantics=("parallel", "arbitrary"),
            vmem_limit_bytes=_vmem_limit(est)),
        name="ffn",
    )(x, g, wg, wu, wd, final_g)


def _rope(x, cos, sin_signed):
    outs = []
    for c in range(x.shape[1] // LANES):
        xc = x[:, c * LANES:(c + 1) * LANES]
        outs.append(xc * cos + pltpu.roll(xc, LANES // 2, axis=1) * sin_signed)
    return jnp.concatenate(outs, axis=1)


def _pool_diff(hist, cur, inv_count_fn):
    run = jnp.concatenate([hist, cur], axis=0)
    outs = []
    for g, w in enumerate(POOL_WINDOWS):
        run = run + pltpu.roll(run, w // 2, axis=0)
        cols = slice(g * POOL_GROUP_WIDTH, (g + 1) * POOL_GROUP_WIDTH)
        outs.append(run[N_META:, :POOL_GROUP_WIDTH] * inv_count_fn(w) - cur[:, cols])
        run = run[:, POOL_GROUP_WIDTH:]
    return jnp.concatenate(outs, axis=1)


def _inproj_kernel(*refs, meta):
    if meta:
        (x_ref, g_ref, w_ref, cq_ref, sq_ref, ck_ref, sk_ref,
         o_ref, u_ref, hist_ref) = refs
    else:
        (x_ref, g_ref, w_ref, cq_ref, sq_ref, ck_ref, sk_ref, um_ref,
         o_ref, hist_ref) = refs
    j = pl.program_id(1)
    rows = x_ref.shape[0]
    chunk = lambda n: slice(n * ATTN_WIDTH, (n + 1) * ATTN_WIDTH)

    if not meta:
        @pl.when(j == 0)
        def _():
            hist_ref[...] = um_ref[...]

    xn = _rms_norm(x_ref[...], g_ref[...], NORM_EPS).astype(BF16)

    def project(n):
        return jnp.dot(xn, w_ref[:, chunk(n)], preferred_element_type=F32)

    o_ref[:, chunk(0)] = _rope(project(0), cq_ref[...], sq_ref[...]).astype(BF16)
    o_ref[:, chunk(1)] = _rope(project(1), ck_ref[...], sk_ref[...]).astype(BF16)
    o_ref[:, chunk(2)] = project(2).astype(BF16)

    acc = project(3)
    if meta:
        u_ref[...] = acc
        hist = jnp.zeros((N_META, POOL_WIDTH), F32)
        pos = lax.broadcasted_iota(jnp.int32, (rows, 1), 0)

        def inv_count(w):
            return 1.0 / jnp.minimum(pos + 1, w).astype(F32)
    else:
        hist = hist_ref[...]

        def inv_count(w):
            return 1.0 / w
    o_ref[:, chunk(3)] = _pool_diff(hist, acc, inv_count).astype(BF16)
    hist_ref[...] = acc[rows - N_META:, :]


def _inproj(x, g, w_in, tables, u_meta, *, layer, batch, tl, meta):
    t = x.shape[0]
    seq = t // batch
    nj = seq // tl
    assert seq % tl == 0
    cq, sq, ck, sk = tables
    tok = lambda b, j: (b * nj + j, 0)
    tab = lambda b, j: (j, 0)
    const = lambda b, j: (0, 0)
    in_specs = [
        pl.BlockSpec((tl, D_MODEL), tok),
        pl.BlockSpec((1, D_MODEL), const),
        pl.BlockSpec((None, D_MODEL, IN_WIDTH), lambda b, j: (layer, 0, 0),
                     pipeline_mode=pl.Buffered(1)),
        pl.BlockSpec((tl, LANES), tab),
        pl.BlockSpec((tl, LANES), tab),
        pl.BlockSpec((tl, LANES), tab),
        pl.BlockSpec((tl, LANES), tab),
    ]
    args = [x, g, w_in, cq, sq, ck, sk]
    proj_shape = jax.ShapeDtypeStruct((t, IN_WIDTH), BF16)
    proj_spec = pl.BlockSpec((tl, IN_WIDTH), tok)
    if meta:
        out_shape = (proj_shape, jax.ShapeDtypeStruct((t, POOL_WIDTH), F32))
        out_specs = (proj_spec, pl.BlockSpec((tl, POOL_WIDTH), const))
    else:
        in_specs.append(pl.BlockSpec((N_META, POOL_WIDTH), const))
        args.append(u_meta)
        out_shape = proj_shape
        out_specs = proj_spec
    est = (2 * tl * D_MODEL * 4 + D_MODEL * IN_WIDTH * 2 + 2 * tl * IN_WIDTH * 2
           + tl * D_MODEL * 2 + 4 * (tl + N_META) * POOL_WIDTH * 4 + 8 * tl * LANES * 4)
    return pl.pallas_call(
        functools.partial(_inproj_kernel, meta=meta),
        out_shape=out_shape,
        grid=(batch, nj),
        in_specs=in_specs,
        out_specs=out_specs,
        scratch_shapes=[pltpu.VMEM((N_META, POOL_WIDTH), F32)],
        compiler_params=pltpu.CompilerParams(
            dimension_semantics=("arbitrary", "arbitrary"),
            vmem_limit_bytes=_vmem_limit(est)),
        name="inproj_meta" if meta else "inproj",
    )(*args)


def _stack_queries(q):
    lane = lax.broadcasted_iota(jnp.int32, q.shape, 1)
    first_map = (lane % HEAD_DIM) < HEAD_DIM // 2
    zero = jnp.zeros_like(q)
    return jnp.concatenate([jnp.where(first_map, q, zero), jnp.where(first_map, zero, q)], axis=0)


def _scores(qq, k):
    return lax.dot_general(qq, k, (((1,), (1,)), ((), ())), preferred_element_type=F32)


def _lambda(lq1_ref, lk1_ref, lq2_ref, lk2_ref, lam_init):
    s1 = jnp.sum(lq1_ref[...] * lk1_ref[...], axis=-1, keepdims=True)
    s2 = jnp.sum(lq2_ref[...] * lk2_ref[...], axis=-1, keepdims=True)
    return jnp.exp(s1) - jnp.exp(s2) + lam_init


def _finish(acc, l, lam, g, lam_init, tq):
    o = acc[:tq] / l[:tq] - lam * (acc[tq:] / l[tq:])
    return _rms_norm(o, g, SUBLN_EPS) * (1.0 - lam_init)


def _attn_kernel(lq1_ref, lk1_ref, lq2_ref, lk2_ref, g_ref, q_ref, k_ref, v_ref, km_ref, vm_ref,
                 o_ref, qq_ref, m_ref, l_ref, acc_ref, s0_ref, s1_ref, *, lam_init, tq, hp):
    nq = q_ref.shape[0] // tq
    lam = _lambda(lq1_ref, lk1_ref, lq2_ref, lk2_ref, lam_init)
    contract_last = (((1,), (1,)), ((), ()))
    contract_first = (((0,), (0,)), ((), ()))
    heads = [slice(h * V_DIM, (h + 1) * V_DIM) for h in range(hp)]

    def block_rows(b):
        return pl.ds(pl.multiple_of(b * tq, tq), tq)

    def scores(k, h):
        return lax.dot_general(k, qq_ref[h], contract_last, preferred_element_type=F32)

    def consume(h, s, v, first):
        blk_max = jnp.max(s, axis=0, keepdims=True)
        m_new = blk_max if first else jnp.maximum(m_ref[h], blk_max)
        p = jnp.exp2(s - m_new)
        pv = lax.dot_general(v, p.astype(BF16), contract_first, preferred_element_type=F32)
        psum = jnp.sum(p, axis=0, keepdims=True)
        if first:
            l_ref[h] = psum
            acc_ref[h] = pv
        else:
            alpha = jnp.exp2(m_ref[h] - m_new)
            l_ref[h] = alpha * l_ref[h] + psum
            acc_ref[h] = alpha * acc_ref[h] + pv
        m_ref[h] = m_new

    def interleave(producers, consumers, lead, per_consumer):
        producers = list(producers)
        for task in producers[:lead]:
            task()
        rest = producers[lead:]
        for c, task in enumerate(consumers):
            task()
            for extra in rest[c * per_consumer:(c + 1) * per_consumer]:
                extra()
        for extra in rest[len(consumers) * per_consumer:]:
            extra()

    def step(b, cur_ref, nxt_ref):
        def prefetch(h, hs):
            def run():
                nxt_ref[h] = scores(k_ref[block_rows(b + 1), hs], h)
            return run

        def use(h, hs):
            return lambda: consume(h, cur_ref[h], v_ref[block_rows(b), hs], False)

        producers = [prefetch(h, hs) for h, hs in enumerate(heads)] if nxt_ref is not None else []
        interleave(producers, [use(h, hs) for h, hs in enumerate(heads)], 2, 1)

    def q_tile(i, _):
        rows = block_rows(i)
        for h, hs in enumerate(heads):
            qq_ref[h] = _stack_queries(q_ref[rows, hs])

        key = lax.broadcasted_iota(jnp.int32, (N_META + tq, 2 * tq), 0) - N_META
        qry = lax.broadcasted_iota(jnp.int32, (N_META + tq, 2 * tq), 1)
        visible = key <= jnp.where(qry >= tq, qry - tq, qry)
        diag = [None] * hp

        def diag_scores(h, hs):
            def run():
                diag[h] = scores(jnp.concatenate([km_ref[:, hs], k_ref[rows, hs]], axis=0), h)
            return run

        def first_scores(h, hs):
            def run():
                s0_ref[h] = scores(k_ref[block_rows(0), hs], h)
            return run

        def use_diag(h, hs):
            return lambda: consume(h, jnp.where(visible, diag[h], jnp.finfo(F32).min),
                                   jnp.concatenate([vm_ref[:, hs], v_ref[rows, hs]], axis=0), True)

        producers = [diag_scores(h, hs) for h, hs in enumerate(heads)]
        producers = producers[:2] + [t for h, hs in enumerate(heads)
                                     for t in (producers[h + 2:h + 3] + [first_scores(h, hs)])]
        interleave(producers, [use_diag(h, hs) for h, hs in enumerate(heads)], 2, 2)

        def body(b, _):
            @pl.when(b % 2 == 0)
            def _():
                step(b, s0_ref, s1_ref)

            @pl.when(b % 2 == 1)
            def _():
                step(b, s1_ref, s0_ref)
            return 0

        lax.fori_loop(0, i - 1, body, 0)

        @pl.when((i >= 1) & (i % 2 == 1))
        def _():
            step(i - 1, s0_ref, None)

        @pl.when((i >= 1) & (i % 2 == 0))
        def _():
            step(i - 1, s1_ref, None)

        for h, hs in enumerate(heads):
            acc = acc_ref[h]
            l = l_ref[h]
            o = acc[:, :tq] / l[:, :tq] - lam * (acc[:, tq:] / l[:, tq:])
            o = o * lax.rsqrt(jnp.mean(o * o, axis=0, keepdims=True) + SUBLN_EPS)
            o = o * g_ref[...] * (1.0 - lam_init)
            o_ref[rows, hs] = o.T.astype(BF16)
        return 0

    lax.fori_loop(0, nq, q_tile, 0)


def _attn(lam_params, subln_g_col, proj, proj_meta, *, batch, lam_init, tq, hp):
    t = proj.shape[0]
    seq = t // batch
    assert seq % tq == 0 and N_HEADS % hp == 0
    groups = N_HEADS // hp
    width = hp * V_DIM
    small = lambda b, g: (0, 0)
    lam_specs = [pl.BlockSpec((1, HEAD_DIM), small)] * 4
    score_buf = pltpu.VMEM((hp, tq, 2 * tq), F32)
    est = (2 * 4 * seq * width * 2 + hp * (2 * tq * V_DIM * 2 + 2 * tq * 2 * tq * 4
                                           + 8 * tq * 2 * tq * 4 // hp + 2 * V_DIM * 2 * tq * 4))
    return pl.pallas_call(
        functools.partial(_attn_kernel, lam_init=lam_init, tq=tq, hp=hp),
        out_shape=jax.ShapeDtypeStruct((t, ATTN_WIDTH), BF16),
        grid=(batch, groups),
        in_specs=lam_specs + [
            pl.BlockSpec((V_DIM, 1), small),
            pl.BlockSpec((seq, width), lambda b, g: (b, g)),
            pl.BlockSpec((seq, width), lambda b, g: (b, groups + g)),
            pl.BlockSpec((seq, width), lambda b, g: (b, 2 * groups + g)),
            pl.BlockSpec((N_META, width), lambda b, g: (0, groups + g)),
            pl.BlockSpec((N_META, width), lambda b, g: (0, 2 * groups + g)),
        ],
        out_specs=pl.BlockSpec((seq, width), lambda b, g: (b, g)),
        scratch_shapes=[pltpu.VMEM((hp, 2 * tq, V_DIM), BF16),
                        pltpu.VMEM((hp, 1, 2 * tq), F32), pltpu.VMEM((hp, 1, 2 * tq), F32),
                        pltpu.VMEM((hp, V_DIM, 2 * tq), F32), score_buf, score_buf],
        compiler_params=pltpu.CompilerParams(
            dimension_semantics=("parallel", "parallel"),
            vmem_limit_bytes=_vmem_limit(est)),
        name="attn",
    )(*lam_params, subln_g_col, proj, proj, proj, proj_meta, proj_meta)


def _attn_meta_kernel(lq1_ref, lk1_ref, lq2_ref, lk2_ref, g_ref, q_ref, k_ref, v_ref, o_ref,
                      *, lam_init):
    qq = _stack_queries(q_ref[...])
    s = _scores(qq, k_ref[...])
    row = lax.broadcasted_iota(jnp.int32, s.shape, 0) % N_META
    col = lax.broadcasted_iota(jnp.int32, s.shape, 1)
    s = jnp.where(col <= row, s, jnp.finfo(F32).min)
    m = jnp.max(s, axis=-1, keepdims=True)
    p = jnp.exp2(s - m)
    l = jnp.sum(p, axis=-1, keepdims=True)
    acc = jnp.dot(p.astype(BF16), v_ref[...], preferred_element_type=F32)
    lam = _lambda(lq1_ref, lk1_ref, lq2_ref, lk2_ref, lam_init)
    o_ref[...] = _finish(acc, l, lam, g_ref[...], lam_init, N_META).astype(BF16)


def _attn_meta(lam_params, subln_g, proj_meta, *, lam_init):
    small = lambda h: (0, 0)
    return pl.pallas_call(
        functools.partial(_attn_meta_kernel, lam_init=lam_init),
        out_shape=jax.ShapeDtypeStruct((N_META, ATTN_WIDTH), BF16),
        grid=(N_HEADS,),
        in_specs=[pl.BlockSpec((1, HEAD_DIM), small)] * 4 + [
            pl.BlockSpec((1, V_DIM), small),
            pl.BlockSpec((N_META, V_DIM), lambda h: (0, h)),
            pl.BlockSpec((N_META, V_DIM), lambda h: (0, N_HEADS + h)),
            pl.BlockSpec((N_META, V_DIM), lambda h: (0, 2 * N_HEADS + h)),
        ],
        out_specs=pl.BlockSpec((N_META, V_DIM), lambda h: (0, h)),
        compiler_params=pltpu.CompilerParams(dimension_semantics=("arbitrary",)),
        name="attn_meta",
    )(*lam_params, subln_g, proj_meta, proj_meta, proj_meta)


def _outproj_kernel(h_ref, a_ref, d_ref, wp_ref, ps_ref, wo_ref, o_ref):
    diff = d_ref[...]
    pools = []
    for g in range(len(POOL_WINDOWS)):
        cols = slice(g * POOL_GROUP_WIDTH, (g + 1) * POOL_GROUP_WIDTH)
        pools.append(jnp.dot(diff[:, cols], wp_ref[g], preferred_element_type=F32))
    pool = (jnp.concatenate(pools, axis=1) * ps_ref[...]).astype(BF16)
    mix = jnp.concatenate([a_ref[...], pool], axis=1)
    o_ref[...] = h_ref[...] + jnp.dot(mix, wo_ref[...], preferred_element_type=F32)


def _outproj(h, attn, proj, w_pool, pool_scale, w_out, *, layer, tm):
    t = h.shape[0]
    assert t % tm == 0
    est = (2 * 2 * tm * D_MODEL * 4 + 2 * 2 * tm * ATTN_WIDTH * 2 + 2 * D_MODEL * D_MODEL * 2
           + 2 * POOL_WIDTH * POOL_GROUP_WIDTH * 2 + 3 * tm * D_MODEL * 4)
    return pl.pallas_call(
        _outproj_kernel,
        out_shape=jax.ShapeDtypeStruct((t, D_MODEL), F32),
        grid=(t // tm,),
        in_specs=[
            pl.BlockSpec((tm, D_MODEL), lambda i: (i, 0)),
            pl.BlockSpec((tm, ATTN_WIDTH), lambda i: (i, 0)),
            pl.BlockSpec((tm, POOL_WIDTH), lambda i: (i, 3)),
            pl.BlockSpec((None, len(POOL_WINDOWS), POOL_GROUP_WIDTH, POOL_GROUP_WIDTH),
                         lambda i: (layer, 0, 0, 0)),
            pl.BlockSpec((1, POOL_WIDTH), lambda i: (0, 0)),
            pl.BlockSpec((None, D_MODEL, D_MODEL), lambda i: (layer, 0, 0)),
        ],
        out_specs=pl.BlockSpec((tm, D_MODEL), lambda i: (i, 0)),
        compiler_params=pltpu.CompilerParams(
            dimension_semantics=("parallel",),
            vmem_limit_bytes=_vmem_limit(est)),
        name="outproj",
    )(h, attn, proj, w_pool, pool_scale, w_out)


def _to_head_layout(w_in):
    depth, half = w_in.shape[0], HEAD_DIM // 2
    qk = w_in[..., :2 * ATTN_WIDTH].reshape(depth, D_MODEL, 2 * N_HEADS, 2, 2, half)
    qk = qk.transpose(0, 1, 2, 4, 3, 5).reshape(depth, D_MODEL, 2 * ATTN_WIDTH)
    return jnp.concatenate([qk, w_in[..., 2 * ATTN_WIDTH:]], axis=-1)


def _rope_tables(length):
    pos = jnp.arange(length, dtype=F32)
    inv_freq = 1.0 / (ROPE_THETA ** (jnp.arange(0, HEAD_DIM, 2, dtype=F32) / HEAD_DIM))
    ang = pos[:, None] * inv_freq[None, :]
    ang = jnp.concatenate([ang, ang, ang, ang], axis=-1)
    sign = jnp.where(jnp.arange(LANES) < LANES // 2, -1.0, 1.0).astype(F32)
    cos, sin = jnp.cos(ang), jnp.sin(ang) * sign
    scale = HEAD_DIM ** -0.5 * math.log2(math.e)
    return cos * scale, sin * scale, cos, sin


def kernel(x, meta_tokens, ffn1_norm_g, ffn1_w_gate, ffn1_w_up, ffn1_w_down, mix_norm_g, w_in,
           lam_q1, lam_k1, lam_q2, lam_k2, subln_g, w_pool, pool_scale, w_out,
           ffn2_norm_g, ffn2_w_gate, ffn2_w_up, ffn2_w_down, final_norm_g):
    batch, seq, d = x.shape
    depth = w_in.shape[0]
    assert d == D_MODEL and meta_tokens.shape == (N_META, D_MODEL)

    tables = _rope_tables(N_META + seq)
    tab_meta = tuple(tb[:N_META] for tb in tables)
    tab_real = tuple(tb[N_META:] for tb in tables)
    final_g = final_norm_g.reshape(1, D_MODEL)

    h = x.reshape(batch * seq, D_MODEL)
    hm = meta_tokens.astype(x.dtype)

    wg1, wu1, wd1 = (w.astype(BF16) for w in (ffn1_w_gate, ffn1_w_up, ffn1_w_down))
    wg2, wu2, wd2 = (w.astype(BF16) for w in (ffn2_w_gate, ffn2_w_up, ffn2_w_down))
    wi = _to_head_layout(w_in).astype(BF16)
    wo = w_out.astype(BF16)
    wp = w_pool.astype(BF16)

    for layer in range(depth):
        lam_init = 0.8 - 0.6 * math.exp(-0.3 * layer)
        last = layer == depth - 1
        g1 = ffn1_norm_g[layer].reshape(1, D_MODEL)
        gm = mix_norm_g[layer].reshape(1, D_MODEL)
        g2 = ffn2_norm_g[layer].reshape(1, D_MODEL)
        ps = pool_scale[layer].reshape(1, POOL_WIDTH)
        sg = subln_g[layer].reshape(1, V_DIM)
        lam_params = tuple(p[layer].reshape(1, HEAD_DIM) for p in (lam_q1, lam_k1, lam_q2, lam_k2))

        hm = _ffn(hm, g1, wg1, wu1, wd1, final_g, layer=layer, tm=N_META, tf=512, apply_final=False)
        proj_m, u_m = _inproj(hm, gm, wi, tab_meta, None, layer=layer, batch=1, tl=N_META, meta=True)
        attn_m = _attn_meta(lam_params, sg, proj_m, lam_init=lam_init)
        if not last:
            hm = _outproj(hm, attn_m, proj_m, wp, ps, wo, layer=layer, tm=N_META)
            hm = _ffn(hm, g2, wg2, wu2, wd2, final_g, layer=layer, tm=N_META, tf=512,
                      apply_final=False)

        h = _ffn(h, g1, wg1, wu1, wd1, final_g, layer=layer, tm=1024, tf=512, apply_final=False)
        proj = _inproj(h, gm, wi, tab_real, u_m, layer=layer, batch=batch, tl=512, meta=False)
        attn = _attn(lam_params, sg.reshape(V_DIM, 1), proj, proj_m, batch=batch,
                     lam_init=lam_init, tq=256, hp=8)
        h = _outproj(h, attn, proj, wp, ps, wo, layer=layer, tm=512)
        h = _ffn(h, g2, wg2, wu2, wd2, final_g, layer=layer, tm=1024, tf=512, apply_final=last)

    return h.reshape(batch, seq, D_MODEL)
```

```python
import functools
import math

import jax
import jax.numpy as jnp
from jax import lax
from jax.experimental import pallas as pl
from jax.experimental.pallas import tpu as pltpu

F32 = jnp.float32
BF16 = jnp.bfloat16

D_MODEL = 2048
N_META = 16
ATTN_WIDTH = 1024
POOL_WIDTH = 1024
HEAD_DIM = 64
V_DIM = 2 * HEAD_DIM
N_HEADS = ATTN_WIDTH // V_DIM
POOL_WINDOWS = (2, 4, 8, 16)
POOL_GROUP_WIDTH = POOL_WIDTH // len(POOL_WINDOWS)
IN_WIDTH = 3 * ATTN_WIDTH + POOL_WIDTH
D_FF = 5632
ROPE_THETA = 10000.0
NORM_EPS = 1e-6
SUBLN_EPS = 1e-5

LANES = 128
V7X_VMEM_BYTES = 64 * 1024 * 1024


def _vmem_limit(estimate_bytes):
    return int(min(estimate_bytes * 5 // 4 + (4 << 20), V7X_VMEM_BYTES * 7 // 8))


def _rms_norm(x, g, eps):
    return x * lax.rsqrt(jnp.mean(x * x, axis=-1, keepdims=True) + eps) * g


def _ffn_kernel(x_ref, ga_ref, gb_ref, wg_ref, wu_ref, wd_ref, fg_ref, o_ref, xn_ref,
                *, chunks, n_ffn, apply_final):
    f = pl.program_id(1)

    @pl.when(f == 0)
    def _():
        x = x_ref[...]
        xn_ref[...] = _rms_norm(x, ga_ref[...], NORM_EPS).astype(BF16)
        o_ref[...] = x

    if n_ffn == 2:
        @pl.when(f == chunks)
        def _():
            xn_ref[...] = _rms_norm(o_ref[...], gb_ref[...], NORM_EPS).astype(BF16)

    xn = xn_ref[...]
    gate = jnp.dot(xn, wg_ref[...], preferred_element_type=F32)
    up = jnp.dot(xn, wu_ref[...], preferred_element_type=F32)
    act = (gate * jax.nn.sigmoid(gate)) * (0.5 * up)
    o_ref[...] += jnp.dot(act.astype(BF16), wd_ref[...], preferred_element_type=F32)

    if apply_final:
        @pl.when(f == pl.num_programs(1) - 1)
        def _():
            o_ref[...] = _rms_norm(o_ref[...], fg_ref[...], NORM_EPS)


def _ffn(x, gains, wg, wu, wd, final_g, *, first, tm, tf, apply_final):
    t = x.shape[0]
    n_ffn = len(gains)
    chunks = D_FF // tf
    assert t % tm == 0 and D_FF % tf == 0 and n_ffn in (1, 2)
    est = (2 * 2 * tm * D_MODEL * 4 + tm * D_MODEL * 2 + 2 * 3 * D_MODEL * tf * 2
           + 3 * tm * tf * 4)
    small = pl.BlockSpec((1, D_MODEL), lambda i, f: (0, 0))
    return pl.pallas_call(
        functools.partial(_ffn_kernel, chunks=chunks, n_ffn=n_ffn, apply_final=apply_final),
        out_shape=jax.ShapeDtypeStruct((t, D_MODEL), F32),
        grid=(t // tm, n_ffn * chunks),
        in_specs=[
            pl.BlockSpec((tm, D_MODEL), lambda i, f: (i, 0)),
            small,
            small,
            pl.BlockSpec((None, D_MODEL, tf), lambda i, f: (first + f // chunks, 0, f % chunks)),
            pl.BlockSpec((None, D_MODEL, tf), lambda i, f: (first + f // chunks, 0, f % chunks)),
            pl.BlockSpec((None, tf, D_MODEL), lambda i, f: (first + f // chunks, f % chunks, 0)),
            small,
        ],
        out_specs=pl.BlockSpec((tm, D_MODEL), lambda i, f: (i, 0)),
        scratch_shapes=[pltpu.VMEM((tm, D_MODEL), BF16)],
        compiler_params=pltpu.CompilerParams(
            dimension_semantics=("parallel", "arbitrary"),
            vmem_limit_bytes=_vmem_limit(est)),
        name="ffn",
    )(x, gains[0], gains[-1], wg, wu, wd, final_g)


def _rope(x, cos, sin_signed):
    outs = []
    for c in range(x.shape[1] // LANES):
        xc = x[:, c * LANES:(c + 1) * LANES]
        outs.append(xc * cos + pltpu.roll(xc, LANES // 2, axis=1) * sin_signed)
    return jnp.concatenate(outs, axis=1)


def _pool_diff(hist, cur, inv_count_fn):
    run = jnp.concatenate([hist, cur], axis=0)
    outs = []
    for g, w in enumerate(POOL_WINDOWS):
        run = run + pltpu.roll(run, w // 2, axis=0)
        cols = slice(g * POOL_GROUP_WIDTH, (g + 1) * POOL_GROUP_WIDTH)
        outs.append(run[N_META:, :POOL_GROUP_WIDTH] * inv_count_fn(w) - cur[:, cols])
        run = run[:, POOL_GROUP_WIDTH:]
    return jnp.concatenate(outs, axis=1)


def _inproj_kernel(*refs, meta):
    if meta:
        (x_ref, g_ref, w_ref, cq_ref, sq_ref, ck_ref, sk_ref,
         o_ref, u_ref, hist_ref) = refs
    else:
        (x_ref, g_ref, w_ref, cq_ref, sq_ref, ck_ref, sk_ref, um_ref,
         o_ref, hist_ref) = refs
    j = pl.program_id(1)
    rows = x_ref.shape[0]
    chunk = lambda n: slice(n * ATTN_WIDTH, (n + 1) * ATTN_WIDTH)

    if not meta:
        @pl.when(j == 0)
        def _():
            hist_ref[...] = um_ref[...]

    xn = _rms_norm(x_ref[...], g_ref[...], NORM_EPS).astype(BF16)

    def project(n):
        return jnp.dot(xn, w_ref[:, chunk(n)], preferred_element_type=F32)

    o_ref[:, chunk(0)] = _rope(project(0), cq_ref[...], sq_ref[...]).astype(BF16)
    o_ref[:, chunk(1)] = _rope(project(1), ck_ref[...], sk_ref[...]).astype(BF16)
    o_ref[:, chunk(2)] = project(2).astype(BF16)

    acc = project(3)
    if meta:
        u_ref[...] = acc
        hist = jnp.zeros((N_META, POOL_WIDTH), F32)
        pos = lax.broadcasted_iota(jnp.int32, (rows, 1), 0)

        def inv_count(w):
            return 1.0 / jnp.minimum(pos + 1, w).astype(F32)
    else:
        hist = hist_ref[...]

        def inv_count(w):
            return 1.0 / w
    o_ref[:, chunk(3)] = _pool_diff(hist, acc, inv_count).astype(BF16)
    hist_ref[...] = acc[rows - N_META:, :]


def _inproj(x, g, w_in, tables, u_meta, *, layer, batch, tl, meta):
    t = x.shape[0]
    seq = t // batch
    nj = seq // tl
    assert seq % tl == 0
    cq, sq, ck, sk = tables
    tok = lambda b, j: (b * nj + j, 0)
    tab = lambda b, j: (j, 0)
    const = lambda b, j: (0, 0)
    in_specs = [
        pl.BlockSpec((tl, D_MODEL), tok),
        pl.BlockSpec((1, D_MODEL), const),
        pl.BlockSpec((None, D_MODEL, IN_WIDTH), lambda b, j: (layer, 0, 0),
                     pipeline_mode=pl.Buffered(1)),
        pl.BlockSpec((tl, LANES), tab),
        pl.BlockSpec((tl, LANES), tab),
        pl.BlockSpec((tl, LANES), tab),
        pl.BlockSpec((tl, LANES), tab),
    ]
    args = [x, g, w_in, cq, sq, ck, sk]
    proj_shape = jax.ShapeDtypeStruct((t, IN_WIDTH), BF16)
    proj_spec = pl.BlockSpec((tl, IN_WIDTH), tok)
    if meta:
        out_shape = (proj_shape, jax.ShapeDtypeStruct((t, POOL_WIDTH), F32))
        out_specs = (proj_spec, pl.BlockSpec((tl, POOL_WIDTH), const))
    else:
        in_specs.append(pl.BlockSpec((N_META, POOL_WIDTH), const))
        args.append(u_meta)
        out_shape = proj_shape
        out_specs = proj_spec
    est = (2 * tl * D_MODEL * 4 + D_MODEL * IN_WIDTH * 2 + 2 * tl * IN_WIDTH * 2
           + tl * D_MODEL * 2 + 4 * (tl + N_META) * POOL_WIDTH * 4 + 8 * tl * LANES * 4)
    return pl.pallas_call(
        functools.partial(_inproj_kernel, meta=meta),
        out_shape=out_shape,
        grid=(batch, nj),
        in_specs=in_specs,
        out_specs=out_specs,
        scratch_shapes=[pltpu.VMEM((N_META, POOL_WIDTH), F32)],
        compiler_params=pltpu.CompilerParams(
            dimension_semantics=("arbitrary", "arbitrary"),
            vmem_limit_bytes=_vmem_limit(est)),
        name="inproj_meta" if meta else "inproj",
    )(*args)


def _stack_queries(q):
    lane = lax.broadcasted_iota(jnp.int32, q.shape, 1)
    first_map = (lane % HEAD_DIM) < HEAD_DIM // 2
    zero = jnp.zeros_like(q)
    return jnp.concatenate([jnp.where(first_map, q, zero), jnp.where(first_map, zero, q)], axis=0)


def _scores(qq, k):
    return lax.dot_general(qq, k, (((1,), (1,)), ((), ())), preferred_element_type=F32)


def _lambda(lq1_ref, lk1_ref, lq2_ref, lk2_ref, lam_init):
    s1 = jnp.sum(lq1_ref[...] * lk1_ref[...], axis=-1, keepdims=True)
    s2 = jnp.sum(lq2_ref[...] * lk2_ref[...], axis=-1, keepdims=True)
    return jnp.exp(s1) - jnp.exp(s2) + lam_init


def _finish(acc, l, lam, g, lam_init, tq):
    o = acc[:tq] / l[:tq] - lam * (acc[tq:] / l[tq:])
    return _rms_norm(o, g, SUBLN_EPS) * (1.0 - lam_init)


def _attn_kernel(lq1_ref, lk1_ref, lq2_ref, lk2_ref, g_ref, q_ref, k_ref, v_ref, km_ref, vm_ref,
                 o_ref, qq_ref, m_ref, l_ref, acc_ref, s0_ref, s1_ref, *, lam_init, tq, hp):
    nq = q_ref.shape[0] // tq
    lam = _lambda(lq1_ref, lk1_ref, lq2_ref, lk2_ref, lam_init)
    contract_last = (((1,), (1,)), ((), ()))
    contract_first = (((0,), (0,)), ((), ()))
    heads = [slice(h * V_DIM, (h + 1) * V_DIM) for h in range(hp)]

    def block_rows(b):
        return pl.ds(pl.multiple_of(b * tq, tq), tq)

    def scores(k, h):
        return lax.dot_general(k, qq_ref[h], contract_last, preferred_element_type=F32)

    def consume(h, s, v, first):
        blk_max = jnp.max(s, axis=0, keepdims=True)
        m_new = blk_max if first else jnp.maximum(m_ref[h], blk_max)
        p = jnp.exp2(s - m_new)
        pv = lax.dot_general(v, p.astype(BF16), contract_first, preferred_element_type=F32)
        psum = jnp.sum(p, axis=0, keepdims=True)
        if first:
            l_ref[h] = psum
            acc_ref[h] = pv
        else:
            alpha = jnp.exp2(m_ref[h] - m_new)
            l_ref[h] = alpha * l_ref[h] + psum
            acc_ref[h] = alpha * acc_ref[h] + pv
        m_ref[h] = m_new

    def interleave(producers, consumers, lead, per_consumer):
        producers = list(producers)
        for task in producers[:lead]:
            task()
        rest = producers[lead:]
        for c, task in enumerate(consumers):
            task()
            for extra in rest[c * per_consumer:(c + 1) * per_consumer]:
                extra()
        for extra in rest[len(consumers) * per_consumer:]:
            extra()

    def step(b, cur_ref, nxt_ref):
        def prefetch(h, hs):
            def run():
                nxt_ref[h] = scores(k_ref[block_rows(b + 1), hs], h)
            return run

        def use(h, hs):
            return lambda: consume(h, cur_ref[h], v_ref[block_rows(b), hs], False)

        producers = [prefetch(h, hs) for h, hs in enumerate(heads)] if nxt_ref is not None else []
        interleave(producers, [use(h, hs) for h, hs in enumerate(heads)], 2, 1)

    def q_tile(i, _):
        rows = block_rows(i)
        for h, hs in enumerate(heads):
            qq_ref[h] = _stack_queries(q_ref[rows, hs])

        key = lax.broadcasted_iota(jnp.int32, (N_META + tq, 2 * tq), 0) - N_META
        qry = lax.broadcasted_iota(jnp.int32, (N_META + tq, 2 * tq), 1)
        visible = key <= jnp.where(qry >= tq, qry - tq, qry)
        diag = [None] * hp

        def diag_scores(h, hs):
            def run():
                diag[h] = scores(jnp.concatenate([km_ref[:, hs], k_ref[rows, hs]], axis=0), h)
            return run

        def first_scores(h, hs):
            def run():
                s0_ref[h] = scores(k_ref[block_rows(0), hs], h)
            return run

        def use_diag(h, hs):
            return lambda: consume(h, jnp.where(visible, diag[h], jnp.finfo(F32).min),
                                   jnp.concatenate([vm_ref[:, hs], v_ref[rows, hs]], axis=0), True)

        producers = [diag_scores(h, hs) for h, hs in enumerate(heads)]
        producers = producers[:2] + [t for h, hs in enumerate(heads)
                                     for t in (producers[h + 2:h + 3] + [first_scores(h, hs)])]
        interleave(producers, [use_diag(h, hs) for h, hs in enumerate(heads)], 2, 2)

        def body(b, _):
            @pl.when(b % 2 == 0)
            def _():
                step(b, s0_ref, s1_ref)

            @pl.when(b % 2 == 1)
            def _():
                step(b, s1_ref, s0_ref)
            return 0

        lax.fori_loop(0, i - 1, body, 0)

        @pl.when((i >= 1) & (i % 2 == 1))
        def _():
            step(i - 1, s0_ref, None)

        @pl.when((i >= 1) & (i % 2 == 0))
        def _():
            step(i - 1, s1_ref, None)

        for h, hs in enumerate(heads):
            acc = acc_ref[h]
            l = l_ref[h]
            o = acc[:, :tq] / l[:, :tq] - lam * (acc[:, tq:] / l[:, tq:])
            o = o * lax.rsqrt(jnp.mean(o * o, axis=0, keepdims=True) + SUBLN_EPS)
            o = o * g_ref[...] * (1.0 - lam_init)
            o_ref[rows, hs] = o.T.astype(BF16)
        return 0

    lax.fori_loop(0, nq, q_tile, 0)


def _attn(lam_params, subln_g_col, proj, proj_meta, *, batch, lam_init, tq, hp):
    t = proj.shape[0]
    seq = t // batch
    assert seq % tq == 0 and N_HEADS % hp == 0
    groups = N_HEADS // hp
    width = hp * V_DIM
    small = lambda b, g: (0, 0)
    lam_specs = [pl.BlockSpec((1, HEAD_DIM), small)] * 4
    score_buf = pltpu.VMEM((hp, tq, 2 * tq), F32)
    est = (2 * 4 * seq * width * 2 + hp * (2 * tq * V_DIM * 2 + 2 * tq * 2 * tq * 4
                                           + 8 * tq * 2 * tq * 4 // hp + 2 * V_DIM * 2 * tq * 4))
    return pl.pallas_call(
        functools.partial(_attn_kernel, lam_init=lam_init, tq=tq, hp=hp),
        out_shape=jax.ShapeDtypeStruct((t, ATTN_WIDTH), BF16),
        grid=(batch, groups),
        in_specs=lam_specs + [
            pl.BlockSpec((V_DIM, 1), small),
            pl.BlockSpec((seq, width), lambda b, g: (b, g)),
            pl.BlockSpec((seq, width), lambda b, g: (b, groups + g)),
            pl.BlockSpec((seq, width), lambda b, g: (b, 2 * groups + g)),
            pl.BlockSpec((N_META, width), lambda b, g: (0, groups + g)),
            pl.BlockSpec((N_META, width), lambda b, g: (0, 2 * groups + g)),
        ],
        out_specs=pl.BlockSpec((seq, width), lambda b, g: (b, g)),
        scratch_shapes=[pltpu.VMEM((hp, 2 * tq, V_DIM), BF16),
                        pltpu.VMEM((hp, 1, 2 * tq), F32), pltpu.VMEM((hp, 1, 2 * tq), F32),
                        pltpu.VMEM((hp, V_DIM, 2 * tq), F32), score_buf, score_buf],
        compiler_params=pltpu.CompilerParams(
            dimension_semantics=("parallel", "parallel"),
            vmem_limit_bytes=_vmem_limit(est)),
        name="attn",
    )(*lam_params, subln_g_col, proj, proj, proj, proj_meta, proj_meta)


def _attn_meta_kernel(lq1_ref, lk1_ref, lq2_ref, lk2_ref, g_ref, q_ref, k_ref, v_ref, o_ref,
                      *, lam_init):
    qq = _stack_queries(q_ref[...])
    s = _scores(qq, k_ref[...])
    row = lax.broadcasted_iota(jnp.int32, s.shape, 0) % N_META
    col = lax.broadcasted_iota(jnp.int32, s.shape, 1)
    s = jnp.where(col <= row, s, jnp.finfo(F32).min)
    m = jnp.max(s, axis=-1, keepdims=True)
    p = jnp.exp2(s - m)
    l = jnp.sum(p, axis=-1, keepdims=True)
    acc = jnp.dot(p.astype(BF16), v_ref[...], preferred_element_type=F32)
    lam = _lambda(lq1_ref, lk1_ref, lq2_ref, lk2_ref, lam_init)
    o_ref[...] = _finish(acc, l, lam, g_ref[...], lam_init, N_META).astype(BF16)


def _attn_meta(lam_params, subln_g, proj_meta, *, lam_init):
    small = lambda h: (0, 0)
    return pl.pallas_call(
        functools.partial(_attn_meta_kernel, lam_init=lam_init),
        out_shape=jax.ShapeDtypeStruct((N_META, ATTN_WIDTH), BF16),
        grid=(N_HEADS,),
        in_specs=[pl.BlockSpec((1, HEAD_DIM), small)] * 4 + [
            pl.BlockSpec((1, V_DIM), small),
            pl.BlockSpec((N_META, V_DIM), lambda h: (0, h)),
            pl.BlockSpec((N_META, V_DIM), lambda h: (0, N_HEADS + h)),
            pl.BlockSpec((N_META, V_DIM), lambda h: (0, 2 * N_HEADS + h)),
        ],
        out_specs=pl.BlockSpec((N_META, V_DIM), lambda h: (0, h)),
        compiler_params=pltpu.CompilerParams(dimension_semantics=("arbitrary",)),
        name="attn_meta",
    )(*lam_params, subln_g, proj_meta, proj_meta, proj_meta)


def _outproj_kernel(h_ref, a_ref, d_ref, wp_ref, ps_ref, wo_ref, o_ref):
    diff = d_ref[...]
    pools = []
    for g in range(len(POOL_WINDOWS)):
        cols = slice(g * POOL_GROUP_WIDTH, (g + 1) * POOL_GROUP_WIDTH)
        pools.append(jnp.dot(diff[:, cols], wp_ref[g], preferred_element_type=F32))
    pool = (jnp.concatenate(pools, axis=1) * ps_ref[...]).astype(BF16)
    mix = jnp.concatenate([a_ref[...], pool], axis=1)
    o_ref[...] = h_ref[...] + jnp.dot(mix, wo_ref[...], preferred_element_type=F32)


def _outproj(h, attn, proj, w_pool, pool_scale, w_out, *, layer, tm):
    t = h.shape[0]
    assert t % tm == 0
    est = (2 * 2 * tm * D_MODEL * 4 + 2 * 2 * tm * ATTN_WIDTH * 2 + 2 * D_MODEL * D_MODEL * 2
           + 2 * POOL_WIDTH * POOL_GROUP_WIDTH * 2 + 3 * tm * D_MODEL * 4)
    return pl.pallas_call(
        _outproj_kernel,
        out_shape=jax.ShapeDtypeStruct((t, D_MODEL), F32),
        grid=(t // tm,),
        in_specs=[
            pl.BlockSpec((tm, D_MODEL), lambda i: (i, 0)),
            pl.BlockSpec((tm, ATTN_WIDTH), lambda i: (i, 0)),
            pl.BlockSpec((tm, POOL_WIDTH), lambda i: (i, 3)),
            pl.BlockSpec((None, len(POOL_WINDOWS), POOL_GROUP_WIDTH, POOL_GROUP_WIDTH),
                         lambda i: (layer, 0, 0, 0)),
            pl.BlockSpec((1, POOL_WIDTH), lambda i: (0, 0)),
            pl.BlockSpec((None, D_MODEL, D_MODEL), lambda i: (layer, 0, 0)),
        ],
        out_specs=pl.BlockSpec((tm, D_MODEL), lambda i: (i, 0)),
        compiler_params=pltpu.CompilerParams(
            dimension_semantics=("parallel",),
            vmem_limit_bytes=_vmem_limit(est)),
        name="outproj",
    )(h, attn, proj, w_pool, pool_scale, w_out)


def _to_head_layout(w_in):
    depth, half = w_in.shape[0], HEAD_DIM // 2
    qk = w_in[..., :2 * ATTN_WIDTH].reshape(depth, D_MODEL, 2 * N_HEADS, 2, 2, half)
    qk = qk.transpose(0, 1, 2, 4, 3, 5).reshape(depth, D_MODEL, 2 * ATTN_WIDTH)
    return jnp.concatenate([qk, w_in[..., 2 * ATTN_WIDTH:]], axis=-1)


def _rope_tables(length):
    pos = jnp.arange(length, dtype=F32)
    inv_freq = 1.0 / (ROPE_THETA ** (jnp.arange(0, HEAD_DIM, 2, dtype=F32) / HEAD_DIM))
    ang = pos[:, None] * inv_freq[None, :]
    ang = jnp.concatenate([ang, ang, ang, ang], axis=-1)
    sign = jnp.where(jnp.arange(LANES) < LANES // 2, -1.0, 1.0).astype(F32)
    cos, sin = jnp.cos(ang), jnp.sin(ang) * sign
    scale = HEAD_DIM ** -0.5 * math.log2(math.e)
    return cos * scale, sin * scale, cos, sin


def kernel(x, meta_tokens, ffn1_norm_g, ffn1_w_gate, ffn1_w_up, ffn1_w_down, mix_norm_g, w_in,
           lam_q1, lam_k1, lam_q2, lam_k2, subln_g, w_pool, pool_scale, w_out,
           ffn2_norm_g, ffn2_w_gate, ffn2_w_up, ffn2_w_down, final_norm_g):
    batch, seq, d = x.shape
    depth = w_in.shape[0]
    assert d == D_MODEL and meta_tokens.shape == (N_META, D_MODEL)

    tables = _rope_tables(N_META + seq)
    tab_meta = tuple(tb[:N_META] for tb in tables)
    tab_real = tuple(tb[N_META:] for tb in tables)
    final_g = final_norm_g.reshape(1, D_MODEL)

    h = x.reshape(batch * seq, D_MODEL)
    hm = meta_tokens.astype(x.dtype)

    def ffn_stack(w1, w2):
        return jnp.stack([w1, w2], axis=1).astype(BF16).reshape((2 * depth,) + w1.shape[1:])

    wg = ffn_stack(ffn1_w_gate, ffn2_w_gate)
    wu = ffn_stack(ffn1_w_up, ffn2_w_up)
    wd = ffn_stack(ffn1_w_down, ffn2_w_down)
    wi = _to_head_layout(w_in).astype(BF16)
    wo = w_out.astype(BF16)
    wp = w_pool.astype(BF16)
    g1 = [ffn1_norm_g[layer].reshape(1, D_MODEL) for layer in range(depth)]
    g2 = [ffn2_norm_g[layer].reshape(1, D_MODEL) for layer in range(depth)]

    def ffn(stream, gains, first, tm, apply_final=False):
        return _ffn(stream, gains, wg, wu, wd, final_g, first=first, tm=tm, tf=512,
                    apply_final=apply_final)

    hm = ffn(hm, [g1[0]], 0, N_META)
    h = ffn(h, [g1[0]], 0, 1024)
    for layer in range(depth):
        lam_init = 0.8 - 0.6 * math.exp(-0.3 * layer)
        last = layer == depth - 1
        gm = mix_norm_g[layer].reshape(1, D_MODEL)
        ps = pool_scale[layer].reshape(1, POOL_WIDTH)
        sg = subln_g[layer].reshape(1, V_DIM)
        lam_params = tuple(p[layer].reshape(1, HEAD_DIM) for p in (lam_q1, lam_k1, lam_q2, lam_k2))
        gains = [g2[layer]] if last else [g2[layer], g1[layer + 1]]

        proj_m, u_m = _inproj(hm, gm, wi, tab_meta, None, layer=layer, batch=1, tl=N_META, meta=True)
        attn_m = _attn_meta(lam_params, sg, proj_m, lam_init=lam_init)
        if not last:
            hm = _outproj(hm, attn_m, proj_m, wp, ps, wo, layer=layer, tm=N_META)
            hm = ffn(hm, gains, 2 * layer + 1, N_META)

        proj = _inproj(h, gm, wi, tab_real, u_m, layer=layer, batch=batch, tl=512, meta=False)
        attn = _attn(lam_params, sg.reshape(V_DIM, 1), proj, proj_m, batch=batch,
                     lam_init=lam_init, tq=256, hp=8)
        h = _outproj(h, attn, proj, wp, ps, wo, layer=layer, tm=512)
        h = ffn(h, gains, 2 * layer + 1, 1024, apply_final=last)

    return h.reshape(batch, seq, D_MODEL)
```

```python
import functools
import math

import jax
import jax.numpy as jnp
from jax import lax
from jax.experimental import pallas as pl
from jax.experimental.pallas import tpu as pltpu

F32 = jnp.float32
BF16 = jnp.bfloat16

D_MODEL = 2048
N_META = 16
ATTN_WIDTH = 1024
POOL_WIDTH = 1024
HEAD_DIM = 64
V_DIM = 2 * HEAD_DIM
N_HEADS = ATTN_WIDTH // V_DIM
POOL_WINDOWS = (2, 4, 8, 16)
POOL_GROUP_WIDTH = POOL_WIDTH // len(POOL_WINDOWS)
IN_WIDTH = 3 * ATTN_WIDTH + POOL_WIDTH
D_FF = 5632
ROPE_THETA = 10000.0
NORM_EPS = 1e-6
SUBLN_EPS = 1e-5

LANES = 128
V7X_VMEM_BYTES = 64 * 1024 * 1024


def _vmem_limit(estimate_bytes):
    return int(min(estimate_bytes * 5 // 4 + (4 << 20), V7X_VMEM_BYTES * 7 // 8))


def _rms_norm(x, g, eps):
    return x * lax.rsqrt(jnp.mean(x * x, axis=-1, keepdims=True) + eps) * g


def _ffn_kernel(x_ref, g_ref, wg_ref, wu_ref, wd_ref, fg_ref, o_ref, xn_ref, *, apply_final):
    f = pl.program_id(1)

    @pl.when(f == 0)
    def _():
        x = x_ref[...]
        xn_ref[...] = _rms_norm(x, g_ref[...], NORM_EPS).astype(BF16)
        o_ref[...] = x

    xn = xn_ref[...]
    gate = jnp.dot(xn, wg_ref[...], preferred_element_type=F32)
    up = jnp.dot(xn, wu_ref[...], preferred_element_type=F32)
    act = (gate * jax.nn.sigmoid(gate)) * (0.5 * up)
    o_ref[...] += jnp.dot(act.astype(BF16), wd_ref[...], preferred_element_type=F32)

    if apply_final:
        @pl.when(f == pl.num_programs(1) - 1)
        def _():
            o_ref[...] = _rms_norm(o_ref[...], fg_ref[...], NORM_EPS)


def _ffn(x, g, wg, wu, wd, final_g, *, layer, tm, tf, apply_final):
    t = x.shape[0]
    assert t % tm == 0 and D_FF % tf == 0
    est = (2 * 2 * tm * D_MODEL * 4 + tm * D_MODEL * 2 + 2 * 3 * D_MODEL * tf * 2
           + 3 * tm * tf * 4)
    return pl.pallas_call(
        functools.partial(_ffn_kernel, apply_final=apply_final),
        out_shape=jax.ShapeDtypeStruct((t, D_MODEL), F32),
        grid=(t // tm, D_FF // tf),
        in_specs=[
            pl.BlockSpec((tm, D_MODEL), lambda i, f: (i, 0)),
            pl.BlockSpec((1, D_MODEL), lambda i, f: (0, 0)),
            pl.BlockSpec((None, D_MODEL, tf), lambda i, f: (layer, 0, f)),
            pl.BlockSpec((None, D_MODEL, tf), lambda i, f: (layer, 0, f)),
            pl.BlockSpec((None, tf, D_MODEL), lambda i, f: (layer, f, 0)),
            pl.BlockSpec((1, D_MODEL), lambda i, f: (0, 0)),
        ],
        out_specs=pl.BlockSpec((tm, D_MODEL), lambda i, f: (i, 0)),
        scratch_shapes=[pltpu.VMEM((tm, D_MODEL), BF16)],
        compiler_params=pltpu.CompilerParams(
            dimension_semantics=("parallel", "arbitrary"),
            vmem_limit_bytes=_vmem_limit(est)),
        name="ffn",
    )(x, g, wg, wu, wd, final_g)


def _cast_plumbing(stacks, layer, n_steps, step_of):
    rows_gu = D_MODEL // n_steps
    rows_d = 2 * D_FF // n_steps
    assert D_MODEL % n_steps == 0 and (2 * D_FF) % n_steps == 0 and rows_d % 16 == 0

    def specs(lead):
        gu = pl.BlockSpec((None, rows_gu, D_FF), lambda *g: (lead, step_of(*g), 0))
        dn = pl.BlockSpec((None, rows_d, D_MODEL), lambda *g: (lead, step_of(*g) // 2, 0))
        return [gu, gu, dn]

    out_shapes = [jax.ShapeDtypeStruct((1,) + w.shape[1:], BF16) for w in stacks]
    return specs(layer), specs(0), out_shapes


def _cast_blocks(src_refs, dst_refs):
    for src, dst in zip(src_refs, dst_refs):
        dst[...] = src[...].astype(BF16)


def _rope(x, cos, sin_signed):
    outs = []
    for c in range(x.shape[1] // LANES):
        xc = x[:, c * LANES:(c + 1) * LANES]
        outs.append(xc * cos + pltpu.roll(xc, LANES // 2, axis=1) * sin_signed)
    return jnp.concatenate(outs, axis=1)


def _pool_diff(hist, cur, inv_count_fn):
    run = jnp.concatenate([hist, cur], axis=0)
    outs = []
    for g, w in enumerate(POOL_WINDOWS):
        run = run + pltpu.roll(run, w // 2, axis=0)
        cols = slice(g * POOL_GROUP_WIDTH, (g + 1) * POOL_GROUP_WIDTH)
        outs.append(run[N_META:, :POOL_GROUP_WIDTH] * inv_count_fn(w) - cur[:, cols])
        run = run[:, POOL_GROUP_WIDTH:]
    return jnp.concatenate(outs, axis=1)


def _inproj_kernel(*refs, meta):
    if meta:
        (x_ref, g_ref, w_ref, cq_ref, sq_ref, ck_ref, sk_ref,
         o_ref, u_ref, hist_ref) = refs
    else:
        (x_ref, g_ref, w_ref, cq_ref, sq_ref, ck_ref, sk_ref, um_ref, *cast_src) = refs[:-5]
        o_ref, *cast_dst, hist_ref = refs[-5:]
        _cast_blocks(cast_src, cast_dst)
    j = pl.program_id(1)
    rows = x_ref.shape[0]
    chunk = lambda n: slice(n * ATTN_WIDTH, (n + 1) * ATTN_WIDTH)

    if not meta:
        @pl.when(j == 0)
        def _():
            hist_ref[...] = um_ref[...]

    xn = _rms_norm(x_ref[...], g_ref[...], NORM_EPS).astype(BF16)

    def project(n):
        return jnp.dot(xn, w_ref[:, chunk(n)], preferred_element_type=F32)

    o_ref[:, chunk(0)] = _rope(project(0), cq_ref[...], sq_ref[...]).astype(BF16)
    o_ref[:, chunk(1)] = _rope(project(1), ck_ref[...], sk_ref[...]).astype(BF16)
    o_ref[:, chunk(2)] = project(2).astype(BF16)

    acc = project(3)
    if meta:
        u_ref[...] = acc
        hist = jnp.zeros((N_META, POOL_WIDTH), F32)
        pos = lax.broadcasted_iota(jnp.int32, (rows, 1), 0)

        def inv_count(w):
            return 1.0 / jnp.minimum(pos + 1, w).astype(F32)
    else:
        hist = hist_ref[...]

        def inv_count(w):
            return 1.0 / w
    o_ref[:, chunk(3)] = _pool_diff(hist, acc, inv_count).astype(BF16)
    hist_ref[...] = acc[rows - N_META:, :]


def _inproj(x, g, w_in, tables, u_meta, cast_stacks=None, *, layer, batch, tl, meta):
    t = x.shape[0]
    seq = t // batch
    nj = seq // tl
    assert seq % tl == 0
    cq, sq, ck, sk = tables
    tok = lambda b, j: (b * nj + j, 0)
    tab = lambda b, j: (j, 0)
    const = lambda b, j: (0, 0)
    in_specs = [
        pl.BlockSpec((tl, D_MODEL), tok),
        pl.BlockSpec((1, D_MODEL), const),
        pl.BlockSpec((None, D_MODEL, IN_WIDTH), lambda b, j: (layer, 0, 0),
                     pipeline_mode=pl.Buffered(1)),
        pl.BlockSpec((tl, LANES), tab),
        pl.BlockSpec((tl, LANES), tab),
        pl.BlockSpec((tl, LANES), tab),
        pl.BlockSpec((tl, LANES), tab),
    ]
    args = [x, g, w_in, cq, sq, ck, sk]
    proj_shape = jax.ShapeDtypeStruct((t, IN_WIDTH), BF16)
    proj_spec = pl.BlockSpec((tl, IN_WIDTH), tok)
    if meta:
        out_shape = (proj_shape, jax.ShapeDtypeStruct((t, POOL_WIDTH), F32))
        out_specs = (proj_spec, pl.BlockSpec((tl, POOL_WIDTH), const))
    else:
        cast_in, cast_out, cast_shapes = _cast_plumbing(cast_stacks, layer, batch * nj,
                                                        lambda b, j: b * nj + j)
        in_specs += [pl.BlockSpec((N_META, POOL_WIDTH), const)] + cast_in
        args += [u_meta] + list(cast_stacks)
        out_shape = [proj_shape] + cast_shapes
        out_specs = [proj_spec] + cast_out
    est = (2 * tl * D_MODEL * 4 + D_MODEL * IN_WIDTH * 2 + 2 * tl * IN_WIDTH * 2
           + tl * D_MODEL * 2 + 4 * (tl + N_META) * POOL_WIDTH * 4 + 8 * tl * LANES * 4
           + (0 if meta else 2 * 6 * 3 * D_MODEL * D_FF // (batch * nj)))
    return pl.pallas_call(
        functools.partial(_inproj_kernel, meta=meta),
        out_shape=out_shape,
        grid=(batch, nj),
        in_specs=in_specs,
        out_specs=out_specs,
        scratch_shapes=[pltpu.VMEM((N_META, POOL_WIDTH), F32)],
        compiler_params=pltpu.CompilerParams(
            dimension_semantics=("arbitrary", "arbitrary"),
            vmem_limit_bytes=_vmem_limit(est)),
        name="inproj_meta" if meta else "inproj",
    )(*args)


def _stack_queries(q):
    lane = lax.broadcasted_iota(jnp.int32, q.shape, 1)
    first_map = (lane % HEAD_DIM) < HEAD_DIM // 2
    zero = jnp.zeros_like(q)
    return jnp.concatenate([jnp.where(first_map, q, zero), jnp.where(first_map, zero, q)], axis=0)


def _scores(qq, k):
    return lax.dot_general(qq, k, (((1,), (1,)), ((), ())), preferred_element_type=F32)


def _lambda(lq1_ref, lk1_ref, lq2_ref, lk2_ref, lam_init):
    s1 = jnp.sum(lq1_ref[...] * lk1_ref[...], axis=-1, keepdims=True)
    s2 = jnp.sum(lq2_ref[...] * lk2_ref[...], axis=-1, keepdims=True)
    return jnp.exp(s1) - jnp.exp(s2) + lam_init


def _finish(acc, l, lam, g, lam_init, tq):
    o = acc[:tq] / l[:tq] - lam * (acc[tq:] / l[tq:])
    return _rms_norm(o, g, SUBLN_EPS) * (1.0 - lam_init)


def _attn_kernel(lq1_ref, lk1_ref, lq2_ref, lk2_ref, g_ref, q_ref, k_ref, v_ref, km_ref, vm_ref,
                 o_ref, qq_ref, m_ref, l_ref, acc_ref, s0_ref, s1_ref, *, lam_init, tq, hp):
    nq = q_ref.shape[0] // tq
    lam = _lambda(lq1_ref, lk1_ref, lq2_ref, lk2_ref, lam_init)
    contract_last = (((1,), (1,)), ((), ()))
    contract_first = (((0,), (0,)), ((), ()))
    heads = [slice(h * V_DIM, (h + 1) * V_DIM) for h in range(hp)]

    def block_rows(b):
        return pl.ds(pl.multiple_of(b * tq, tq), tq)

    def scores(k, h):
        return lax.dot_general(k, qq_ref[h], contract_last, preferred_element_type=F32)

    def consume(h, s, v, first):
        blk_max = jnp.max(s, axis=0, keepdims=True)
        m_new = blk_max if first else jnp.maximum(m_ref[h], blk_max)
        p = jnp.exp2(s - m_new)
        pv = lax.dot_general(v, p.astype(BF16), contract_first, preferred_element_type=F32)
        psum = jnp.sum(p, axis=0, keepdims=True)
        if first:
            l_ref[h] = psum
            acc_ref[h] = pv
        else:
            alpha = jnp.exp2(m_ref[h] - m_new)
            l_ref[h] = alpha * l_ref[h] + psum
            acc_ref[h] = alpha * acc_ref[h] + pv
        m_ref[h] = m_new

    def interleave(producers, consumers, lead, per_consumer):
        producers = list(producers)
        for task in producers[:lead]:
            task()
        rest = producers[lead:]
        for c, task in enumerate(consumers):
            task()
            for extra in rest[c * per_consumer:(c + 1) * per_consumer]:
                extra()
        for extra in rest[len(consumers) * per_consumer:]:
            extra()

    def step(b, cur_ref, nxt_ref):
        def prefetch(h, hs):
            def run():
                nxt_ref[h] = scores(k_ref[block_rows(b + 1), hs], h)
            return run

        def use(h, hs):
            return lambda: consume(h, cur_ref[h], v_ref[block_rows(b), hs], False)

        producers = [prefetch(h, hs) for h, hs in enumerate(heads)] if nxt_ref is not None else []
        interleave(producers, [use(h, hs) for h, hs in enumerate(heads)], 2, 1)

    def q_tile(i, _):
        rows = block_rows(i)
        for h, hs in enumerate(heads):
            qq_ref[h] = _stack_queries(q_ref[rows, hs])

        key = lax.broadcasted_iota(jnp.int32, (N_META + tq, 2 * tq), 0) - N_META
        qry = lax.broadcasted_iota(jnp.int32, (N_META + tq, 2 * tq), 1)
        visible = key <= jnp.where(qry >= tq, qry - tq, qry)
        diag = [None] * hp

        def diag_scores(h, hs):
            def run():
                diag[h] = scores(jnp.concatenate([km_ref[:, hs], k_ref[rows, hs]], axis=0), h)
            return run

        def first_scores(h, hs):
            def run():
                s0_ref[h] = scores(k_ref[block_rows(0), hs], h)
            return run

        def use_diag(h, hs):
            return lambda: consume(h, jnp.where(visible, diag[h], jnp.finfo(F32).min),
                                   jnp.concatenate([vm_ref[:, hs], v_ref[rows, hs]], axis=0), True)

        producers = [diag_scores(h, hs) for h, hs in enumerate(heads)]
        producers = producers[:2] + [t for h, hs in enumerate(heads)
                                     for t in (producers[h + 2:h + 3] + [first_scores(h, hs)])]
        interleave(producers, [use_diag(h, hs) for h, hs in enumerate(heads)], 2, 2)

        def body(b, _):
            @pl.when(b % 2 == 0)
            def _():
                step(b, s0_ref, s1_ref)

            @pl.when(b % 2 == 1)
            def _():
                step(b, s1_ref, s0_ref)
            return 0

        lax.fori_loop(0, i - 1, body, 0)

        @pl.when((i >= 1) & (i % 2 == 1))
        def _():
            step(i - 1, s0_ref, None)

        @pl.when((i >= 1) & (i % 2 == 0))
        def _():
            step(i - 1, s1_ref, None)

        for h, hs in enumerate(heads):
            acc = acc_ref[h]
            l = l_ref[h]
            o = acc[:, :tq] / l[:, :tq] - lam * (acc[:, tq:] / l[:, tq:])
            o = o * lax.rsqrt(jnp.mean(o * o, axis=0, keepdims=True) + SUBLN_EPS)
            o = o * g_ref[...] * (1.0 - lam_init)
            o_ref[rows, hs] = o.T.astype(BF16)
        return 0

    lax.fori_loop(0, nq, q_tile, 0)


def _attn(lam_params, subln_g_col, proj, proj_meta, *, batch, lam_init, tq, hp):
    t = proj.shape[0]
    seq = t // batch
    assert seq % tq == 0 and N_HEADS % hp == 0
    groups = N_HEADS // hp
    width = hp * V_DIM
    small = lambda b, g: (0, 0)
    lam_specs = [pl.BlockSpec((1, HEAD_DIM), small)] * 4
    score_buf = pltpu.VMEM((hp, tq, 2 * tq), F32)
    est = (2 * 4 * seq * width * 2 + hp * (2 * tq * V_DIM * 2 + 2 * tq * 2 * tq * 4
                                           + 8 * tq * 2 * tq * 4 // hp + 2 * V_DIM * 2 * tq * 4))
    return pl.pallas_call(
        functools.partial(_attn_kernel, lam_init=lam_init, tq=tq, hp=hp),
        out_shape=jax.ShapeDtypeStruct((t, ATTN_WIDTH), BF16),
        grid=(batch, groups),
        in_specs=lam_specs + [
            pl.BlockSpec((V_DIM, 1), small),
            pl.BlockSpec((seq, width), lambda b, g: (b, g)),
            pl.BlockSpec((seq, width), lambda b, g: (b, groups + g)),
            pl.BlockSpec((seq, width), lambda b, g: (b, 2 * groups + g)),
            pl.BlockSpec((N_META, width), lambda b, g: (0, groups + g)),
            pl.BlockSpec((N_META, width), lambda b, g: (0, 2 * groups + g)),
        ],
        out_specs=pl.BlockSpec((seq, width), lambda b, g: (b, g)),
        scratch_shapes=[pltpu.VMEM((hp, 2 * tq, V_DIM), BF16),
                        pltpu.VMEM((hp, 1, 2 * tq), F32), pltpu.VMEM((hp, 1, 2 * tq), F32),
                        pltpu.VMEM((hp, V_DIM, 2 * tq), F32), score_buf, score_buf],
        compiler_params=pltpu.CompilerParams(
            dimension_semantics=("parallel", "parallel"),
            vmem_limit_bytes=_vmem_limit(est)),
        name="attn",
    )(*lam_params, subln_g_col, proj, proj, proj, proj_meta, proj_meta)


def _attn_meta_kernel(lq1_ref, lk1_ref, lq2_ref, lk2_ref, g_ref, q_ref, k_ref, v_ref, o_ref,
                      *, lam_init):
    qq = _stack_queries(q_ref[...])
    s = _scores(qq, k_ref[...])
    row = lax.broadcasted_iota(jnp.int32, s.shape, 0) % N_META
    col = lax.broadcasted_iota(jnp.int32, s.shape, 1)
    s = jnp.where(col <= row, s, jnp.finfo(F32).min)
    m = jnp.max(s, axis=-1, keepdims=True)
    p = jnp.exp2(s - m)
    l = jnp.sum(p, axis=-1, keepdims=True)
    acc = jnp.dot(p.astype(BF16), v_ref[...], preferred_element_type=F32)
    lam = _lambda(lq1_ref, lk1_ref, lq2_ref, lk2_ref, lam_init)
    o_ref[...] = _finish(acc, l, lam, g_ref[...], lam_init, N_META).astype(BF16)


def _attn_meta(lam_params, subln_g, proj_meta, *, lam_init):
    small = lambda h: (0, 0)
    return pl.pallas_call(
        functools.partial(_attn_meta_kernel, lam_init=lam_init),
        out_shape=jax.ShapeDtypeStruct((N_META, ATTN_WIDTH), BF16),
        grid=(N_HEADS,),
        in_specs=[pl.BlockSpec((1, HEAD_DIM), small)] * 4 + [
            pl.BlockSpec((1, V_DIM), small),
            pl.BlockSpec((N_META, V_DIM), lambda h: (0, h)),
            pl.BlockSpec((N_META, V_DIM), lambda h: (0, N_HEADS + h)),
            pl.BlockSpec((N_META, V_DIM), lambda h: (0, 2 * N_HEADS + h)),
        ],
        out_specs=pl.BlockSpec((N_META, V_DIM), lambda h: (0, h)),
        compiler_params=pltpu.CompilerParams(dimension_semantics=("arbitrary",)),
        name="attn_meta",
    )(*lam_params, subln_g, proj_meta, proj_meta, proj_meta)


def _outproj_kernel(h_ref, a_ref, d_ref, wp_ref, ps_ref, wo_ref, *rest):
    cast_src, (o_ref, *cast_dst) = rest[:len(rest) // 2], rest[len(rest) // 2:]
    _cast_blocks(cast_src, cast_dst)
    diff = d_ref[...]
    pools = []
    for g in range(len(POOL_WINDOWS)):
        cols = slice(g * POOL_GROUP_WIDTH, (g + 1) * POOL_GROUP_WIDTH)
        pools.append(jnp.dot(diff[:, cols], wp_ref[g], preferred_element_type=F32))
    pool = (jnp.concatenate(pools, axis=1) * ps_ref[...]).astype(BF16)
    mix = jnp.concatenate([a_ref[...], pool], axis=1)
    o_ref[...] = h_ref[...] + jnp.dot(mix, wo_ref[...], preferred_element_type=F32)


def _outproj(h, attn, proj, w_pool, pool_scale, w_out, cast_stacks=None, *, layer, cast_layer=0,
             tm):
    t = h.shape[0]
    assert t % tm == 0
    est = (2 * 2 * tm * D_MODEL * 4 + 2 * 2 * tm * ATTN_WIDTH * 2 + 2 * D_MODEL * D_MODEL * 2
           + 2 * POOL_WIDTH * POOL_GROUP_WIDTH * 2 + 3 * tm * D_MODEL * 4)
    in_specs = [
        pl.BlockSpec((tm, D_MODEL), lambda i: (i, 0)),
        pl.BlockSpec((tm, ATTN_WIDTH), lambda i: (i, 0)),
        pl.BlockSpec((tm, POOL_WIDTH), lambda i: (i, 3)),
        pl.BlockSpec((None, len(POOL_WINDOWS), POOL_GROUP_WIDTH, POOL_GROUP_WIDTH),
                     lambda i: (layer, 0, 0, 0)),
        pl.BlockSpec((1, POOL_WIDTH), lambda i: (0, 0)),
        pl.BlockSpec((None, D_MODEL, D_MODEL), lambda i: (layer, 0, 0)),
    ]
    args = [h, attn, proj, w_pool, pool_scale, w_out]
    out_shape = [jax.ShapeDtypeStruct((t, D_MODEL), F32)]
    out_specs = [pl.BlockSpec((tm, D_MODEL), lambda i: (i, 0))]
    if cast_stacks is not None:
        cast_in, cast_out, cast_shapes = _cast_plumbing(cast_stacks, cast_layer, t // tm,
                                                        lambda i: i)
        in_specs += cast_in
        args += list(cast_stacks)
        out_shape += cast_shapes
        out_specs += cast_out
        est += 2 * 6 * 3 * D_MODEL * D_FF // (t // tm)
    outs = pl.pallas_call(
        _outproj_kernel,
        out_shape=out_shape,
        grid=(t // tm,),
        in_specs=in_specs,
        out_specs=out_specs,
        compiler_params=pltpu.CompilerParams(
            dimension_semantics=("arbitrary",),
            vmem_limit_bytes=_vmem_limit(est)),
        name="outproj",
    )(*args)
    return outs[0] if cast_stacks is None else outs


def _to_head_layout(w_in):
    depth, half = w_in.shape[0], HEAD_DIM // 2
    qk = w_in[..., :2 * ATTN_WIDTH].reshape(depth, D_MODEL, 2 * N_HEADS, 2, 2, half)
    qk = qk.transpose(0, 1, 2, 4, 3, 5).reshape(depth, D_MODEL, 2 * ATTN_WIDTH)
    return jnp.concatenate([qk, w_in[..., 2 * ATTN_WIDTH:]], axis=-1)


def _rope_tables(length):
    pos = jnp.arange(length, dtype=F32)
    inv_freq = 1.0 / (ROPE_THETA ** (jnp.arange(0, HEAD_DIM, 2, dtype=F32) / HEAD_DIM))
    ang = pos[:, None] * inv_freq[None, :]
    ang = jnp.concatenate([ang, ang, ang, ang], axis=-1)
    sign = jnp.where(jnp.arange(LANES) < LANES // 2, -1.0, 1.0).astype(F32)
    cos, sin = jnp.cos(ang), jnp.sin(ang) * sign
    scale = HEAD_DIM ** -0.5 * math.log2(math.e)
    return cos * scale, sin * scale, cos, sin


def kernel(x, meta_tokens, ffn1_norm_g, ffn1_w_gate, ffn1_w_up, ffn1_w_down, mix_norm_g, w_in,
           lam_q1, lam_k1, lam_q2, lam_k2, subln_g, w_pool, pool_scale, w_out,
           ffn2_norm_g, ffn2_w_gate, ffn2_w_up, ffn2_w_down, final_norm_g):
    batch, seq, d = x.shape
    depth = w_in.shape[0]
    assert d == D_MODEL and meta_tokens.shape == (N_META, D_MODEL)

    tables = _rope_tables(N_META + seq)
    tab_meta = tuple(tb[:N_META] for tb in tables)
    tab_real = tuple(tb[N_META:] for tb in tables)
    final_g = final_norm_g.reshape(1, D_MODEL)

    h = x.reshape(batch * seq, D_MODEL)
    hm = meta_tokens.astype(x.dtype)

    wi = _to_head_layout(w_in).astype(BF16)
    wo = w_out.astype(BF16)
    wp = w_pool.astype(BF16)
    ffn1_stacks = (ffn1_w_gate, ffn1_w_up, ffn1_w_down)
    ffn2_stacks = (ffn2_w_gate, ffn2_w_up, ffn2_w_down)
    ffn1_w = tuple(w[:1].astype(BF16) for w in ffn1_stacks)

    for layer in range(depth):
        lam_init = 0.8 - 0.6 * math.exp(-0.3 * layer)
        last = layer == depth - 1
        g1 = ffn1_norm_g[layer].reshape(1, D_MODEL)
        gm = mix_norm_g[layer].reshape(1, D_MODEL)
        g2 = ffn2_norm_g[layer].reshape(1, D_MODEL)
        ps = pool_scale[layer].reshape(1, POOL_WIDTH)
        sg = subln_g[layer].reshape(1, V_DIM)
        lam_params = tuple(p[layer].reshape(1, HEAD_DIM) for p in (lam_q1, lam_k1, lam_q2, lam_k2))

        hm = _ffn(hm, g1, *ffn1_w, final_g, layer=0, tm=N_META, tf=512, apply_final=False)
        proj_m, u_m = _inproj(hm, gm, wi, tab_meta, None, layer=layer, batch=1, tl=N_META, meta=True)
        attn_m = _attn_meta(lam_params, sg, proj_m, lam_init=lam_init)
        h = _ffn(h, g1, *ffn1_w, final_g, layer=0, tm=1024, tf=512, apply_final=False)
        proj, *ffn2_w = _inproj(h, gm, wi, tab_real, u_m, ffn2_stacks, layer=layer, batch=batch,
                                tl=512, meta=False)
        attn = _attn(lam_params, sg.reshape(V_DIM, 1), proj, proj_m, batch=batch,
                     lam_init=lam_init, tq=256, hp=8)

        if last:
            h = _outproj(h, attn, proj, wp, ps, wo, layer=layer, tm=512)
        else:
            h, *ffn1_w = _outproj(h, attn, proj, wp, ps, wo, ffn1_stacks, layer=layer,
                                  cast_layer=layer + 1, tm=512)
            hm = _outproj(hm, attn_m, proj_m, wp, ps, wo, layer=layer, tm=N_META)
            hm = _ffn(hm, g2, *ffn2_w, final_g, layer=0, tm=N_META, tf=512, apply_final=False)
        h = _ffn(h, g2, *ffn2_w, final_g, layer=0, tm=1024, tf=512, apply_final=last)

    return h.reshape(batch, seq, D_MODEL)
```

```python
import functools
import math

import jax
import jax.numpy as jnp
from jax import lax
from jax.experimental import pallas as pl
from jax.experimental.pallas import tpu as pltpu

F32 = jnp.float32
BF16 = jnp.bfloat16

D_MODEL = 2048
N_META = 16
ATTN_WIDTH = 1024
POOL_WIDTH = 1024
HEAD_DIM = 64
V_DIM = 2 * HEAD_DIM
N_HEADS = ATTN_WIDTH // V_DIM
POOL_WINDOWS = (2, 4, 8, 16)
POOL_GROUP_WIDTH = POOL_WIDTH // len(POOL_WINDOWS)
IN_WIDTH = 3 * ATTN_WIDTH + POOL_WIDTH
D_FF = 5632
ROPE_THETA = 10000.0
NORM_EPS = 1e-6
SUBLN_EPS = 1e-5

LANES = 128
V7X_VMEM_BYTES = 64 * 1024 * 1024


def _vmem_limit(estimate_bytes):
    return int(min(estimate_bytes * 5 // 4 + (4 << 20), V7X_VMEM_BYTES * 7 // 8))


def _rms_norm(x, g, eps):
    return x * lax.rsqrt(jnp.mean(x * x, axis=-1, keepdims=True) + eps) * g


def _ffn_kernel(x_ref, g_ref, wg_ref, wu_ref, wd_ref, fg_ref, o_ref, xn_ref, *, apply_final):
    f = pl.program_id(1)

    @pl.when(f == 0)
    def _():
        x = x_ref[...]
        xn_ref[...] = _rms_norm(x, g_ref[...], NORM_EPS).astype(BF16)
        o_ref[...] = x

    xn = xn_ref[...]
    gate = jnp.dot(xn, wg_ref[...], preferred_element_type=F32)
    up = jnp.dot(xn, wu_ref[...], preferred_element_type=F32)
    act = (gate * jax.nn.sigmoid(gate)) * (0.5 * up)
    o_ref[...] += jnp.dot(act.astype(BF16), wd_ref[...], preferred_element_type=F32)

    if apply_final:
        @pl.when(f == pl.num_programs(1) - 1)
        def _():
            o_ref[...] = _rms_norm(o_ref[...], fg_ref[...], NORM_EPS)


def _ffn(x, g, wg, wu, wd, final_g, *, layer, tm, tf, apply_final):
    t = x.shape[0]
    assert t % tm == 0 and D_FF % tf == 0
    est = (2 * 2 * tm * D_MODEL * 4 + tm * D_MODEL * 2 + 2 * 3 * D_MODEL * tf * 2
           + 3 * tm * tf * 4)
    return pl.pallas_call(
        functools.partial(_ffn_kernel, apply_final=apply_final),
        out_shape=jax.ShapeDtypeStruct((t, D_MODEL), F32),
        grid=(t // tm, D_FF // tf),
        in_specs=[
            pl.BlockSpec((tm, D_MODEL), lambda i, f: (i, 0)),
            pl.BlockSpec((1, D_MODEL), lambda i, f: (0, 0)),
            pl.BlockSpec((None, D_MODEL, tf), lambda i, f: (layer, 0, f)),
            pl.BlockSpec((None, D_MODEL, tf), lambda i, f: (layer, 0, f)),
            pl.BlockSpec((None, tf, D_MODEL), lambda i, f: (layer, f, 0)),
            pl.BlockSpec((1, D_MODEL), lambda i, f: (0, 0)),
        ],
        out_specs=pl.BlockSpec((tm, D_MODEL), lambda i, f: (i, 0)),
        scratch_shapes=[pltpu.VMEM((tm, D_MODEL), BF16)],
        compiler_params=pltpu.CompilerParams(
            dimension_semantics=("parallel", "arbitrary"),
            vmem_limit_bytes=_vmem_limit(est)),
        name="ffn",
    )(x, g, wg, wu, wd, final_g)


def _cast_plumbing(stacks, layer, n_steps, step_of):
    rows_gu = D_MODEL // n_steps
    rows_d = 2 * D_FF // n_steps
    assert D_MODEL % n_steps == 0 and (2 * D_FF) % n_steps == 0 and rows_d % 16 == 0

    def specs(lead):
        gu = pl.BlockSpec((None, rows_gu, D_FF), lambda *g: (lead, step_of(*g), 0))
        dn = pl.BlockSpec((None, rows_d, D_MODEL), lambda *g: (lead, step_of(*g) // 2, 0))
        return [gu, gu, dn]

    out_shapes = [jax.ShapeDtypeStruct((1,) + w.shape[1:], BF16) for w in stacks]
    return specs(layer), specs(0), out_shapes


def _cast_blocks(src_refs, dst_refs):
    for src, dst in zip(src_refs, dst_refs):
        dst[...] = src[...].astype(BF16)


def _cast_kernel(*refs):
    _cast_blocks(refs[:len(refs) // 2], refs[len(refs) // 2:])


def _cast_ffn_weights(stacks, layer, n_steps=32):
    in_specs, out_specs, out_shapes = _cast_plumbing(stacks, layer, n_steps, lambda i: i)
    est = 2 * 6 * 3 * D_MODEL * D_FF // n_steps
    return pl.pallas_call(
        _cast_kernel,
        out_shape=out_shapes,
        grid=(n_steps,),
        in_specs=in_specs,
        out_specs=out_specs,
        compiler_params=pltpu.CompilerParams(
            dimension_semantics=("arbitrary",),
            vmem_limit_bytes=_vmem_limit(est)),
        name="cast_ffn_weights",
    )(*stacks)


def _rope(x, cos, sin_signed):
    lane = lax.broadcasted_iota(jnp.int32, (x.shape[0], LANES), 1)
    upper = (lane % HEAD_DIM) >= (HEAD_DIM // 2)
    outs = []
    for c in range(x.shape[1] // LANES):
        xc = x[:, c * LANES:(c + 1) * LANES]
        from_below = pltpu.roll(xc, HEAD_DIM // 2, axis=1)
        from_above = pltpu.roll(xc, LANES - HEAD_DIM // 2, axis=1)
        outs.append(xc * cos + jnp.where(upper, from_below, from_above) * sin_signed)
    return jnp.concatenate(outs, axis=1)


def _pool_diff(hist, cur, inv_count_fn):
    run = jnp.concatenate([hist, cur], axis=0)
    outs = []
    for g, w in enumerate(POOL_WINDOWS):
        run = run + pltpu.roll(run, w // 2, axis=0)
        cols = slice(g * POOL_GROUP_WIDTH, (g + 1) * POOL_GROUP_WIDTH)
        outs.append(run[N_META:, :POOL_GROUP_WIDTH] * inv_count_fn(w) - cur[:, cols])
        run = run[:, POOL_GROUP_WIDTH:]
    return jnp.concatenate(outs, axis=1)


def _inproj_kernel(*refs, meta):
    if meta:
        (x_ref, g_ref, w_ref, cq_ref, sq_ref, ck_ref, sk_ref,
         o_ref, u_ref, hist_ref) = refs
    else:
        (x_ref, g_ref, w_ref, cq_ref, sq_ref, ck_ref, sk_ref, um_ref, *cast_src) = refs[:-5]
        o_ref, *cast_dst, hist_ref = refs[-5:]
        _cast_blocks(cast_src, cast_dst)
    j = pl.program_id(1)
    rows = x_ref.shape[0]
    chunk = lambda n: slice(n * ATTN_WIDTH, (n + 1) * ATTN_WIDTH)

    if not meta:
        @pl.when(j == 0)
        def _():
            hist_ref[...] = um_ref[...]

    xn = _rms_norm(x_ref[...], g_ref[...], NORM_EPS).astype(BF16)

    def project(n):
        return jnp.dot(xn, w_ref[:, chunk(n)], preferred_element_type=F32)

    o_ref[:, chunk(0)] = _rope(project(0), cq_ref[...], sq_ref[...]).astype(BF16)
    o_ref[:, chunk(1)] = _rope(project(1), ck_ref[...], sk_ref[...]).astype(BF16)
    o_ref[:, chunk(2)] = project(2).astype(BF16)

    acc = project(3)
    if meta:
        u_ref[...] = acc
        hist = jnp.zeros((N_META, POOL_WIDTH), F32)
        pos = lax.broadcasted_iota(jnp.int32, (rows, 1), 0)

        def inv_count(w):
            return 1.0 / jnp.minimum(pos + 1, w).astype(F32)
    else:
        hist = hist_ref[...]

        def inv_count(w):
            return 1.0 / w
    o_ref[:, chunk(3)] = _pool_diff(hist, acc, inv_count).astype(BF16)
    hist_ref[...] = acc[rows - N_META:, :]


def _inproj(x, g, w_in, tables, u_meta, cast_stacks=None, *, layer, batch, tl, meta):
    t = x.shape[0]
    seq = t // batch
    nj = seq // tl
    assert seq % tl == 0
    cq, sq, ck, sk = tables
    tok = lambda b, j: (b * nj + j, 0)
    tab = lambda b, j: (j, 0)
    const = lambda b, j: (0, 0)
    in_specs = [
        pl.BlockSpec((tl, D_MODEL), tok),
        pl.BlockSpec((1, D_MODEL), const),
        pl.BlockSpec((None, D_MODEL, IN_WIDTH), lambda b, j: (layer, 0, 0),
                     pipeline_mode=pl.Buffered(1)),
        pl.BlockSpec((tl, LANES), tab),
        pl.BlockSpec((tl, LANES), tab),
        pl.BlockSpec((tl, LANES), tab),
        pl.BlockSpec((tl, LANES), tab),
    ]
    args = [x, g, w_in, cq, sq, ck, sk]
    proj_shape = jax.ShapeDtypeStruct((t, IN_WIDTH), BF16)
    proj_spec = pl.BlockSpec((tl, IN_WIDTH), tok)
    if meta:
        out_shape = (proj_shape, jax.ShapeDtypeStruct((t, POOL_WIDTH), F32))
        out_specs = (proj_spec, pl.BlockSpec((tl, POOL_WIDTH), const))
    else:
        cast_in, cast_out, cast_shapes = _cast_plumbing(cast_stacks, layer, batch * nj,
                                                        lambda b, j: b * nj + j)
        in_specs += [pl.BlockSpec((N_META, POOL_WIDTH), const)] + cast_in
        args += [u_meta] + list(cast_stacks)
        out_shape = [proj_shape] + cast_shapes
        out_specs = [proj_spec] + cast_out
    est = (2 * tl * D_MODEL * 4 + D_MODEL * IN_WIDTH * 2 + 2 * tl * IN_WIDTH * 2
           + tl * D_MODEL * 2 + 4 * (tl + N_META) * POOL_WIDTH * 4 + 8 * tl * LANES * 4
           + (0 if meta else 2 * 6 * 3 * D_MODEL * D_FF // (batch * nj)))
    return pl.pallas_call(
        functools.partial(_inproj_kernel, meta=meta),
        out_shape=out_shape,
        grid=(batch, nj),
        in_specs=in_specs,
        out_specs=out_specs,
        scratch_shapes=[pltpu.VMEM((N_META, POOL_WIDTH), F32)],
        compiler_params=pltpu.CompilerParams(
            dimension_semantics=("arbitrary", "arbitrary"),
            vmem_limit_bytes=_vmem_limit(est)),
        name="inproj_meta" if meta else "inproj",
    )(*args)


def _stack_queries(q):
    lane = lax.broadcasted_iota(jnp.int32, q.shape, 1)
    zero = jnp.zeros_like(q)
    return jnp.concatenate([jnp.where(lane < HEAD_DIM, q, zero),
                            jnp.where(lane >= HEAD_DIM, q, zero)], axis=0)


def _scores(qq, k):
    return lax.dot_general(qq, k, (((1,), (1,)), ((), ())), preferred_element_type=F32)


def _lambda(lq1_ref, lk1_ref, lq2_ref, lk2_ref, lam_init):
    s1 = jnp.sum(lq1_ref[...] * lk1_ref[...], axis=-1, keepdims=True)
    s2 = jnp.sum(lq2_ref[...] * lk2_ref[...], axis=-1, keepdims=True)
    return jnp.exp(s1) - jnp.exp(s2) + lam_init


def _finish(acc, l, lam, g, lam_init, tq):
    o = acc[:tq] / l[:tq] - lam * (acc[tq:] / l[tq:])
    return _rms_norm(o, g, SUBLN_EPS) * (1.0 - lam_init)


def _attn_kernel(lq1_ref, lk1_ref, lq2_ref, lk2_ref, g_ref, q_ref, k_ref, v_ref, km_ref, vm_ref,
                 o_ref, qq_ref, m_ref, l_ref, acc_ref, s0_ref, s1_ref, *, lam_init, tq, hp):
    nq = q_ref.shape[0] // tq
    lam = _lambda(lq1_ref, lk1_ref, lq2_ref, lk2_ref, lam_init)
    contract_last = (((1,), (1,)), ((), ()))
    contract_first = (((0,), (0,)), ((), ()))
    heads = [slice(h * V_DIM, (h + 1) * V_DIM) for h in range(hp)]

    def block_rows(b):
        return pl.ds(pl.multiple_of(b * tq, tq), tq)

    def scores(k, h):
        return lax.dot_general(k, qq_ref[h], contract_last, preferred_element_type=F32)

    def consume(h, s, v, first):
        blk_max = jnp.max(s, axis=0, keepdims=True)
        m_new = blk_max if first else jnp.maximum(m_ref[h], blk_max)
        p = jnp.exp2(s - m_new)
        pv = lax.dot_general(v, p.astype(BF16), contract_first, preferred_element_type=F32)
        psum = jnp.sum(p, axis=0, keepdims=True)
        if first:
            l_ref[h] = psum
            acc_ref[h] = pv
        else:
            alpha = jnp.exp2(m_ref[h] - m_new)
            l_ref[h] = alpha * l_ref[h] + psum
            acc_ref[h] = alpha * acc_ref[h] + pv
        m_ref[h] = m_new

    def interleave(producers, consumers, lead, per_consumer):
        producers = list(producers)
        for task in producers[:lead]:
            task()
        rest = producers[lead:]
        for c, task in enumerate(consumers):
            task()
            for extra in rest[c * per_consumer:(c + 1) * per_consumer]:
                extra()
        for extra in rest[len(consumers) * per_consumer:]:
            extra()

    def step(b, cur_ref, nxt_ref):
        def prefetch(h, hs):
            def run():
                nxt_ref[h] = scores(k_ref[block_rows(b + 1), hs], h)
            return run

        def use(h, hs):
            return lambda: consume(h, cur_ref[h], v_ref[block_rows(b), hs], False)

        producers = [prefetch(h, hs) for h, hs in enumerate(heads)] if nxt_ref is not None else []
        interleave(producers, [use(h, hs) for h, hs in enumerate(heads)], 2, 1)

    def q_tile(i, _):
        rows = block_rows(i)
        for h, hs in enumerate(heads):
            qq_ref[h] = _stack_queries(q_ref[rows, hs])

        key = lax.broadcasted_iota(jnp.int32, (N_META + tq, 2 * tq), 0) - N_META
        qry = lax.broadcasted_iota(jnp.int32, (N_META + tq, 2 * tq), 1)
        visible = key <= jnp.where(qry >= tq, qry - tq, qry)
        diag = [None] * hp

        def diag_scores(h, hs):
            def run():
                diag[h] = scores(jnp.concatenate([km_ref[:, hs], k_ref[rows, hs]], axis=0), h)
            return run

        def first_scores(h, hs):
            def run():
                s0_ref[h] = scores(k_ref[block_rows(0), hs], h)
            return run

        def use_diag(h, hs):
            return lambda: consume(h, jnp.where(visible, diag[h], jnp.finfo(F32).min),
                                   jnp.concatenate([vm_ref[:, hs], v_ref[rows, hs]], axis=0), True)

        producers = [diag_scores(h, hs) for h, hs in enumerate(heads)]
        producers = producers[:2] + [t for h, hs in enumerate(heads)
                                     for t in (producers[h + 2:h + 3] + [first_scores(h, hs)])]
        interleave(producers, [use_diag(h, hs) for h, hs in enumerate(heads)], 2, 2)

        def body(b, _):
            @pl.when(b % 2 == 0)
            def _():
                step(b, s0_ref, s1_ref)

            @pl.when(b % 2 == 1)
            def _():
                step(b, s1_ref, s0_ref)
            return 0

        lax.fori_loop(0, i - 1, body, 0)

        @pl.when((i >= 1) & (i % 2 == 1))
        def _():
            step(i - 1, s0_ref, None)

        @pl.when((i >= 1) & (i % 2 == 0))
        def _():
            step(i - 1, s1_ref, None)

        for h, hs in enumerate(heads):
            acc = acc_ref[h]
            l = l_ref[h]
            o = acc[:, :tq] / l[:, :tq] - lam * (acc[:, tq:] / l[:, tq:])
            o = o * lax.rsqrt(jnp.mean(o * o, axis=0, keepdims=True) + SUBLN_EPS)
            o = o * g_ref[...] * (1.0 - lam_init)
            o_ref[rows, hs] = o.T.astype(BF16)
        return 0

    lax.fori_loop(0, nq, q_tile, 0)


def _attn(lam_params, subln_g_col, proj, proj_meta, *, batch, lam_init, tq, hp):
    t = proj.shape[0]
    seq = t // batch
    assert seq % tq == 0 and N_HEADS % hp == 0
    groups = N_HEADS // hp
    width = hp * V_DIM
    small = lambda b, g: (0, 0)
    lam_specs = [pl.BlockSpec((1, HEAD_DIM), small)] * 4
    score_buf = pltpu.VMEM((hp, tq, 2 * tq), F32)
    est = (2 * 4 * seq * width * 2 + hp * (2 * tq * V_DIM * 2 + 2 * tq * 2 * tq * 4
                                           + 8 * tq * 2 * tq * 4 // hp + 2 * V_DIM * 2 * tq * 4))
    return pl.pallas_call(
        functools.partial(_attn_kernel, lam_init=lam_init, tq=tq, hp=hp),
        out_shape=jax.ShapeDtypeStruct((t, ATTN_WIDTH), BF16),
        grid=(batch, groups),
        in_specs=lam_specs + [
            pl.BlockSpec((V_DIM, 1), small),
            pl.BlockSpec((seq, width), lambda b, g: (b, g)),
            pl.BlockSpec((seq, width), lambda b, g: (b, groups + g)),
            pl.BlockSpec((seq, width), lambda b, g: (b, 2 * groups + g)),
            pl.BlockSpec((N_META, width), lambda b, g: (0, groups + g)),
            pl.BlockSpec((N_META, width), lambda b, g: (0, 2 * groups + g)),
        ],
        out_specs=pl.BlockSpec((seq, width), lambda b, g: (b, g)),
        scratch_shapes=[pltpu.VMEM((hp, 2 * tq, V_DIM), BF16),
                        pltpu.VMEM((hp, 1, 2 * tq), F32), pltpu.VMEM((hp, 1, 2 * tq), F32),
                        pltpu.VMEM((hp, V_DIM, 2 * tq), F32), score_buf, score_buf],
        compiler_params=pltpu.CompilerParams(
            dimension_semantics=("parallel", "parallel"),
            vmem_limit_bytes=_vmem_limit(est)),
        name="attn",
    )(*lam_params, subln_g_col, proj, proj, proj, proj_meta, proj_meta)


def _attn_meta_kernel(lq1_ref, lk1_ref, lq2_ref, lk2_ref, g_ref, q_ref, k_ref, v_ref, o_ref,
                      *, lam_init):
    qq = _stack_queries(q_ref[...])
    s = _scores(qq, k_ref[...])
    row = lax.broadcasted_iota(jnp.int32, s.shape, 0) % N_META
    col = lax.broadcasted_iota(jnp.int32, s.shape, 1)
    s = jnp.where(col <= row, s, jnp.finfo(F32).min)
    m = jnp.max(s, axis=-1, keepdims=True)
    p = jnp.exp2(s - m)
    l = jnp.sum(p, axis=-1, keepdims=True)
    acc = jnp.dot(p.astype(BF16), v_ref[...], preferred_element_type=F32)
    lam = _lambda(lq1_ref, lk1_ref, lq2_ref, lk2_ref, lam_init)
    o_ref[...] = _finish(acc, l, lam, g_ref[...], lam_init, N_META).astype(BF16)


def _attn_meta(lam_params, subln_g, proj_meta, *, lam_init):
    small = lambda h: (0, 0)
    return pl.pallas_call(
        functools.partial(_attn_meta_kernel, lam_init=lam_init),
        out_shape=jax.ShapeDtypeStruct((N_META, ATTN_WIDTH), BF16),
        grid=(N_HEADS,),
        in_specs=[pl.BlockSpec((1, HEAD_DIM), small)] * 4 + [
            pl.BlockSpec((1, V_DIM), small),
            pl.BlockSpec((N_META, V_DIM), lambda h: (0, h)),
            pl.BlockSpec((N_META, V_DIM), lambda h: (0, N_HEADS + h)),
            pl.BlockSpec((N_META, V_DIM), lambda h: (0, 2 * N_HEADS + h)),
        ],
        out_specs=pl.BlockSpec((N_META, V_DIM), lambda h: (0, h)),
        compiler_params=pltpu.CompilerParams(dimension_semantics=("arbitrary",)),
        name="attn_meta",
    )(*lam_params, subln_g, proj_meta, proj_meta, proj_meta)


def _outproj_kernel(h_ref, a_ref, d_ref, wp_ref, ps_ref, wo_ref, *rest):
    cast_src, (o_ref, *cast_dst) = rest[:len(rest) // 2], rest[len(rest) // 2:]
    _cast_blocks(cast_src, cast_dst)
    diff = d_ref[...]
    pools = []
    for g in range(len(POOL_WINDOWS)):
        cols = slice(g * POOL_GROUP_WIDTH, (g + 1) * POOL_GROUP_WIDTH)
        pools.append(jnp.dot(diff[:, cols], wp_ref[g], preferred_element_type=F32))
    pool = (jnp.concatenate(pools, axis=1) * ps_ref[...]).astype(BF16)
    mix = jnp.concatenate([a_ref[...], pool], axis=1)
    o_ref[...] = h_ref[...] + jnp.dot(mix, wo_ref[...], preferred_element_type=F32)


def _outproj(h, attn, proj, w_pool, pool_scale, w_out, cast_stacks=None, *, layer, cast_layer=0,
             tm):
    t = h.shape[0]
    assert t % tm == 0
    est = (2 * 2 * tm * D_MODEL * 4 + 2 * 2 * tm * ATTN_WIDTH * 2 + 2 * D_MODEL * D_MODEL * 2
           + 2 * POOL_WIDTH * POOL_GROUP_WIDTH * 2 + 3 * tm * D_MODEL * 4)
    in_specs = [
        pl.BlockSpec((tm, D_MODEL), lambda i: (i, 0)),
        pl.BlockSpec((tm, ATTN_WIDTH), lambda i: (i, 0)),
        pl.BlockSpec((tm, POOL_WIDTH), lambda i: (i, 3)),
        pl.BlockSpec((None, len(POOL_WINDOWS), POOL_GROUP_WIDTH, POOL_GROUP_WIDTH),
                     lambda i: (layer, 0, 0, 0)),
        pl.BlockSpec((1, POOL_WIDTH), lambda i: (0, 0)),
        pl.BlockSpec((None, D_MODEL, D_MODEL), lambda i: (layer, 0, 0)),
    ]
    args = [h, attn, proj, w_pool, pool_scale, w_out]
    out_shape = [jax.ShapeDtypeStruct((t, D_MODEL), F32)]
    out_specs = [pl.BlockSpec((tm, D_MODEL), lambda i: (i, 0))]
    if cast_stacks is not None:
        cast_in, cast_out, cast_shapes = _cast_plumbing(cast_stacks, cast_layer, t // tm,
                                                        lambda i: i)
        in_specs += cast_in
        args += list(cast_stacks)
        out_shape += cast_shapes
        out_specs += cast_out
        est += 2 * 6 * 3 * D_MODEL * D_FF // (t // tm)
    outs = pl.pallas_call(
        _outproj_kernel,
        out_shape=out_shape,
        grid=(t // tm,),
        in_specs=in_specs,
        out_specs=out_specs,
        compiler_params=pltpu.CompilerParams(
            dimension_semantics=("arbitrary",),
            vmem_limit_bytes=_vmem_limit(est)),
        name="outproj",
    )(*args)
    return outs[0] if cast_stacks is None else outs


def _rope_tables(length):
    pos = jnp.arange(length, dtype=F32)
    inv_freq = 1.0 / (ROPE_THETA ** (jnp.arange(0, HEAD_DIM, 2, dtype=F32) / HEAD_DIM))
    ang = pos[:, None] * inv_freq[None, :]
    ang = jnp.concatenate([ang, ang, ang, ang], axis=-1)
    sign = jnp.where((jnp.arange(LANES) % HEAD_DIM) < HEAD_DIM // 2, -1.0, 1.0).astype(F32)
    cos, sin = jnp.cos(ang), jnp.sin(ang) * sign
    scale = HEAD_DIM ** -0.5 * math.log2(math.e)
    return cos * scale, sin * scale, cos, sin


def kernel(x, meta_tokens, ffn1_norm_g, ffn1_w_gate, ffn1_w_up, ffn1_w_down, mix_norm_g, w_in,
           lam_q1, lam_k1, lam_q2, lam_k2, subln_g, w_pool, pool_scale, w_out,
           ffn2_norm_g, ffn2_w_gate, ffn2_w_up, ffn2_w_down, final_norm_g):
    batch, seq, d = x.shape
    depth = w_in.shape[0]
    assert d == D_MODEL and meta_tokens.shape == (N_META, D_MODEL)

    tables = _rope_tables(N_META + seq)
    tab_meta = tuple(tb[:N_META] for tb in tables)
    tab_real = tuple(tb[N_META:] for tb in tables)
    final_g = final_norm_g.reshape(1, D_MODEL)

    h = x.reshape(batch * seq, D_MODEL)
    hm = meta_tokens.astype(x.dtype)

    wi = w_in.astype(BF16)
    wo = w_out.astype(BF16)
    wp = w_pool.astype(BF16)
    ffn1_stacks = (ffn1_w_gate, ffn1_w_up, ffn1_w_down)
    ffn2_stacks = (ffn2_w_gate, ffn2_w_up, ffn2_w_down)
    ffn1_w = _cast_ffn_weights(ffn1_stacks, 0)

    for layer in range(depth):
        lam_init = 0.8 - 0.6 * math.exp(-0.3 * layer)
        last = layer == depth - 1
        g1 = ffn1_norm_g[layer].reshape(1, D_MODEL)
        gm = mix_norm_g[layer].reshape(1, D_MODEL)
        g2 = ffn2_norm_g[layer].reshape(1, D_MODEL)
        ps = pool_scale[layer].reshape(1, POOL_WIDTH)
        sg = subln_g[layer].reshape(1, V_DIM)
        lam_params = tuple(p[layer].reshape(1, HEAD_DIM) for p in (lam_q1, lam_k1, lam_q2, lam_k2))

        hm = _ffn(hm, g1, *ffn1_w, final_g, layer=0, tm=N_META, tf=512, apply_final=False)
        proj_m, u_m = _inproj(hm, gm, wi, tab_meta, None, layer=layer, batch=1, tl=N_META, meta=True)
        attn_m = _attn_meta(lam_params, sg, proj_m, lam_init=lam_init)
        h = _ffn(h, g1, *ffn1_w, final_g, layer=0, tm=1024, tf=512, apply_final=False)
        proj, *ffn2_w = _inproj(h, gm, wi, tab_real, u_m, ffn2_stacks, layer=layer, batch=batch,
                                tl=512, meta=False)
        attn = _attn(lam_params, sg.reshape(V_DIM, 1), proj, proj_m, batch=batch,
                     lam_init=lam_init, tq=256, hp=8)

        if last:
            h = _outproj(h, attn, proj, wp, ps, wo, layer=layer, tm=512)
        else:
            h, *ffn1_w = _outproj(h, attn, proj, wp, ps, wo, ffn1_stacks, layer=layer,
                                  cast_layer=layer + 1, tm=512)
            hm = _outproj(hm, attn_m, proj_m, wp, ps, wo, layer=layer, tm=N_META)
            hm = _ffn(hm, g2, *ffn2_w, final_g, layer=0, tm=N_META, tf=512, apply_final=False)
        h = _ffn(h, g2, *ffn2_w, final_g, layer=0, tm=1024, tf=512, apply_final=last)

    return h.reshape(batch, seq, D_MODEL)
```

```python
import functools
import math

import jax
import jax.numpy as jnp
from jax import lax
from jax.experimental import pallas as pl
from jax.experimental.pallas import tpu as pltpu

F32 = jnp.float32
BF16 = jnp.bfloat16

D_MODEL = 2048
N_META = 16
ATTN_WIDTH = 1024
POOL_WIDTH = 1024
HEAD_DIM = 64
V_DIM = 2 * HEAD_DIM
N_HEADS = ATTN_WIDTH // V_DIM
POOL_WINDOWS = (2, 4, 8, 16)
POOL_GROUP_WIDTH = POOL_WIDTH // len(POOL_WINDOWS)
IN_WIDTH = 3 * ATTN_WIDTH + POOL_WIDTH
D_FF = 5632
ROPE_THETA = 10000.0
NORM_EPS = 1e-6
SUBLN_EPS = 1e-5

LANES = 128
V7X_VMEM_BYTES = 64 * 1024 * 1024


def _vmem_limit(estimate_bytes):
    return int(min(estimate_bytes * 5 // 4 + (4 << 20), V7X_VMEM_BYTES * 7 // 8))


def _rms_norm(x, g, eps):
    return x * lax.rsqrt(jnp.mean(x * x, axis=-1, keepdims=True) + eps) * g


def _ffn_kernel(x_ref, g_ref, wg_ref, wu_ref, wd_ref, fg_ref, o_ref, xn_ref, *, apply_final):
    f = pl.program_id(1)

    @pl.when(f == 0)
    def _():
        x = x_ref[...]
        xn_ref[...] = _rms_norm(x, g_ref[...], NORM_EPS).astype(BF16)
        o_ref[...] = x

    xn = xn_ref[...]
    gate = jnp.dot(xn, wg_ref[...], preferred_element_type=F32)
    up = jnp.dot(xn, wu_ref[...], preferred_element_type=F32)
    act = (gate * jax.nn.sigmoid(gate)) * (0.5 * up)
    o_ref[...] += jnp.dot(act.astype(BF16), wd_ref[...], preferred_element_type=F32)

    if apply_final:
        @pl.when(f == pl.num_programs(1) - 1)
        def _():
            o_ref[...] = _rms_norm(o_ref[...], fg_ref[...], NORM_EPS)


def _ffn(x, g, wg, wu, wd, final_g, *, layer, tm, tf, apply_final):
    t = x.shape[0]
    assert t % tm == 0 and D_FF % tf == 0
    est = (2 * 2 * tm * D_MODEL * 4 + tm * D_MODEL * 2 + 2 * 3 * D_MODEL * tf * 2
           + 3 * tm * tf * 4)
    return pl.pallas_call(
        functools.partial(_ffn_kernel, apply_final=apply_final),
        out_shape=jax.ShapeDtypeStruct((t, D_MODEL), F32),
        grid=(t // tm, D_FF // tf),
        in_specs=[
            pl.BlockSpec((tm, D_MODEL), lambda i, f: (i, 0)),
            pl.BlockSpec((1, D_MODEL), lambda i, f: (0, 0)),
            pl.BlockSpec((None, D_MODEL, tf), lambda i, f: (layer, 0, f)),
            pl.BlockSpec((None, D_MODEL, tf), lambda i, f: (layer, 0, f)),
            pl.BlockSpec((None, tf, D_MODEL), lambda i, f: (layer, f, 0)),
            pl.BlockSpec((1, D_MODEL), lambda i, f: (0, 0)),
        ],
        out_specs=pl.BlockSpec((tm, D_MODEL), lambda i, f: (i, 0)),
        scratch_shapes=[pltpu.VMEM((tm, D_MODEL), BF16)],
        compiler_params=pltpu.CompilerParams(
            dimension_semantics=("parallel", "arbitrary"),
            vmem_limit_bytes=_vmem_limit(est)),
        name="ffn",
    )(x, g, wg, wu, wd, final_g)


def _cast_plumbing(stacks, layer, n_steps, step_of):
    rows_gu = D_MODEL // n_steps
    rows_d = 2 * D_FF // n_steps
    assert D_MODEL % n_steps == 0 and (2 * D_FF) % n_steps == 0 and rows_d % 16 == 0

    def specs(lead):
        gu = pl.BlockSpec((None, rows_gu, D_FF), lambda *g: (lead, step_of(*g), 0))
        dn = pl.BlockSpec((None, rows_d, D_MODEL), lambda *g: (lead, step_of(*g) // 2, 0))
        return [gu, gu, dn]

    out_shapes = [jax.ShapeDtypeStruct((1,) + w.shape[1:], BF16) for w in stacks]
    return specs(layer), specs(0), out_shapes


def _cast_blocks(src_refs, dst_refs):
    for src, dst in zip(src_refs, dst_refs):
        dst[...] = src[...].astype(BF16)


def _cast_kernel(*refs):
    _cast_blocks(refs[:len(refs) // 2], refs[len(refs) // 2:])


def _cast_ffn_weights(stacks, layer, n_steps=32):
    in_specs, out_specs, out_shapes = _cast_plumbing(stacks, layer, n_steps, lambda i: i)
    est = 2 * 6 * 3 * D_MODEL * D_FF // n_steps
    return pl.pallas_call(
        _cast_kernel,
        out_shape=out_shapes,
        grid=(n_steps,),
        in_specs=in_specs,
        out_specs=out_specs,
        compiler_params=pltpu.CompilerParams(
            dimension_semantics=("arbitrary",),
            vmem_limit_bytes=_vmem_limit(est)),
        name="cast_ffn_weights",
    )(*stacks)


def _rope(x, cos, sin_signed):
    lane = lax.broadcasted_iota(jnp.int32, (x.shape[0], LANES), 1)
    upper = (lane % HEAD_DIM) >= (HEAD_DIM // 2)
    outs = []
    for c in range(x.shape[1] // LANES):
        xc = x[:, c * LANES:(c + 1) * LANES]
        from_below = pltpu.roll(xc, HEAD_DIM // 2, axis=1)
        from_above = pltpu.roll(xc, LANES - HEAD_DIM // 2, axis=1)
        outs.append(xc * cos + jnp.where(upper, from_below, from_above) * sin_signed)
    return jnp.concatenate(outs, axis=1)


def _pool_diff(hist, cur, inv_count_fn):
    run = jnp.concatenate([hist, cur], axis=0)
    outs = []
    for g, w in enumerate(POOL_WINDOWS):
        run = run + pltpu.roll(run, w // 2, axis=0)
        cols = slice(g * POOL_GROUP_WIDTH, (g + 1) * POOL_GROUP_WIDTH)
        outs.append(run[N_META:, :POOL_GROUP_WIDTH] * inv_count_fn(w) - cur[:, cols])
        run = run[:, POOL_GROUP_WIDTH:]
    return jnp.concatenate(outs, axis=1)


def _inproj_kernel(*refs, meta):
    if meta:
        (x_ref, g_ref, w_ref, cq_ref, sq_ref, ck_ref, sk_ref,
         o_ref, u_ref, hist_ref) = refs
    else:
        (x_ref, g_ref, w_ref, cq_ref, sq_ref, ck_ref, sk_ref, um_ref, *cast_src) = refs[:-5]
        o_ref, *cast_dst, hist_ref = refs[-5:]
        _cast_blocks(cast_src, cast_dst)
    j = pl.program_id(1)
    rows = x_ref.shape[0]
    chunk = lambda n: slice(n * ATTN_WIDTH, (n + 1) * ATTN_WIDTH)

    if not meta:
        @pl.when(j == 0)
        def _():
            hist_ref[...] = um_ref[...]

    xn = _rms_norm(x_ref[...], g_ref[...], NORM_EPS).astype(BF16)

    def project(n):
        return jnp.dot(xn, w_ref[:, chunk(n)], preferred_element_type=F32)

    o_ref[:, chunk(0)] = _rope(project(0), cq_ref[...], sq_ref[...]).astype(BF16)
    o_ref[:, chunk(1)] = _rope(project(1), ck_ref[...], sk_ref[...]).astype(BF16)
    o_ref[:, chunk(2)] = project(2).astype(BF16)

    acc = project(3)
    if meta:
        u_ref[...] = acc
        hist = jnp.zeros((N_META, POOL_WIDTH), F32)
        pos = lax.broadcasted_iota(jnp.int32, (rows, 1), 0)

        def inv_count(w):
            return 1.0 / jnp.minimum(pos + 1, w).astype(F32)
    else:
        hist = hist_ref[...]

        def inv_count(w):
            return 1.0 / w
    o_ref[:, chunk(3)] = _pool_diff(hist, acc, inv_count).astype(BF16)
    hist_ref[...] = acc[rows - N_META:, :]


def _inproj(x, g, w_in, tables, u_meta, cast_stacks=None, *, layer, batch, tl, meta):
    t = x.shape[0]
    seq = t // batch
    nj = seq // tl
    assert seq % tl == 0
    cq, sq, ck, sk = tables
    tok = lambda b, j: (b * nj + j, 0)
    tab = lambda b, j: (j, 0)
    const = lambda b, j: (0, 0)
    in_specs = [
        pl.BlockSpec((tl, D_MODEL), tok),
        pl.BlockSpec((1, D_MODEL), const),
        pl.BlockSpec((None, D_MODEL, IN_WIDTH), lambda b, j: (layer, 0, 0),
                     pipeline_mode=pl.Buffered(1)),
        pl.BlockSpec((tl, LANES), tab),
        pl.BlockSpec((tl, LANES), tab),
        pl.BlockSpec((tl, LANES), tab),
        pl.BlockSpec((tl, LANES), tab),
    ]
    args = [x, g, w_in, cq, sq, ck, sk]
    proj_shape = jax.ShapeDtypeStruct((t, IN_WIDTH), BF16)
    proj_spec = pl.BlockSpec((tl, IN_WIDTH), tok)
    if meta:
        out_shape = (proj_shape, jax.ShapeDtypeStruct((t, POOL_WIDTH), F32))
        out_specs = (proj_spec, pl.BlockSpec((tl, POOL_WIDTH), const))
    else:
        cast_in, cast_out, cast_shapes = _cast_plumbing(cast_stacks, layer, batch * nj,
                                                        lambda b, j: b * nj + j)
        in_specs += [pl.BlockSpec((N_META, POOL_WIDTH), const)] + cast_in
        args += [u_meta] + list(cast_stacks)
        out_shape = [proj_shape] + cast_shapes
        out_specs = [proj_spec] + cast_out
    est = (2 * tl * D_MODEL * 4 + D_MODEL * IN_WIDTH * 2 + 2 * tl * IN_WIDTH * 2
           + tl * D_MODEL * 2 + 4 * (tl + N_META) * POOL_WIDTH * 4 + 8 * tl * LANES * 4
           + (0 if meta else 2 * 6 * 3 * D_MODEL * D_FF // (batch * nj)))
    return pl.pallas_call(
        functools.partial(_inproj_kernel, meta=meta),
        out_shape=out_shape,
        grid=(batch, nj),
        in_specs=in_specs,
        out_specs=out_specs,
        scratch_shapes=[pltpu.VMEM((N_META, POOL_WIDTH), F32)],
        compiler_params=pltpu.CompilerParams(
            dimension_semantics=("arbitrary", "arbitrary"),
            vmem_limit_bytes=_vmem_limit(est)),
        name="inproj_meta" if meta else "inproj",
    )(*args)


def _stack_queries(q):
    lane = lax.broadcasted_iota(jnp.int32, q.shape, 1)
    zero = jnp.zeros_like(q)
    return jnp.concatenate([jnp.where(lane < HEAD_DIM, q, zero),
                            jnp.where(lane >= HEAD_DIM, q, zero)], axis=0)


def _scores(qq, k):
    return lax.dot_general(qq, k, (((1,), (1,)), ((), ())), preferred_element_type=F32)


def _lambda(lq1_ref, lk1_ref, lq2_ref, lk2_ref, lam_init):
    s1 = jnp.sum(lq1_ref[...] * lk1_ref[...], axis=-1, keepdims=True)
    s2 = jnp.sum(lq2_ref[...] * lk2_ref[...], axis=-1, keepdims=True)
    return jnp.exp(s1) - jnp.exp(s2) + lam_init


def _finish(acc, l, lam, g, lam_init, tq):
    o = acc[:tq] / l[:tq] - lam * (acc[tq:] / l[tq:])
    return _rms_norm(o, g, SUBLN_EPS) * (1.0 - lam_init)


def _attn_kernel(lq1_ref, lk1_ref, lq2_ref, lk2_ref, g_ref, q_ref, k_ref, v_ref, km_ref, vmt_ref,
                 o_ref, qqt_ref, vt_ref, m_ref, l_ref, acc_ref, s0_ref, s1_ref,
                 *, lam_init, tq, hp):
    nq = q_ref.shape[0] // tq
    lam = _lambda(lq1_ref, lk1_ref, lq2_ref, lk2_ref, lam_init)
    heads = [slice(h * V_DIM, (h + 1) * V_DIM) for h in range(hp)]

    def block_rows(b):
        return pl.ds(pl.multiple_of(b * tq, tq), tq)

    for h, hs in enumerate(heads):
        for c in range(nq):
            vt_ref[h, c] = v_ref[c * tq:(c + 1) * tq, hs].T

    def scores(k, h):
        return jnp.dot(k, qqt_ref[h], preferred_element_type=F32)

    def consume(h, s, vt, first):
        blk_max = jnp.max(s, axis=0, keepdims=True)
        m_new = blk_max if first else jnp.maximum(m_ref[h], blk_max)
        p = jnp.exp2(s - m_new)
        pv = jnp.dot(vt, p.astype(BF16), preferred_element_type=F32)
        psum = jnp.sum(p, axis=0, keepdims=True)
        if first:
            l_ref[h] = psum
            acc_ref[h] = pv
        else:
            alpha = jnp.exp2(m_ref[h] - m_new)
            l_ref[h] = alpha * l_ref[h] + psum
            acc_ref[h] = alpha * acc_ref[h] + pv
        m_ref[h] = m_new

    def interleave(producers, consumers, lead, per_consumer):
        producers = list(producers)
        for task in producers[:lead]:
            task()
        rest = producers[lead:]
        for c, task in enumerate(consumers):
            task()
            for extra in rest[c * per_consumer:(c + 1) * per_consumer]:
                extra()
        for extra in rest[len(consumers) * per_consumer:]:
            extra()

    def step(b, cur_ref, nxt_ref):
        def prefetch(h, hs):
            def run():
                nxt_ref[h] = scores(k_ref[block_rows(b + 1), hs], h)
            return run

        def use(h):
            return lambda: consume(h, cur_ref[h], vt_ref[h, b], False)

        producers = [prefetch(h, hs) for h, hs in enumerate(heads)] if nxt_ref is not None else []
        interleave(producers, [use(h) for h in range(hp)], 2, 1)

    def q_tile(i, _):
        rows = block_rows(i)
        for h, hs in enumerate(heads):
            qqt_ref[h] = _stack_queries(q_ref[rows, hs]).T

        key = lax.broadcasted_iota(jnp.int32, (tq + N_META, 2 * tq), 0)
        qry = lax.broadcasted_iota(jnp.int32, (tq + N_META, 2 * tq), 1)
        visible = (key >= tq) | (key <= jnp.where(qry >= tq, qry - tq, qry))
        diag = [None] * hp

        def diag_scores(h, hs):
            def run():
                diag[h] = scores(jnp.concatenate([k_ref[rows, hs], km_ref[:, hs]], axis=0), h)
            return run

        def first_scores(h, hs):
            def run():
                s0_ref[h] = scores(k_ref[block_rows(0), hs], h)
            return run

        def use_diag(h, hs):
            return lambda: consume(h, jnp.where(visible, diag[h], jnp.finfo(F32).min),
                                   jnp.concatenate([vt_ref[h, i], vmt_ref[hs, :]], axis=1), True)

        producers = [diag_scores(h, hs) for h, hs in enumerate(heads)]
        producers = producers[:2] + [t for h, hs in enumerate(heads)
                                     for t in (producers[h + 2:h + 3] + [first_scores(h, hs)])]
        interleave(producers, [use_diag(h, hs) for h, hs in enumerate(heads)], 2, 2)

        def body(b, _):
            @pl.when(b % 2 == 0)
            def _():
                step(b, s0_ref, s1_ref)

            @pl.when(b % 2 == 1)
            def _():
                step(b, s1_ref, s0_ref)
            return 0

        lax.fori_loop(0, i - 1, body, 0)

        @pl.when((i >= 1) & (i % 2 == 1))
        def _():
            step(i - 1, s0_ref, None)

        @pl.when((i >= 1) & (i % 2 == 0))
        def _():
            step(i - 1, s1_ref, None)

        for h, hs in enumerate(heads):
            acc = acc_ref[h]
            l = l_ref[h]
            o = acc[:, :tq] / l[:, :tq] - lam * (acc[:, tq:] / l[:, tq:])
            o = o * lax.rsqrt(jnp.mean(o * o, axis=0, keepdims=True) + SUBLN_EPS)
            o = o * g_ref[...] * (1.0 - lam_init)
            o_ref[rows, hs] = o.T.astype(BF16)
        return 0

    lax.fori_loop(0, nq, q_tile, 0)


def _attn(lam_params, subln_g_col, proj, proj_meta, *, batch, lam_init, tq, hp):
    t = proj.shape[0]
    seq = t // batch
    assert seq % tq == 0 and N_HEADS % hp == 0
    groups = N_HEADS // hp
    width = hp * V_DIM
    small = lambda b, g: (0, 0)
    lam_specs = [pl.BlockSpec((1, HEAD_DIM), small)] * 4
    score_buf = pltpu.VMEM((hp, tq, 2 * tq), F32)
    v_meta_t = proj_meta[:, 2 * ATTN_WIDTH:3 * ATTN_WIDTH].T
    est = (2 * 4 * seq * width * 2 + seq * width * 2
           + hp * (2 * tq * V_DIM * 2 + 2 * tq * 2 * tq * 4 + 8 * tq * 2 * tq * 4 // hp
                   + 2 * V_DIM * 2 * tq * 4))
    return pl.pallas_call(
        functools.partial(_attn_kernel, lam_init=lam_init, tq=tq, hp=hp),
        out_shape=jax.ShapeDtypeStruct((t, ATTN_WIDTH), BF16),
        grid=(batch, groups),
        in_specs=lam_specs + [
            pl.BlockSpec((V_DIM, 1), small),
            pl.BlockSpec((seq, width), lambda b, g: (b, g)),
            pl.BlockSpec((seq, width), lambda b, g: (b, groups + g)),
            pl.BlockSpec((seq, width), lambda b, g: (b, 2 * groups + g)),
            pl.BlockSpec((N_META, width), lambda b, g: (0, groups + g)),
            pl.BlockSpec((width, N_META), lambda b, g: (g, 0)),
        ],
        out_specs=pl.BlockSpec((seq, width), lambda b, g: (b, g)),
        scratch_shapes=[pltpu.VMEM((hp, V_DIM, 2 * tq), BF16),
                        pltpu.VMEM((hp, seq // tq, V_DIM, tq), BF16),
                        pltpu.VMEM((hp, 1, 2 * tq), F32), pltpu.VMEM((hp, 1, 2 * tq), F32),
                        pltpu.VMEM((hp, V_DIM, 2 * tq), F32), score_buf, score_buf],
        compiler_params=pltpu.CompilerParams(
            dimension_semantics=("parallel", "parallel"),
            vmem_limit_bytes=_vmem_limit(est)),
        name="attn",
    )(*lam_params, subln_g_col, proj, proj, proj, proj_meta, v_meta_t)


def _attn_meta_kernel(lq1_ref, lk1_ref, lq2_ref, lk2_ref, g_ref, q_ref, k_ref, v_ref, o_ref,
                      *, lam_init):
    qq = _stack_queries(q_ref[...])
    s = _scores(qq, k_ref[...])
    row = lax.broadcasted_iota(jnp.int32, s.shape, 0) % N_META
    col = lax.broadcasted_iota(jnp.int32, s.shape, 1)
    s = jnp.where(col <= row, s, jnp.finfo(F32).min)
    m = jnp.max(s, axis=-1, keepdims=True)
    p = jnp.exp2(s - m)
    l = jnp.sum(p, axis=-1, keepdims=True)
    acc = jnp.dot(p.astype(BF16), v_ref[...], preferred_element_type=F32)
    lam = _lambda(lq1_ref, lk1_ref, lq2_ref, lk2_ref, lam_init)
    o_ref[...] = _finish(acc, l, lam, g_ref[...], lam_init, N_META).astype(BF16)


def _attn_meta(lam_params, subln_g, proj_meta, *, lam_init):
    small = lambda h: (0, 0)
    return pl.pallas_call(
        functools.partial(_attn_meta_kernel, lam_init=lam_init),
        out_shape=jax.ShapeDtypeStruct((N_META, ATTN_WIDTH), BF16),
        grid=(N_HEADS,),
        in_specs=[pl.BlockSpec((1, HEAD_DIM), small)] * 4 + [
            pl.BlockSpec((1, V_DIM), small),
            pl.BlockSpec((N_META, V_DIM), lambda h: (0, h)),
            pl.BlockSpec((N_META, V_DIM), lambda h: (0, N_HEADS + h)),
            pl.BlockSpec((N_META, V_DIM), lambda h: (0, 2 * N_HEADS + h)),
        ],
        out_specs=pl.BlockSpec((N_META, V_DIM), lambda h: (0, h)),
        compiler_params=pltpu.CompilerParams(dimension_semantics=("arbitrary",)),
        name="attn_meta",
    )(*lam_params, subln_g, proj_meta, proj_meta, proj_meta)


def _outproj_kernel(h_ref, a_ref, d_ref, wp_ref, ps_ref, wo_ref, *rest):
    cast_src, (o_ref, *cast_dst) = rest[:len(rest) // 2], rest[len(rest) // 2:]
    _cast_blocks(cast_src, cast_dst)
    diff = d_ref[...]
    pools = []
    for g in range(len(POOL_WINDOWS)):
        cols = slice(g * POOL_GROUP_WIDTH, (g + 1) * POOL_GROUP_WIDTH)
        pools.append(jnp.dot(diff[:, cols], wp_ref[g], preferred_element_type=F32))
    pool = (jnp.concatenate(pools, axis=1) * ps_ref[...]).astype(BF16)
    mix = jnp.concatenate([a_ref[...], pool], axis=1)
    o_ref[...] = h_ref[...] + jnp.dot(mix, wo_ref[...], preferred_element_type=F32)


def _outproj(h, attn, proj, w_pool, pool_scale, w_out, cast_stacks=None, *, layer, cast_layer=0,
             tm):
    t = h.shape[0]
    assert t % tm == 0
    est = (2 * 2 * tm * D_MODEL * 4 + 2 * 2 * tm * ATTN_WIDTH * 2 + 2 * D_MODEL * D_MODEL * 2
           + 2 * POOL_WIDTH * POOL_GROUP_WIDTH * 2 + 3 * tm * D_MODEL * 4)
    in_specs = [
        pl.BlockSpec((tm, D_MODEL), lambda i: (i, 0)),
        pl.BlockSpec((tm, ATTN_WIDTH), lambda i: (i, 0)),
        pl.BlockSpec((tm, POOL_WIDTH), lambda i: (i, 3)),
        pl.BlockSpec((None, len(POOL_WINDOWS), POOL_GROUP_WIDTH, POOL_GROUP_WIDTH),
                     lambda i: (layer, 0, 0, 0)),
        pl.BlockSpec((1, POOL_WIDTH), lambda i: (0, 0)),
        pl.BlockSpec((None, D_MODEL, D_MODEL), lambda i: (layer, 0, 0)),
    ]
    args = [h, attn, proj, w_pool, pool_scale, w_out]
    out_shape = [jax.ShapeDtypeStruct((t, D_MODEL), F32)]
    out_specs = [pl.BlockSpec((tm, D_MODEL), lambda i: (i, 0))]
    if cast_stacks is not None:
        cast_in, cast_out, cast_shapes = _cast_plumbing(cast_stacks, cast_layer, t // tm,
                                                        lambda i: i)
        in_specs += cast_in
        args += list(cast_stacks)
        out_shape += cast_shapes
        out_specs += cast_out
        est += 2 * 6 * 3 * D_MODEL * D_FF // (t // tm)
    outs = pl.pallas_call(
        _outproj_kernel,
        out_shape=out_shape,
        grid=(t // tm,),
        in_specs=in_specs,
        out_specs=out_specs,
        compiler_params=pltpu.CompilerParams(
            dimension_semantics=("arbitrary",),
            vmem_limit_bytes=_vmem_limit(est)),
        name="outproj",
    )(*args)
    return outs[0] if cast_stacks is None else outs


def _rope_tables(length):
    pos = jnp.arange(length, dtype=F32)
    inv_freq = 1.0 / (ROPE_THETA ** (jnp.arange(0, HEAD_DIM, 2, dtype=F32) / HEAD_DIM))
    ang = pos[:, None] * inv_freq[None, :]
    ang = jnp.concatenate([ang, ang, ang, ang], axis=-1)
    sign = jnp.where((jnp.arange(LANES) % HEAD_DIM) < HEAD_DIM // 2, -1.0, 1.0).astype(F32)
    cos, sin = jnp.cos(ang), jnp.sin(ang) * sign
    scale = HEAD_DIM ** -0.5 * math.log2(math.e)
    return cos * scale, sin * scale, cos, sin


def kernel(x, meta_tokens, ffn1_norm_g, ffn1_w_gate, ffn1_w_up, ffn1_w_down, mix_norm_g, w_in,
           lam_q1, lam_k1, lam_q2, lam_k2, subln_g, w_pool, pool_scale, w_out,
           ffn2_norm_g, ffn2_w_gate, ffn2_w_up, ffn2_w_down, final_norm_g):
    batch, seq, d = x.shape
    depth = w_in.shape[0]
    assert d == D_MODEL and meta_tokens.shape == (N_META, D_MODEL)

    tables = _rope_tables(N_META + seq)
    tab_meta = tuple(tb[:N_META] for tb in tables)
    tab_real = tuple(tb[N_META:] for tb in tables)
    final_g = final_norm_g.reshape(1, D_MODEL)

    h = x.reshape(batch * seq, D_MODEL)
    hm = meta_tokens.astype(x.dtype)

    wi = w_in.astype(BF16)
    wo = w_out.astype(BF16)
    wp = w_pool.astype(BF16)
    ffn1_stacks = (ffn1_w_gate, ffn1_w_up, ffn1_w_down)
    ffn2_stacks = (ffn2_w_gate, ffn2_w_up, ffn2_w_down)
    ffn1_w = _cast_ffn_weights(ffn1_stacks, 0)

    for layer in range(depth):
        lam_init = 0.8 - 0.6 * math.exp(-0.3 * layer)
        last = layer == depth - 1
        g1 = ffn1_norm_g[layer].reshape(1, D_MODEL)
        gm = mix_norm_g[layer].reshape(1, D_MODEL)
        g2 = ffn2_norm_g[layer].reshape(1, D_MODEL)
        ps = pool_scale[layer].reshape(1, POOL_WIDTH)
        sg = subln_g[layer].reshape(1, V_DIM)
        lam_params = tuple(p[layer].reshape(1, HEAD_DIM) for p in (lam_q1, lam_k1, lam_q2, lam_k2))

        hm = _ffn(hm, g1, *ffn1_w, final_g, layer=0, tm=N_META, tf=512, apply_final=False)
        proj_m, u_m = _inproj(hm, gm, wi, tab_meta, None, layer=layer, batch=1, tl=N_META, meta=True)
        attn_m = _attn_meta(lam_params, sg, proj_m, lam_init=lam_init)
        h = _ffn(h, g1, *ffn1_w, final_g, layer=0, tm=1024, tf=512, apply_final=False)
        proj, *ffn2_w = _inproj(h, gm, wi, tab_real, u_m, ffn2_stacks, layer=layer, batch=batch,
                                tl=512, meta=False)
        attn = _attn(lam_params, sg.reshape(V_DIM, 1), proj, proj_m, batch=batch,
                     lam_init=lam_init, tq=256, hp=8)

        if last:
            h = _outproj(h, attn, proj, wp, ps, wo, layer=layer, tm=512)
        else:
            h, *ffn1_w = _outproj(h, attn, proj, wp, ps, wo, ffn1_stacks, layer=layer,
                                  cast_layer=layer + 1, tm=512)
            hm = _outproj(hm, attn_m, proj_m, wp, ps, wo, layer=layer, tm=N_META)
            hm = _ffn(hm, g2, *ffn2_w, final_g, layer=0, tm=N_META, tf=512, apply_final=False)
        h = _ffn(h, g2, *ffn2_w, final_g, layer=0, tm=1024, tf=512, apply_final=last)

    return h.reshape(batch, seq, D_MODEL)
```

```python
import functools
import math

import jax
import jax.numpy as jnp
from jax import lax
from jax.experimental import pallas as pl
from jax.experimental.pallas import tpu as pltpu

F32 = jnp.float32
BF16 = jnp.bfloat16

D_MODEL = 2048
N_META = 16
ATTN_WIDTH = 1024
POOL_WIDTH = 1024
HEAD_DIM = 64
V_DIM = 2 * HEAD_DIM
N_HEADS = ATTN_WIDTH // V_DIM
POOL_WINDOWS = (2, 4, 8, 16)
POOL_GROUP_WIDTH = POOL_WIDTH // len(POOL_WINDOWS)
IN_WIDTH = 3 * ATTN_WIDTH + POOL_WIDTH
D_FF = 5632
ROPE_THETA = 10000.0
NORM_EPS = 1e-6
SUBLN_EPS = 1e-5

LANES = 128
ONES_ROWS = 16
V7X_VMEM_BYTES = 64 * 1024 * 1024


def _vmem_limit(estimate_bytes):
    return int(min(estimate_bytes * 5 // 4 + (4 << 20), V7X_VMEM_BYTES * 7 // 8))


def _rms_norm(x, g, eps):
    return x * lax.rsqrt(jnp.mean(x * x, axis=-1, keepdims=True) + eps) * g


def _ffn_kernel(x_ref, g_ref, wg_ref, wu_ref, wd_ref, fg_ref, o_ref, xn_ref, *, apply_final):
    f = pl.program_id(1)

    @pl.when(f == 0)
    def _():
        x = x_ref[...]
        xn_ref[...] = _rms_norm(x, g_ref[...], NORM_EPS).astype(BF16)
        o_ref[...] = x

    xn = xn_ref[...]
    gate = jnp.dot(xn, wg_ref[...], preferred_element_type=F32)
    up = jnp.dot(xn, wu_ref[...], preferred_element_type=F32)
    act = (gate * jax.nn.sigmoid(gate)) * (0.5 * up)
    o_ref[...] += jnp.dot(act.astype(BF16), wd_ref[...], preferred_element_type=F32)

    if apply_final:
        @pl.when(f == pl.num_programs(1) - 1)
        def _():
            o_ref[...] = _rms_norm(o_ref[...], fg_ref[...], NORM_EPS)


def _ffn(x, g, wg, wu, wd, final_g, *, layer, tm, tf, apply_final):
    t = x.shape[0]
    assert t % tm == 0 and D_FF % tf == 0
    est = (2 * 2 * tm * D_MODEL * 4 + tm * D_MODEL * 2 + 2 * 3 * D_MODEL * tf * 2
           + 3 * tm * tf * 4)
    return pl.pallas_call(
        functools.partial(_ffn_kernel, apply_final=apply_final),
        out_shape=jax.ShapeDtypeStruct((t, D_MODEL), F32),
        grid=(t // tm, D_FF // tf),
        in_specs=[
            pl.BlockSpec((tm, D_MODEL), lambda i, f: (i, 0)),
            pl.BlockSpec((1, D_MODEL), lambda i, f: (0, 0)),
            pl.BlockSpec((None, D_MODEL, tf), lambda i, f: (layer, 0, f)),
            pl.BlockSpec((None, D_MODEL, tf), lambda i, f: (layer, 0, f)),
            pl.BlockSpec((None, tf, D_MODEL), lambda i, f: (layer, f, 0)),
            pl.BlockSpec((1, D_MODEL), lambda i, f: (0, 0)),
        ],
        out_specs=pl.BlockSpec((tm, D_MODEL), lambda i, f: (i, 0)),
        scratch_shapes=[pltpu.VMEM((tm, D_MODEL), BF16)],
        compiler_params=pltpu.CompilerParams(
            dimension_semantics=("parallel", "arbitrary"),
            vmem_limit_bytes=_vmem_limit(est)),
        name="ffn",
    )(x, g, wg, wu, wd, final_g)


def _cast_plumbing(stacks, layer, n_steps, step_of):
    rows_gu = D_MODEL // n_steps
    rows_d = 2 * D_FF // n_steps
    assert D_MODEL % n_steps == 0 and (2 * D_FF) % n_steps == 0 and rows_d % 16 == 0

    def specs(lead):
        gu = pl.BlockSpec((None, rows_gu, D_FF), lambda *g: (lead, step_of(*g), 0))
        dn = pl.BlockSpec((None, rows_d, D_MODEL), lambda *g: (lead, step_of(*g) // 2, 0))
        return [gu, gu, dn]

    out_shapes = [jax.ShapeDtypeStruct((1,) + w.shape[1:], BF16) for w in stacks]
    return specs(layer), specs(0), out_shapes


def _cast_blocks(src_refs, dst_refs):
    for src, dst in zip(src_refs, dst_refs):
        dst[...] = src[...].astype(BF16)


def _cast_kernel(*refs):
    _cast_blocks(refs[:len(refs) // 2], refs[len(refs) // 2:])


def _cast_ffn_weights(stacks, layer, n_steps=32):
    in_specs, out_specs, out_shapes = _cast_plumbing(stacks, layer, n_steps, lambda i: i)
    est = 2 * 6 * 3 * D_MODEL * D_FF // n_steps
    return pl.pallas_call(
        _cast_kernel,
        out_shape=out_shapes,
        grid=(n_steps,),
        in_specs=in_specs,
        out_specs=out_specs,
        compiler_params=pltpu.CompilerParams(
            dimension_semantics=("arbitrary",),
            vmem_limit_bytes=_vmem_limit(est)),
        name="cast_ffn_weights",
    )(*stacks)


def _rope(x, cos, sin_signed):
    lane = lax.broadcasted_iota(jnp.int32, (x.shape[0], LANES), 1)
    upper = (lane % HEAD_DIM) >= (HEAD_DIM // 2)
    outs = []
    for c in range(x.shape[1] // LANES):
        xc = x[:, c * LANES:(c + 1) * LANES]
        from_below = pltpu.roll(xc, HEAD_DIM // 2, axis=1)
        from_above = pltpu.roll(xc, LANES - HEAD_DIM // 2, axis=1)
        outs.append(xc * cos + jnp.where(upper, from_below, from_above) * sin_signed)
    return jnp.concatenate(outs, axis=1)


def _pool_diff(hist, cur, inv_count_fn):
    run = jnp.concatenate([hist, cur], axis=0)
    outs = []
    for g, w in enumerate(POOL_WINDOWS):
        run = run + pltpu.roll(run, w // 2, axis=0)
        cols = slice(g * POOL_GROUP_WIDTH, (g + 1) * POOL_GROUP_WIDTH)
        outs.append(run[N_META:, :POOL_GROUP_WIDTH] * inv_count_fn(w) - cur[:, cols])
        run = run[:, POOL_GROUP_WIDTH:]
    return jnp.concatenate(outs, axis=1)


def _inproj_kernel(*refs, meta):
    if meta:
        (x_ref, g_ref, w_ref, cq_ref, sq_ref, ck_ref, sk_ref,
         o_ref, u_ref, hist_ref) = refs
    else:
        (x_ref, g_ref, w_ref, cq_ref, sq_ref, ck_ref, sk_ref, um_ref, *cast_src) = refs[:-5]
        o_ref, *cast_dst, hist_ref = refs[-5:]
        _cast_blocks(cast_src, cast_dst)
    j = pl.program_id(1)
    rows = x_ref.shape[0]
    chunk = lambda n: slice(n * ATTN_WIDTH, (n + 1) * ATTN_WIDTH)

    if not meta:
        @pl.when(j == 0)
        def _():
            hist_ref[...] = um_ref[...]

    xn = _rms_norm(x_ref[...], g_ref[...], NORM_EPS).astype(BF16)

    def project(n):
        return jnp.dot(xn, w_ref[:, chunk(n)], preferred_element_type=F32)

    o_ref[:, chunk(0)] = _rope(project(0), cq_ref[...], sq_ref[...]).astype(BF16)
    o_ref[:, chunk(1)] = _rope(project(1), ck_ref[...], sk_ref[...]).astype(BF16)
    o_ref[:, chunk(2)] = project(2).astype(BF16)

    acc = project(3)
    if meta:
        u_ref[...] = acc
        hist = jnp.zeros((N_META, POOL_WIDTH), F32)
        pos = lax.broadcasted_iota(jnp.int32, (rows, 1), 0)

        def inv_count(w):
            return 1.0 / jnp.minimum(pos + 1, w).astype(F32)
    else:
        hist = hist_ref[...]

        def inv_count(w):
            return 1.0 / w
    o_ref[:, chunk(3)] = _pool_diff(hist, acc, inv_count).astype(BF16)
    hist_ref[...] = acc[rows - N_META:, :]


def _inproj(x, g, w_in, tables, u_meta, cast_stacks=None, *, layer, batch, tl, meta):
    t = x.shape[0]
    seq = t // batch
    nj = seq // tl
    assert seq % tl == 0
    cq, sq, ck, sk = tables
    tok = lambda b, j: (b * nj + j, 0)
    tab = lambda b, j: (j, 0)
    const = lambda b, j: (0, 0)
    in_specs = [
        pl.BlockSpec((tl, D_MODEL), tok),
        pl.BlockSpec((1, D_MODEL), const),
        pl.BlockSpec((None, D_MODEL, IN_WIDTH), lambda b, j: (layer, 0, 0),
                     pipeline_mode=pl.Buffered(1)),
        pl.BlockSpec((tl, LANES), tab),
        pl.BlockSpec((tl, LANES), tab),
        pl.BlockSpec((tl, LANES), tab),
        pl.BlockSpec((tl, LANES), tab),
    ]
    args = [x, g, w_in, cq, sq, ck, sk]
    proj_shape = jax.ShapeDtypeStruct((t, IN_WIDTH), BF16)
    proj_spec = pl.BlockSpec((tl, IN_WIDTH), tok)
    if meta:
        out_shape = (proj_shape, jax.ShapeDtypeStruct((t, POOL_WIDTH), F32))
        out_specs = (proj_spec, pl.BlockSpec((tl, POOL_WIDTH), const))
    else:
        cast_in, cast_out, cast_shapes = _cast_plumbing(cast_stacks, layer, batch * nj,
                                                        lambda b, j: b * nj + j)
        in_specs += [pl.BlockSpec((N_META, POOL_WIDTH), const)] + cast_in
        args += [u_meta] + list(cast_stacks)
        out_shape = [proj_shape] + cast_shapes
        out_specs = [proj_spec] + cast_out
    est = (2 * tl * D_MODEL * 4 + D_MODEL * IN_WIDTH * 2 + 2 * tl * IN_WIDTH * 2
           + tl * D_MODEL * 2 + 4 * (tl + N_META) * POOL_WIDTH * 4 + 8 * tl * LANES * 4
           + (0 if meta else 2 * 6 * 3 * D_MODEL * D_FF // (batch * nj)))
    return pl.pallas_call(
        functools.partial(_inproj_kernel, meta=meta),
        out_shape=out_shape,
        grid=(batch, nj),
        in_specs=in_specs,
        out_specs=out_specs,
        scratch_shapes=[pltpu.VMEM((N_META, POOL_WIDTH), F32)],
        compiler_params=pltpu.CompilerParams(
            dimension_semantics=("arbitrary", "arbitrary"),
            vmem_limit_bytes=_vmem_limit(est)),
        name="inproj_meta" if meta else "inproj",
    )(*args)


def _stack_queries(q):
    lane = lax.broadcasted_iota(jnp.int32, q.shape, 1)
    zero = jnp.zeros_like(q)
    return jnp.concatenate([jnp.where(lane < HEAD_DIM, q, zero),
                            jnp.where(lane >= HEAD_DIM, q, zero)], axis=0)


def _scores(qq, k):
    return lax.dot_general(qq, k, (((1,), (1,)), ((), ())), preferred_element_type=F32)


def _lambda(lq1_ref, lk1_ref, lq2_ref, lk2_ref, lam_init):
    s1 = jnp.sum(lq1_ref[...] * lk1_ref[...], axis=-1, keepdims=True)
    s2 = jnp.sum(lq2_ref[...] * lk2_ref[...], axis=-1, keepdims=True)
    return jnp.exp(s1) - jnp.exp(s2) + lam_init


def _finish(acc, l, lam, g, lam_init, tq):
    o = acc[:tq] / l[:tq] - lam * (acc[tq:] / l[tq:])
    return _rms_norm(o, g, SUBLN_EPS) * (1.0 - lam_init)


def _attn_kernel(lq1_ref, lk1_ref, lq2_ref, lk2_ref, g_ref, q_ref, k_ref, v_ref, km_ref, vmt_ref,
                 o_ref, qqt_ref, vt_ref, m_ref, acc_ref, s0_ref, s1_ref,
                 *, lam_init, tq, hp):
    nq = q_ref.shape[0] // tq
    lam = _lambda(lq1_ref, lk1_ref, lq2_ref, lk2_ref, lam_init)
    heads = [slice(h * V_DIM, (h + 1) * V_DIM) for h in range(hp)]

    def block_rows(b):
        return pl.ds(pl.multiple_of(b * tq, tq), tq)

    ones_rows = jnp.ones((ONES_ROWS, tq), BF16)
    for h, hs in enumerate(heads):
        for c in range(nq):
            vt_ref[h, c, :V_DIM, :] = v_ref[c * tq:(c + 1) * tq, hs].T
            vt_ref[h, c, V_DIM:, :] = ones_rows

    def scores(k, h):
        return jnp.dot(k, qqt_ref[h], preferred_element_type=F32)

    def consume(h, s, vt, first):
        blk_max = jnp.max(s, axis=0, keepdims=True)
        m_new = blk_max if first else jnp.maximum(m_ref[h], blk_max)
        p = jnp.exp2(s - m_new)
        pv = jnp.dot(vt, p.astype(BF16), preferred_element_type=F32)
        if first:
            acc_ref[h] = pv
        else:
            acc_ref[h] = jnp.exp2(m_ref[h] - m_new) * acc_ref[h] + pv
        m_ref[h] = m_new

    def interleave(producers, consumers, lead, per_consumer):
        producers = list(producers)
        for task in producers[:lead]:
            task()
        rest = producers[lead:]
        for c, task in enumerate(consumers):
            task()
            for extra in rest[c * per_consumer:(c + 1) * per_consumer]:
                extra()
        for extra in rest[len(consumers) * per_consumer:]:
            extra()

    def step(b, cur_ref, nxt_ref):
        def prefetch(h, hs):
            def run():
                nxt_ref[h] = scores(k_ref[block_rows(b + 1), hs], h)
            return run

        def use(h):
            return lambda: consume(h, cur_ref[h], vt_ref[h, b], False)

        producers = [prefetch(h, hs) for h, hs in enumerate(heads)] if nxt_ref is not None else []
        interleave(producers, [use(h) for h in range(hp)], 2, 1)

    def q_tile(i, _):
        rows = block_rows(i)
        for h, hs in enumerate(heads):
            qqt_ref[h] = _stack_queries(q_ref[rows, hs]).T

        key = lax.broadcasted_iota(jnp.int32, (tq + N_META, 2 * tq), 0)
        qry = lax.broadcasted_iota(jnp.int32, (tq + N_META, 2 * tq), 1)
        visible = (key >= tq) | (key <= jnp.where(qry >= tq, qry - tq, qry))
        diag = [None] * hp

        def diag_scores(h, hs):
            def run():
                diag[h] = scores(jnp.concatenate([k_ref[rows, hs], km_ref[:, hs]], axis=0), h)
            return run

        def first_scores(h, hs):
            def run():
                s0_ref[h] = scores(k_ref[block_rows(0), hs], h)
            return run

        def use_diag(h, hs):
            return lambda: consume(h, jnp.where(visible, diag[h], jnp.finfo(F32).min),
                                   jnp.concatenate([vt_ref[h, i], vmt_ref[h]], axis=1), True)

        producers = [diag_scores(h, hs) for h, hs in enumerate(heads)]
        producers = producers[:2] + [t for h, hs in enumerate(heads)
                                     for t in (producers[h + 2:h + 3] + [first_scores(h, hs)])]
        interleave(producers, [use_diag(h, hs) for h, hs in enumerate(heads)], 2, 2)

        def body(b, _):
            @pl.when(b % 2 == 0)
            def _():
                step(b, s0_ref, s1_ref)

            @pl.when(b % 2 == 1)
            def _():
                step(b, s1_ref, s0_ref)
            return 0

        lax.fori_loop(0, i - 1, body, 0)

        @pl.when((i >= 1) & (i % 2 == 1))
        def _():
            step(i - 1, s0_ref, None)

        @pl.when((i >= 1) & (i % 2 == 0))
        def _():
            step(i - 1, s1_ref, None)

        for h, hs in enumerate(heads):
            acc = acc_ref[h, :V_DIM, :]
            l = acc_ref[h, V_DIM:V_DIM + 1, :]
            o = acc[:, :tq] / l[:, :tq] - lam * (acc[:, tq:] / l[:, tq:])
            o = o * lax.rsqrt(jnp.mean(o * o, axis=0, keepdims=True) + SUBLN_EPS)
            o = o * g_ref[...] * (1.0 - lam_init)
            o_ref[rows, hs] = o.T.astype(BF16)
        return 0

    lax.fori_loop(0, nq, q_tile, 0)


def _attn(lam_params, subln_g_col, proj, proj_meta, *, batch, lam_init, tq, hp):
    t = proj.shape[0]
    seq = t // batch
    assert seq % tq == 0 and N_HEADS % hp == 0
    groups = N_HEADS // hp
    width = hp * V_DIM
    small = lambda b, g: (0, 0)
    lam_specs = [pl.BlockSpec((1, HEAD_DIM), small)] * 4
    score_buf = pltpu.VMEM((hp, tq, 2 * tq), F32)
    v_meta_t = proj_meta[:, 2 * ATTN_WIDTH:3 * ATTN_WIDTH].T.reshape(N_HEADS, V_DIM, N_META)
    v_meta_t = jnp.concatenate([v_meta_t, jnp.ones((N_HEADS, ONES_ROWS, N_META), BF16)], axis=1)
    est = (2 * 4 * seq * width * 2 + seq * width * 2
           + hp * (2 * tq * V_DIM * 2 + 2 * tq * 2 * tq * 4 + 8 * tq * 2 * tq * 4 // hp
                   + 2 * V_DIM * 2 * tq * 4))
    return pl.pallas_call(
        functools.partial(_attn_kernel, lam_init=lam_init, tq=tq, hp=hp),
        out_shape=jax.ShapeDtypeStruct((t, ATTN_WIDTH), BF16),
        grid=(batch, groups),
        in_specs=lam_specs + [
            pl.BlockSpec((V_DIM, 1), small),
            pl.BlockSpec((seq, width), lambda b, g: (b, g)),
            pl.BlockSpec((seq, width), lambda b, g: (b, groups + g)),
            pl.BlockSpec((seq, width), lambda b, g: (b, 2 * groups + g)),
            pl.BlockSpec((N_META, width), lambda b, g: (0, groups + g)),
            pl.BlockSpec((hp, V_DIM + ONES_ROWS, N_META), lambda b, g: (g, 0, 0)),
        ],
        out_specs=pl.BlockSpec((seq, width), lambda b, g: (b, g)),
        scratch_shapes=[pltpu.VMEM((hp, V_DIM, 2 * tq), BF16),
                        pltpu.VMEM((hp, seq // tq, V_DIM + ONES_ROWS, tq), BF16),
                        pltpu.VMEM((hp, 1, 2 * tq), F32),
                        pltpu.VMEM((hp, V_DIM + ONES_ROWS, 2 * tq), F32), score_buf, score_buf],
        compiler_params=pltpu.CompilerParams(
            dimension_semantics=("parallel", "parallel"),
            vmem_limit_bytes=_vmem_limit(est)),
        name="attn",
    )(*lam_params, subln_g_col, proj, proj, proj, proj_meta, v_meta_t)


def _attn_meta_kernel(lq1_ref, lk1_ref, lq2_ref, lk2_ref, g_ref, q_ref, k_ref, v_ref, o_ref,
                      *, lam_init):
    qq = _stack_queries(q_ref[...])
    s = _scores(qq, k_ref[...])
    row = lax.broadcasted_iota(jnp.int32, s.shape, 0) % N_META
    col = lax.broadcasted_iota(jnp.int32, s.shape, 1)
    s = jnp.where(col <= row, s, jnp.finfo(F32).min)
    m = jnp.max(s, axis=-1, keepdims=True)
    p = jnp.exp2(s - m)
    l = jnp.sum(p, axis=-1, keepdims=True)
    acc = jnp.dot(p.astype(BF16), v_ref[...], preferred_element_type=F32)
    lam = _lambda(lq1_ref, lk1_ref, lq2_ref, lk2_ref, lam_init)
    o_ref[...] = _finish(acc, l, lam, g_ref[...], lam_init, N_META).astype(BF16)


def _attn_meta(lam_params, subln_g, proj_meta, *, lam_init):
    small = lambda h: (0, 0)
    return pl.pallas_call(
        functools.partial(_attn_meta_kernel, lam_init=lam_init),
        out_shape=jax.ShapeDtypeStruct((N_META, ATTN_WIDTH), BF16),
        grid=(N_HEADS,),
        in_specs=[pl.BlockSpec((1, HEAD_DIM), small)] * 4 + [
            pl.BlockSpec((1, V_DIM), small),
            pl.BlockSpec((N_META, V_DIM), lambda h: (0, h)),
            pl.BlockSpec((N_META, V_DIM), lambda h: (0, N_HEADS + h)),
            pl.BlockSpec((N_META, V_DIM), lambda h: (0, 2 * N_HEADS + h)),
        ],
        out_specs=pl.BlockSpec((N_META, V_DIM), lambda h: (0, h)),
        compiler_params=pltpu.CompilerParams(dimension_semantics=("arbitrary",)),
        name="attn_meta",
    )(*lam_params, subln_g, proj_meta, proj_meta, proj_meta)


def _outproj_kernel(h_ref, a_ref, d_ref, wp_ref, ps_ref, wo_ref, *rest):
    cast_src, (o_ref, *cast_dst) = rest[:len(rest) // 2], rest[len(rest) // 2:]
    _cast_blocks(cast_src, cast_dst)
    diff = d_ref[...]
    pools = []
    for g in range(len(POOL_WINDOWS)):
        cols = slice(g * POOL_GROUP_WIDTH, (g + 1) * POOL_GROUP_WIDTH)
        pools.append(jnp.dot(diff[:, cols], wp_ref[g], preferred_element_type=F32))
    pool = (jnp.concatenate(pools, axis=1) * ps_ref[...]).astype(BF16)
    mix = jnp.concatenate([a_ref[...], pool], axis=1)
    o_ref[...] = h_ref[...] + jnp.dot(mix, wo_ref[...], preferred_element_type=F32)


def _outproj(h, attn, proj, w_pool, pool_scale, w_out, cast_stacks=None, *, layer, cast_layer=0,
             tm):
    t = h.shape[0]
    assert t % tm == 0
    est = (2 * 2 * tm * D_MODEL * 4 + 2 * 2 * tm * ATTN_WIDTH * 2 + 2 * D_MODEL * D_MODEL * 2
           + 2 * POOL_WIDTH * POOL_GROUP_WIDTH * 2 + 3 * tm * D_MODEL * 4)
    in_specs = [
        pl.BlockSpec((tm, D_MODEL), lambda i: (i, 0)),
        pl.BlockSpec((tm, ATTN_WIDTH), lambda i: (i, 0)),
        pl.BlockSpec((tm, POOL_WIDTH), lambda i: (i, 3)),
        pl.BlockSpec((None, len(POOL_WINDOWS), POOL_GROUP_WIDTH, POOL_GROUP_WIDTH),
                     lambda i: (layer, 0, 0, 0)),
        pl.BlockSpec((1, POOL_WIDTH), lambda i: (0, 0)),
        pl.BlockSpec((None, D_MODEL, D_MODEL), lambda i: (layer, 0, 0)),
    ]
    args = [h, attn, proj, w_pool, pool_scale, w_out]
    out_shape = [jax.ShapeDtypeStruct((t, D_MODEL), F32)]
    out_specs = [pl.BlockSpec((tm, D_MODEL), lambda i: (i, 0))]
    if cast_stacks is not None:
        cast_in, cast_out, cast_shapes = _cast_plumbing(cast_stacks, cast_layer, t // tm,
                                                        lambda i: i)
        in_specs += cast_in
        args += list(cast_stacks)
        out_shape += cast_shapes
        out_specs += cast_out
        est += 2 * 6 * 3 * D_MODEL * D_FF // (t // tm)
    outs = pl.pallas_call(
        _outproj_kernel,
        out_shape=out_shape,
        grid=(t // tm,),
        in_specs=in_specs,
        out_specs=out_specs,
        compiler_params=pltpu.CompilerParams(
            dimension_semantics=("arbitrary",),
            vmem_limit_bytes=_vmem_limit(est)),
        name="outproj",
    )(*args)
    return outs[0] if cast_stacks is None else outs


def _rope_tables(length):
    pos = jnp.arange(length, dtype=F32)
    inv_freq = 1.0 / (ROPE_THETA ** (jnp.arange(0, HEAD_DIM, 2, dtype=F32) / HEAD_DIM))
    ang = pos[:, None] * inv_freq[None, :]
    ang = jnp.concatenate([ang, ang, ang, ang], axis=-1)
    sign = jnp.where((jnp.arange(LANES) % HEAD_DIM) < HEAD_DIM // 2, -1.0, 1.0).astype(F32)
    cos, sin = jnp.cos(ang), jnp.sin(ang) * sign
    scale = HEAD_DIM ** -0.5 * math.log2(math.e)
    return cos * scale, sin * scale, cos, sin


def kernel(x, meta_tokens, ffn1_norm_g, ffn1_w_gate, ffn1_w_up, ffn1_w_down, mix_norm_g, w_in,
           lam_q1, lam_k1, lam_q2, lam_k2, subln_g, w_pool, pool_scale, w_out,
           ffn2_norm_g, ffn2_w_gate, ffn2_w_up, ffn2_w_down, final_norm_g):
    batch, seq, d = x.shape
    depth = w_in.shape[0]
    assert d == D_MODEL and meta_tokens.shape == (N_META, D_MODEL)

    tables = _rope_tables(N_META + seq)
    tab_meta = tuple(tb[:N_META] for tb in tables)
    tab_real = tuple(tb[N_META:] for tb in tables)
    final_g = final_norm_g.reshape(1, D_MODEL)

    h = x.reshape(batch * seq, D_MODEL)
    hm = meta_tokens.astype(x.dtype)

    wi = w_in.astype(BF16)
    wo = w_out.astype(BF16)
    wp = w_pool.astype(BF16)
    ffn1_stacks = (ffn1_w_gate, ffn1_w_up, ffn1_w_down)
    ffn2_stacks = (ffn2_w_gate, ffn2_w_up, ffn2_w_down)
    ffn1_w = _cast_ffn_weights(ffn1_stacks, 0)

    for layer in range(depth):
        lam_init = 0.8 - 0.6 * math.exp(-0.3 * layer)
        last = layer == depth - 1
        g1 = ffn1_norm_g[layer].reshape(1, D_MODEL)
        gm = mix_norm_g[layer].reshape(1, D_MODEL)
        g2 = ffn2_norm_g[layer].reshape(1, D_MODEL)
        ps = pool_scale[layer].reshape(1, POOL_WIDTH)
        sg = subln_g[layer].reshape(1, V_DIM)
        lam_params = tuple(p[layer].reshape(1, HEAD_DIM) for p in (lam_q1, lam_k1, lam_q2, lam_k2))

        hm = _ffn(hm, g1, *ffn1_w, final_g, layer=0, tm=N_META, tf=512, apply_final=False)
        proj_m, u_m = _inproj(hm, gm, wi, tab_meta, None, layer=layer, batch=1, tl=N_META, meta=True)
        attn_m = _attn_meta(lam_params, sg, proj_m, lam_init=lam_init)
        h = _ffn(h, g1, *ffn1_w, final_g, layer=0, tm=1024, tf=512, apply_final=False)
        proj, *ffn2_w = _inproj(h, gm, wi, tab_real, u_m, ffn2_stacks, layer=layer, batch=batch,
                                tl=512, meta=False)
        attn = _attn(lam_params, sg.reshape(V_DIM, 1), proj, proj_m, batch=batch,
                     lam_init=lam_init, tq=256, hp=8)

        if last:
            h = _outproj(h, attn, proj, wp, ps, wo, layer=layer, tm=512)
        else:
            h, *ffn1_w = _outproj(h, attn, proj, wp, ps, wo, ffn1_stacks, layer=layer,
                                  cast_layer=layer + 1, tm=512)
            hm = _outproj(hm, attn_m, proj_m, wp, ps, wo, layer=layer, tm=N_META)
            hm = _ffn(hm, g2, *ffn2_w, final_g, layer=0, tm=N_META, tf=512, apply_final=False)
        h = _ffn(h, g2, *ffn2_w, final_g, layer=0, tm=1024, tf=512, apply_final=last)

    return h.reshape(batch, seq, D_MODEL)
```

```python
import functools
import math

import jax
import jax.numpy as jnp
from jax import lax
from jax.experimental import pallas as pl
from jax.experimental.pallas import tpu as pltpu

F32 = jnp.float32
BF16 = jnp.bfloat16

D_MODEL = 2048
N_META = 16
ATTN_WIDTH = 1024
POOL_WIDTH = 1024
HEAD_DIM = 64
V_DIM = 2 * HEAD_DIM
N_HEADS = ATTN_WIDTH // V_DIM
POOL_WINDOWS = (2, 4, 8, 16)
POOL_GROUP_WIDTH = POOL_WIDTH // len(POOL_WINDOWS)
IN_WIDTH = 3 * ATTN_WIDTH + POOL_WIDTH
D_FF = 5632
ROPE_THETA = 10000.0
NORM_EPS = 1e-6
SUBLN_EPS = 1e-5

LANES = 128
ONES_ROWS = 16
V7X_VMEM_BYTES = 64 * 1024 * 1024


def _vmem_limit(estimate_bytes):
    return int(min(estimate_bytes * 5 // 4 + (4 << 20), V7X_VMEM_BYTES * 7 // 8))


def _rms_norm(x, g, eps):
    return x * lax.rsqrt(jnp.mean(x * x, axis=-1, keepdims=True) + eps) * g


def _ffn_kernel(x_ref, g_ref, wg_ref, wu_ref, wd_ref, fg_ref, o_ref, xn_ref, *, apply_final):
    f = pl.program_id(1)

    @pl.when(f == 0)
    def _():
        x = x_ref[...]
        xn_ref[...] = _rms_norm(x, g_ref[...], NORM_EPS).astype(BF16)
        o_ref[...] = x

    xn = xn_ref[...]
    gate = jnp.dot(xn, wg_ref[...], preferred_element_type=F32)
    up = jnp.dot(xn, wu_ref[...], preferred_element_type=F32)
    act = (gate * jax.nn.sigmoid(gate)) * (0.5 * up)
    o_ref[...] += jnp.dot(act.astype(BF16), wd_ref[...], preferred_element_type=F32)

    if apply_final:
        @pl.when(f == pl.num_programs(1) - 1)
        def _():
            o_ref[...] = _rms_norm(o_ref[...], fg_ref[...], NORM_EPS)


def _ffn(x, g, wg, wu, wd, final_g, *, layer, tm, tf, apply_final):
    t = x.shape[0]
    assert t % tm == 0 and D_FF % tf == 0
    est = (2 * 2 * tm * D_MODEL * 4 + tm * D_MODEL * 2 + 2 * 3 * D_MODEL * tf * 2
           + 3 * tm * tf * 4)
    return pl.pallas_call(
        functools.partial(_ffn_kernel, apply_final=apply_final),
        out_shape=jax.ShapeDtypeStruct((t, D_MODEL), F32),
        grid=(t // tm, D_FF // tf),
        in_specs=[
            pl.BlockSpec((tm, D_MODEL), lambda i, f: (i, 0)),
            pl.BlockSpec((1, D_MODEL), lambda i, f: (0, 0)),
            pl.BlockSpec((None, D_MODEL, tf), lambda i, f: (layer, 0, f)),
            pl.BlockSpec((None, D_MODEL, tf), lambda i, f: (layer, 0, f)),
            pl.BlockSpec((None, tf, D_MODEL), lambda i, f: (layer, f, 0)),
            pl.BlockSpec((1, D_MODEL), lambda i, f: (0, 0)),
        ],
        out_specs=pl.BlockSpec((tm, D_MODEL), lambda i, f: (i, 0)),
        scratch_shapes=[pltpu.VMEM((tm, D_MODEL), BF16)],
        compiler_params=pltpu.CompilerParams(
            dimension_semantics=("parallel", "arbitrary"),
            vmem_limit_bytes=_vmem_limit(est)),
        name="ffn",
    )(x, g, wg, wu, wd, final_g)


def _cast_plumbing(stacks, layer, n_steps, step_of):
    rows_gu = D_MODEL // n_steps
    rows_d = 2 * D_FF // n_steps
    assert D_MODEL % n_steps == 0 and (2 * D_FF) % n_steps == 0 and rows_d % 16 == 0

    def specs(lead):
        gu = pl.BlockSpec((None, rows_gu, D_FF), lambda *g: (lead, step_of(*g), 0))
        dn = pl.BlockSpec((None, rows_d, D_MODEL), lambda *g: (lead, step_of(*g) // 2, 0))
        return [gu, gu, dn]

    out_shapes = [jax.ShapeDtypeStruct((1,) + w.shape[1:], BF16) for w in stacks]
    return specs(layer), specs(0), out_shapes


def _cast_blocks(src_refs, dst_refs):
    for src, dst in zip(src_refs, dst_refs):
        dst[...] = src[...].astype(BF16)


def _cast_kernel(*refs):
    _cast_blocks(refs[:len(refs) // 2], refs[len(refs) // 2:])


def _cast_ffn_weights(stacks, layer, n_steps=32):
    in_specs, out_specs, out_shapes = _cast_plumbing(stacks, layer, n_steps, lambda i: i)
    est = 2 * 6 * 3 * D_MODEL * D_FF // n_steps
    return pl.pallas_call(
        _cast_kernel,
        out_shape=out_shapes,
        grid=(n_steps,),
        in_specs=in_specs,
        out_specs=out_specs,
        compiler_params=pltpu.CompilerParams(
            dimension_semantics=("arbitrary",),
            vmem_limit_bytes=_vmem_limit(est)),
        name="cast_ffn_weights",
    )(*stacks)


def _rope(x, cos, sin_signed):
    lane = lax.broadcasted_iota(jnp.int32, (x.shape[0], LANES), 1)
    upper = (lane % HEAD_DIM) >= (HEAD_DIM // 2)
    outs = []
    for c in range(x.shape[1] // LANES):
        xc = x[:, c * LANES:(c + 1) * LANES]
        from_below = pltpu.roll(xc, HEAD_DIM // 2, axis=1)
        from_above = pltpu.roll(xc, LANES - HEAD_DIM // 2, axis=1)
        outs.append(xc * cos + jnp.where(upper, from_below, from_above) * sin_signed)
    return jnp.concatenate(outs, axis=1)


def _pool_diff(hist, cur, inv_count_fn):
    run = jnp.concatenate([hist, cur], axis=0)
    outs = []
    for g, w in enumerate(POOL_WINDOWS):
        run = run + pltpu.roll(run, w // 2, axis=0)
        cols = slice(g * POOL_GROUP_WIDTH, (g + 1) * POOL_GROUP_WIDTH)
        outs.append(run[N_META:, :POOL_GROUP_WIDTH] * inv_count_fn(w) - cur[:, cols])
        run = run[:, POOL_GROUP_WIDTH:]
    return jnp.concatenate(outs, axis=1)


def _inproj_kernel(*refs, meta):
    if meta:
        (x_ref, g_ref, w_ref, cq_ref, sq_ref, ck_ref, sk_ref,
         o_ref, u_ref, hist_ref) = refs
    else:
        (x_ref, g_ref, w_ref, cq_ref, sq_ref, ck_ref, sk_ref, um_ref, *cast_src) = refs[:-5]
        o_ref, *cast_dst, hist_ref = refs[-5:]
        _cast_blocks(cast_src, cast_dst)
    j = pl.program_id(1)
    rows = x_ref.shape[0]
    chunk = lambda n: slice(n * ATTN_WIDTH, (n + 1) * ATTN_WIDTH)

    if not meta:
        @pl.when(j == 0)
        def _():
            hist_ref[...] = um_ref[...]

    xn = _rms_norm(x_ref[...], g_ref[...], NORM_EPS).astype(BF16)

    def project(n):
        return jnp.dot(xn, w_ref[:, chunk(n)], preferred_element_type=F32)

    o_ref[:, chunk(0)] = _rope(project(0), cq_ref[...], sq_ref[...]).astype(BF16)
    o_ref[:, chunk(1)] = _rope(project(1), ck_ref[...], sk_ref[...]).astype(BF16)
    o_ref[:, chunk(2)] = project(2).astype(BF16)

    acc = project(3)
    if meta:
        u_ref[...] = acc
        hist = jnp.zeros((N_META, POOL_WIDTH), F32)
        pos = lax.broadcasted_iota(jnp.int32, (rows, 1), 0)

        def inv_count(w):
            return 1.0 / jnp.minimum(pos + 1, w).astype(F32)
    else:
        hist = hist_ref[...]

        def inv_count(w):
            return 1.0 / w
    o_ref[:, chunk(3)] = _pool_diff(hist, acc, inv_count).astype(BF16)
    hist_ref[...] = acc[rows - N_META:, :]


def _inproj(x, g, w_in, tables, u_meta, cast_stacks=None, *, layer, batch, tl, meta):
    t = x.shape[0]
    seq = t // batch
    nj = seq // tl
    assert seq % tl == 0
    cq, sq, ck, sk = tables
    tok = lambda b, j: (b * nj + j, 0)
    tab = lambda b, j: (j, 0)
    const = lambda b, j: (0, 0)
    in_specs = [
        pl.BlockSpec((tl, D_MODEL), tok),
        pl.BlockSpec((1, D_MODEL), const),
        pl.BlockSpec((None, D_MODEL, IN_WIDTH), lambda b, j: (layer, 0, 0),
                     pipeline_mode=pl.Buffered(1)),
        pl.BlockSpec((tl, LANES), tab),
        pl.BlockSpec((tl, LANES), tab),
        pl.BlockSpec((tl, LANES), tab),
        pl.BlockSpec((tl, LANES), tab),
    ]
    args = [x, g, w_in, cq, sq, ck, sk]
    proj_shape = jax.ShapeDtypeStruct((t, IN_WIDTH), BF16)
    proj_spec = pl.BlockSpec((tl, IN_WIDTH), tok)
    if meta:
        out_shape = (proj_shape, jax.ShapeDtypeStruct((t, POOL_WIDTH), F32))
        out_specs = (proj_spec, pl.BlockSpec((tl, POOL_WIDTH), const))
    else:
        cast_in, cast_out, cast_shapes = _cast_plumbing(cast_stacks, layer, batch * nj,
                                                        lambda b, j: b * nj + j)
        in_specs += [pl.BlockSpec((N_META, POOL_WIDTH), const)] + cast_in
        args += [u_meta] + list(cast_stacks)
        out_shape = [proj_shape] + cast_shapes
        out_specs = [proj_spec] + cast_out
    est = (2 * tl * D_MODEL * 4 + D_MODEL * IN_WIDTH * 2 + 2 * tl * IN_WIDTH * 2
           + tl * D_MODEL * 2 + 4 * (tl + N_META) * POOL_WIDTH * 4 + 8 * tl * LANES * 4
           + (0 if meta else 2 * 6 * 3 * D_MODEL * D_FF // (batch * nj)))
    return pl.pallas_call(
        functools.partial(_inproj_kernel, meta=meta),
        out_shape=out_shape,
        grid=(batch, nj),
        in_specs=in_specs,
        out_specs=out_specs,
        scratch_shapes=[pltpu.VMEM((N_META, POOL_WIDTH), F32)],
        compiler_params=pltpu.CompilerParams(
            dimension_semantics=("arbitrary", "arbitrary"),
            vmem_limit_bytes=_vmem_limit(est)),
        name="inproj_meta" if meta else "inproj",
    )(*args)


def _stack_queries(q):
    lane = lax.broadcasted_iota(jnp.int32, q.shape, 1)
    zero = jnp.zeros_like(q)
    return jnp.concatenate([jnp.where(lane < HEAD_DIM, q, zero),
                            jnp.where(lane >= HEAD_DIM, q, zero)], axis=0)


def _scores(qq, k):
    return lax.dot_general(qq, k, (((1,), (1,)), ((), ())), preferred_element_type=F32)


def _lambda(lq1_ref, lk1_ref, lq2_ref, lk2_ref, lam_init):
    s1 = jnp.sum(lq1_ref[...] * lk1_ref[...], axis=-1, keepdims=True)
    s2 = jnp.sum(lq2_ref[...] * lk2_ref[...], axis=-1, keepdims=True)
    return jnp.exp(s1) - jnp.exp(s2) + lam_init


def _finish(acc, l, lam, g, lam_init, tq):
    o = acc[:tq] / l[:tq] - lam * (acc[tq:] / l[tq:])
    return _rms_norm(o, g, SUBLN_EPS) * (1.0 - lam_init)


def _attn_kernel(lq1_ref, lk1_ref, lq2_ref, lk2_ref, g_ref, q_ref, k_ref, v_ref, km_ref, vmt_ref,
                 o_ref, qqt_ref, vt_ref, m_ref, acc_ref, s0_ref, s1_ref,
                 *, lam_init, tq, hp):
    nq = q_ref.shape[0] // tq
    lam = _lambda(lq1_ref, lk1_ref, lq2_ref, lk2_ref, lam_init)
    heads = [slice(h * V_DIM, (h + 1) * V_DIM) for h in range(hp)]

    def block_rows(b):
        return pl.ds(pl.multiple_of(b * tq, tq), tq)

    qqt_ref[...] = jnp.zeros(qqt_ref.shape, BF16)
    acc_ref[1] = jnp.ones(acc_ref.shape[1:], F32)
    ones_rows = jnp.ones((ONES_ROWS, tq), BF16)
    for h, hs in enumerate(heads):
        for c in range(nq):
            vt_ref[h, c, :V_DIM, :] = v_ref[c * tq:(c + 1) * tq, hs].T
            vt_ref[h, c, V_DIM:, :] = ones_rows

    def scores(k, h):
        return jnp.dot(k, qqt_ref[h], preferred_element_type=F32)

    def consume(slot, h, s, vt, first):
        blk_max = jnp.max(s, axis=0, keepdims=True)
        m_new = blk_max if first else jnp.maximum(m_ref[h], blk_max)
        p = jnp.exp2(s - m_new)
        pv = jnp.dot(vt, p.astype(BF16), preferred_element_type=F32)
        if first:
            acc_ref[slot, h] = pv
        else:
            acc_ref[slot, h] = jnp.exp2(m_ref[h] - m_new) * acc_ref[slot, h] + pv
        m_ref[h] = m_new

    def finish(slot, rows, h, hs):
        acc = acc_ref[slot, h, :V_DIM, :]
        inv_l = 1.0 / acc_ref[slot, h, V_DIM:V_DIM + 1, :]
        o = acc[:, :tq] * inv_l[:, :tq] - acc[:, tq:] * (lam * inv_l[:, tq:])
        o = o * lax.rsqrt(jnp.mean(o * o, axis=0, keepdims=True) + SUBLN_EPS)
        o_ref[rows, hs] = (o * (g_ref[...] * (1.0 - lam_init))).T.astype(BF16)

    def interleave(producers, consumers, lead, per_consumer):
        producers = list(producers)
        for task in producers[:lead]:
            task()
        rest = producers[lead:]
        for c, task in enumerate(consumers):
            task()
            for extra in rest[c * per_consumer:(c + 1) * per_consumer]:
                extra()
        for extra in rest[len(consumers) * per_consumer:]:
            extra()

    def step(slot, b, cur_ref, nxt_ref):
        def prefetch(h, hs):
            def run():
                nxt_ref[h] = scores(k_ref[block_rows(b + 1), hs], h)
            return run

        def use(h):
            return lambda: consume(slot, h, cur_ref[h], vt_ref[h, b], False)

        producers = [prefetch(h, hs) for h, hs in enumerate(heads)] if nxt_ref is not None else []
        interleave(producers, [use(h) for h in range(hp)], 2, 1)

    def q_tile(i, _):
        rows = block_rows(i)
        slot = i % 2
        for h, hs in enumerate(heads):
            qt = q_ref[rows, hs].T
            qqt_ref[h, :HEAD_DIM, :tq] = qt[:HEAD_DIM]
            qqt_ref[h, HEAD_DIM:, tq:] = qt[HEAD_DIM:]

        prev_rows = block_rows(jnp.maximum(i - 1, 0))

        key = lax.broadcasted_iota(jnp.int32, (tq + N_META, 2 * tq), 0)
        qry = lax.broadcasted_iota(jnp.int32, (tq + N_META, 2 * tq), 1)
        visible = (key >= tq) | (key <= jnp.where(qry >= tq, qry - tq, qry))
        diag = [None] * hp

        def diag_scores(h, hs):
            def run():
                diag[h] = scores(jnp.concatenate([k_ref[rows, hs], km_ref[:, hs]], axis=0), h)
            return run

        def first_scores(h, hs):
            def run():
                s0_ref[h] = scores(k_ref[block_rows(0), hs], h)
            return run

        def use_diag(h, hs):
            def run():
                consume(slot, h, jnp.where(visible, diag[h], jnp.finfo(F32).min),
                        jnp.concatenate([vt_ref[h, i], vmt_ref[h]], axis=1), True)
                finish(1 - slot, prev_rows, h, hs)
            return run

        producers = [diag_scores(h, hs) for h, hs in enumerate(heads)]
        producers = producers[:2] + [t for h, hs in enumerate(heads)
                                     for t in (producers[h + 2:h + 3] + [first_scores(h, hs)])]
        interleave(producers, [use_diag(h, hs) for h, hs in enumerate(heads)], 2, 2)

        def body(b, _):
            @pl.when(b % 2 == 0)
            def _():
                step(slot, b, s0_ref, s1_ref)

            @pl.when(b % 2 == 1)
            def _():
                step(slot, b, s1_ref, s0_ref)
            return 0

        lax.fori_loop(0, i - 1, body, 0)

        @pl.when((i >= 1) & (i % 2 == 1))
        def _():
            step(slot, i - 1, s0_ref, None)

        @pl.when((i >= 1) & (i % 2 == 0))
        def _():
            step(slot, i - 1, s1_ref, None)

        return 0

    lax.fori_loop(0, nq, q_tile, 0)
    for h, hs in enumerate(heads):
        finish((nq - 1) % 2, pl.ds((nq - 1) * tq, tq), h, hs)


def _attn(lam_params, subln_g_col, proj, proj_meta, *, batch, lam_init, tq, hp):
    t = proj.shape[0]
    seq = t // batch
    assert seq % tq == 0 and N_HEADS % hp == 0
    groups = N_HEADS // hp
    width = hp * V_DIM
    small = lambda b, g: (0, 0)
    lam_specs = [pl.BlockSpec((1, HEAD_DIM), small)] * 4
    score_buf = pltpu.VMEM((hp, tq, 2 * tq), F32)
    v_meta_t = proj_meta[:, 2 * ATTN_WIDTH:3 * ATTN_WIDTH].T.reshape(N_HEADS, V_DIM, N_META)
    v_meta_t = jnp.concatenate([v_meta_t, jnp.ones((N_HEADS, ONES_ROWS, N_META), BF16)], axis=1)
    est = (2 * 4 * seq * width * 2 + seq * width * 2
           + hp * (2 * tq * V_DIM * 2 + 2 * tq * 2 * tq * 4 + 8 * tq * 2 * tq * 4 // hp
                   + 2 * V_DIM * 2 * tq * 4))
    return pl.pallas_call(
        functools.partial(_attn_kernel, lam_init=lam_init, tq=tq, hp=hp),
        out_shape=jax.ShapeDtypeStruct((t, ATTN_WIDTH), BF16),
        grid=(batch, groups),
        in_specs=lam_specs + [
            pl.BlockSpec((V_DIM, 1), small),
            pl.BlockSpec((seq, width), lambda b, g: (b, g)),
            pl.BlockSpec((seq, width), lambda b, g: (b, groups + g)),
            pl.BlockSpec((seq, width), lambda b, g: (b, 2 * groups + g)),
            pl.BlockSpec((N_META, width), lambda b, g: (0, groups + g)),
            pl.BlockSpec((hp, V_DIM + ONES_ROWS, N_META), lambda b, g: (g, 0, 0)),
        ],
        out_specs=pl.BlockSpec((seq, width), lambda b, g: (b, g)),
        scratch_shapes=[pltpu.VMEM((hp, V_DIM, 2 * tq), BF16),
                        pltpu.VMEM((hp, seq // tq, V_DIM + ONES_ROWS, tq), BF16),
                        pltpu.VMEM((hp, 1, 2 * tq), F32),
                        pltpu.VMEM((2, hp, V_DIM + ONES_ROWS, 2 * tq), F32), score_buf, score_buf],
        compiler_params=pltpu.CompilerParams(
            dimension_semantics=("parallel", "parallel"),
            vmem_limit_bytes=_vmem_limit(est)),
        name="attn",
    )(*lam_params, subln_g_col, proj, proj, proj, proj_meta, v_meta_t)


def _attn_meta_kernel(lq1_ref, lk1_ref, lq2_ref, lk2_ref, g_ref, q_ref, k_ref, v_ref, o_ref,
                      *, lam_init):
    qq = _stack_queries(q_ref[...])
    s = _scores(qq, k_ref[...])
    row = lax.broadcasted_iota(jnp.int32, s.shape, 0) % N_META
    col = lax.broadcasted_iota(jnp.int32, s.shape, 1)
    s = jnp.where(col <= row, s, jnp.finfo(F32).min)
    m = jnp.max(s, axis=-1, keepdims=True)
    p = jnp.exp2(s - m)
    l = jnp.sum(p, axis=-1, keepdims=True)
    acc = jnp.dot(p.astype(BF16), v_ref[...], preferred_element_type=F32)
    lam = _lambda(lq1_ref, lk1_ref, lq2_ref, lk2_ref, lam_init)
    o_ref[...] = _finish(acc, l, lam, g_ref[...], lam_init, N_META).astype(BF16)


def _attn_meta(lam_params, subln_g, proj_meta, *, lam_init):
    small = lambda h: (0, 0)
    return pl.pallas_call(
        functools.partial(_attn_meta_kernel, lam_init=lam_init),
        out_shape=jax.ShapeDtypeStruct((N_META, ATTN_WIDTH), BF16),
        grid=(N_HEADS,),
        in_specs=[pl.BlockSpec((1, HEAD_DIM), small)] * 4 + [
            pl.BlockSpec((1, V_DIM), small),
            pl.BlockSpec((N_META, V_DIM), lambda h: (0, h)),
            pl.BlockSpec((N_META, V_DIM), lambda h: (0, N_HEADS + h)),
            pl.BlockSpec((N_META, V_DIM), lambda h: (0, 2 * N_HEADS + h)),
        ],
        out_specs=pl.BlockSpec((N_META, V_DIM), lambda h: (0, h)),
        compiler_params=pltpu.CompilerParams(dimension_semantics=("arbitrary",)),
        name="attn_meta",
    )(*lam_params, subln_g, proj_meta, proj_meta, proj_meta)


def _outproj_kernel(h_ref, a_ref, d_ref, wp_ref, ps_ref, wo_ref, *rest):
    cast_src, (o_ref, *cast_dst) = rest[:len(rest) // 2], rest[len(rest) // 2:]
    _cast_blocks(cast_src, cast_dst)
    diff = d_ref[...]
    pools = []
    for g in range(len(POOL_WINDOWS)):
        cols = slice(g * POOL_GROUP_WIDTH, (g + 1) * POOL_GROUP_WIDTH)
        pools.append(jnp.dot(diff[:, cols], wp_ref[g], preferred_element_type=F32))
    pool = (jnp.concatenate(pools, axis=1) * ps_ref[...]).astype(BF16)
    mix = jnp.concatenate([a_ref[...], pool], axis=1)
    o_ref[...] = h_ref[...] + jnp.dot(mix, wo_ref[...], preferred_element_type=F32)


def _outproj(h, attn, proj, w_pool, pool_scale, w_out, cast_stacks=None, *, layer, cast_layer=0,
             tm):
    t = h.shape[0]
    assert t % tm == 0
    est = (2 * 2 * tm * D_MODEL * 4 + 2 * 2 * tm * ATTN_WIDTH * 2 + 2 * D_MODEL * D_MODEL * 2
           + 2 * POOL_WIDTH * POOL_GROUP_WIDTH * 2 + 3 * tm * D_MODEL * 4)
    in_specs = [
        pl.BlockSpec((tm, D_MODEL), lambda i: (i, 0)),
        pl.BlockSpec((tm, ATTN_WIDTH), lambda i: (i, 0)),
        pl.BlockSpec((tm, POOL_WIDTH), lambda i: (i, 3)),
        pl.BlockSpec((None, len(POOL_WINDOWS), POOL_GROUP_WIDTH, POOL_GROUP_WIDTH),
                     lambda i: (layer, 0, 0, 0)),
        pl.BlockSpec((1, POOL_WIDTH), lambda i: (0, 0)),
        pl.BlockSpec((None, D_MODEL, D_MODEL), lambda i: (layer, 0, 0)),
    ]
    args = [h, attn, proj, w_pool, pool_scale, w_out]
    out_shape = [jax.ShapeDtypeStruct((t, D_MODEL), F32)]
    out_specs = [pl.BlockSpec((tm, D_MODEL), lambda i: (i, 0))]
    if cast_stacks is not None:
        cast_in, cast_out, cast_shapes = _cast_plumbing(cast_stacks, cast_layer, t // tm,
                                                        lambda i: i)
        in_specs += cast_in
        args += list(cast_stacks)
        out_shape += cast_shapes
        out_specs += cast_out
        est += 2 * 6 * 3 * D_MODEL * D_FF // (t // tm)
    outs = pl.pallas_call(
        _outproj_kernel,
        out_shape=out_shape,
        grid=(t // tm,),
        in_specs=in_specs,
        out_specs=out_specs,
        compiler_params=pltpu.CompilerParams(
            dimension_semantics=("arbitrary",),
            vmem_limit_bytes=_vmem_limit(est)),
        name="outproj",
    )(*args)
    return outs[0] if cast_stacks is None else outs


def _rope_tables(length):
    pos = jnp.arange(length, dtype=F32)
    inv_freq = 1.0 / (ROPE_THETA ** (jnp.arange(0, HEAD_DIM, 2, dtype=F32) / HEAD_DIM))
    ang = pos[:, None] * inv_freq[None, :]
    ang = jnp.concatenate([ang, ang, ang, ang], axis=-1)
    sign = jnp.where((jnp.arange(LANES) % HEAD_DIM) < HEAD_DIM // 2, -1.0, 1.0).astype(F32)
    cos, sin = jnp.cos(ang), jnp.sin(ang) * sign
    scale = HEAD_DIM ** -0.5 * math.log2(math.e)
    return cos * scale, sin * scale, cos, sin


def kernel(x, meta_tokens, ffn1_norm_g, ffn1_w_gate, ffn1_w_up, ffn1_w_down, mix_norm_g, w_in,
           lam_q1, lam_k1, lam_q2, lam_k2, subln_g, w_pool, pool_scale, w_out,
           ffn2_norm_g, ffn2_w_gate, ffn2_w_up, ffn2_w_down, final_norm_g):
    batch, seq, d = x.shape
    depth = w_in.shape[0]
    assert d == D_MODEL and meta_tokens.shape == (N_META, D_MODEL)

    tables = _rope_tables(N_META + seq)
    tab_meta = tuple(tb[:N_META] for tb in tables)
    tab_real = tuple(tb[N_META:] for tb in tables)
    final_g = final_norm_g.reshape(1, D_MODEL)

    h = x.reshape(batch * seq, D_MODEL)
    hm = meta_tokens.astype(x.dtype)

    wi = w_in.astype(BF16)
    wo = w_out.astype(BF16)
    wp = w_pool.astype(BF16)
    ffn1_stacks = (ffn1_w_gate, ffn1_w_up, ffn1_w_down)
    ffn2_stacks = (ffn2_w_gate, ffn2_w_up, ffn2_w_down)
    ffn1_w = _cast_ffn_weights(ffn1_stacks, 0)

    for layer in range(depth):
        lam_init = 0.8 - 0.6 * math.exp(-0.3 * layer)
        last = layer == depth - 1
        g1 = ffn1_norm_g[layer].reshape(1, D_MODEL)
        gm = mix_norm_g[layer].reshape(1, D_MODEL)
        g2 = ffn2_norm_g[layer].reshape(1, D_MODEL)
        ps = pool_scale[layer].reshape(1, POOL_WIDTH)
        sg = subln_g[layer].reshape(1, V_DIM)
        lam_params = tuple(p[layer].reshape(1, HEAD_DIM) for p in (lam_q1, lam_k1, lam_q2, lam_k2))

        hm = _ffn(hm, g1, *ffn1_w, final_g, layer=0, tm=N_META, tf=512, apply_final=False)
        proj_m, u_m = _inproj(hm, gm, wi, tab_meta, None, layer=layer, batch=1, tl=N_META, meta=True)
        attn_m = _attn_meta(lam_params, sg, proj_m, lam_init=lam_init)
        h = _ffn(h, g1, *ffn1_w, final_g, layer=0, tm=1024, tf=512, apply_final=False)
        proj, *ffn2_w = _inproj(h, gm, wi, tab_real, u_m, ffn2_stacks, layer=layer, batch=batch,
                                tl=512, meta=False)
        attn = _attn(lam_params, sg.reshape(V_DIM, 1), proj, proj_m, batch=batch,
                     lam_init=lam_init, tq=256, hp=8)

        if last:
            h = _outproj(h, attn, proj, wp, ps, wo, layer=layer, tm=512)
        else:
            h, *ffn1_w = _outproj(h, attn, proj, wp, ps, wo, ffn1_stacks, layer=layer,
                                  cast_layer=layer + 1, tm=512)
            hm = _outproj(hm, attn_m, proj_m, wp, ps, wo, layer=layer, tm=N_META)
            hm = _ffn(hm, g2, *ffn2_w, final_g, layer=0, tm=N_META, tf=512, apply_final=False)
        h = _ffn(h, g2, *ffn2_w, final_g, layer=0, tm=1024, tf=512, apply_final=last)

    return h.reshape(batch, seq, D_MODEL)
```

```python
import functools
import math

import jax
import jax.numpy as jnp
from jax import lax
from jax.experimental import pallas as pl
from jax.experimental.pallas import tpu as pltpu

F32 = jnp.float32
BF16 = jnp.bfloat16

D_MODEL = 2048
N_META = 16
ATTN_WIDTH = 1024
POOL_WIDTH = 1024
HEAD_DIM = 64
V_DIM = 2 * HEAD_DIM
N_HEADS = ATTN_WIDTH // V_DIM
POOL_WINDOWS = (2, 4, 8, 16)
POOL_GROUP_WIDTH = POOL_WIDTH // len(POOL_WINDOWS)
IN_WIDTH = 3 * ATTN_WIDTH + POOL_WIDTH
D_FF = 5632
ROPE_THETA = 10000.0
NORM_EPS = 1e-6
SUBLN_EPS = 1e-5

LANES = 128
ONES_ROWS = 16
V7X_VMEM_BYTES = 64 * 1024 * 1024


def _vmem_limit(estimate_bytes):
    return int(min(estimate_bytes * 5 // 4 + (4 << 20), V7X_VMEM_BYTES * 7 // 8))


def _rms_norm(x, g, eps):
    return x * lax.rsqrt(jnp.mean(x * x, axis=-1, keepdims=True) + eps) * g


def _ffn_kernel(x_ref, g_ref, wg_ref, wu_ref, wd_ref, fg_ref, o_ref, xn_ref, *, apply_final):
    f = pl.program_id(1)

    @pl.when(f == 0)
    def _():
        xn_ref[...] = _rms_norm(x_ref[...], g_ref[...], NORM_EPS).astype(BF16)

    xn = xn_ref[...]
    gate = jnp.dot(xn, wg_ref[...], preferred_element_type=F32)
    up = jnp.dot(xn, wu_ref[...], preferred_element_type=F32)
    act = (gate * jax.nn.sigmoid(gate)) * (0.5 * up)
    base = jnp.where(f == 0, x_ref[...], o_ref[...])
    o_ref[...] = base + jnp.dot(act.astype(BF16), wd_ref[...], preferred_element_type=F32)

    if apply_final:
        @pl.when(f == pl.num_programs(1) - 1)
        def _():
            o_ref[...] = _rms_norm(o_ref[...], fg_ref[...], NORM_EPS)


def _ffn(x, g, wg, wu, wd, final_g, *, layer, tm, tf, apply_final):
    t = x.shape[0]
    assert t % tm == 0 and D_FF % tf == 0
    est = (2 * 2 * tm * D_MODEL * 4 + tm * D_MODEL * 2 + 2 * 3 * D_MODEL * tf * 2
           + 3 * tm * tf * 4)
    return pl.pallas_call(
        functools.partial(_ffn_kernel, apply_final=apply_final),
        out_shape=jax.ShapeDtypeStruct((t, D_MODEL), F32),
        grid=(t // tm, D_FF // tf),
        in_specs=[
            pl.BlockSpec((tm, D_MODEL), lambda i, f: (i, 0)),
            pl.BlockSpec((1, D_MODEL), lambda i, f: (0, 0)),
            pl.BlockSpec((None, D_MODEL, tf), lambda i, f: (layer, 0, f)),
            pl.BlockSpec((None, D_MODEL, tf), lambda i, f: (layer, 0, f)),
            pl.BlockSpec((None, tf, D_MODEL), lambda i, f: (layer, f, 0)),
            pl.BlockSpec((1, D_MODEL), lambda i, f: (0, 0)),
        ],
        out_specs=pl.BlockSpec((tm, D_MODEL), lambda i, f: (i, 0)),
        scratch_shapes=[pltpu.VMEM((tm, D_MODEL), BF16)],
        compiler_params=pltpu.CompilerParams(
            dimension_semantics=("parallel", "arbitrary"),
            vmem_limit_bytes=_vmem_limit(est)),
        name="ffn",
    )(x, g, wg, wu, wd, final_g)


def _cast_plumbing(stacks, layer, n_steps, step_of):
    rows_gu = D_MODEL // n_steps
    rows_d = 2 * D_FF // n_steps
    assert D_MODEL % n_steps == 0 and (2 * D_FF) % n_steps == 0 and rows_d % 16 == 0

    def specs(lead):
        gu = pl.BlockSpec((None, rows_gu, D_FF), lambda *g: (lead, step_of(*g), 0))
        dn = pl.BlockSpec((None, rows_d, D_MODEL), lambda *g: (lead, step_of(*g) // 2, 0))
        return [gu, gu, dn]

    out_shapes = [jax.ShapeDtypeStruct((1,) + w.shape[1:], BF16) for w in stacks]
    return specs(layer), specs(0), out_shapes


def _cast_blocks(src_refs, dst_refs):
    for src, dst in zip(src_refs, dst_refs):
        dst[...] = src[...].astype(BF16)


def _cast_kernel(*refs):
    _cast_blocks(refs[:len(refs) // 2], refs[len(refs) // 2:])


def _cast_ffn_weights(stacks, layer, n_steps=32):
    in_specs, out_specs, out_shapes = _cast_plumbing(stacks, layer, n_steps, lambda i: i)
    est = 2 * 6 * 3 * D_MODEL * D_FF // n_steps
    return pl.pallas_call(
        _cast_kernel,
        out_shape=out_shapes,
        grid=(n_steps,),
        in_specs=in_specs,
        out_specs=out_specs,
        compiler_params=pltpu.CompilerParams(
            dimension_semantics=("arbitrary",),
            vmem_limit_bytes=_vmem_limit(est)),
        name="cast_ffn_weights",
    )(*stacks)


def _rope(x, cos, sin_signed):
    lane = lax.broadcasted_iota(jnp.int32, (x.shape[0], LANES), 1)
    upper = (lane % HEAD_DIM) >= (HEAD_DIM // 2)
    outs = []
    for c in range(x.shape[1] // LANES):
        xc = x[:, c * LANES:(c + 1) * LANES]
        from_below = pltpu.roll(xc, HEAD_DIM // 2, axis=1)
        from_above = pltpu.roll(xc, LANES - HEAD_DIM // 2, axis=1)
        outs.append(xc * cos + jnp.where(upper, from_below, from_above) * sin_signed)
    return jnp.concatenate(outs, axis=1)


def _pool_diff(hist, cur, inv_count_fn):
    run = jnp.concatenate([hist, cur], axis=0)
    outs = []
    for g, w in enumerate(POOL_WINDOWS):
        run = run + pltpu.roll(run, w // 2, axis=0)
        cols = slice(g * POOL_GROUP_WIDTH, (g + 1) * POOL_GROUP_WIDTH)
        outs.append(run[N_META:, :POOL_GROUP_WIDTH] * inv_count_fn(w) - cur[:, cols])
        run = run[:, POOL_GROUP_WIDTH:]
    return jnp.concatenate(outs, axis=1)


def _inproj_kernel(*refs, meta):
    if meta:
        (x_ref, g_ref, w_ref, cq_ref, sq_ref, ck_ref, sk_ref,
         o_ref, u_ref, hist_ref) = refs
    else:
        (x_ref, g_ref, w_ref, cq_ref, sq_ref, ck_ref, sk_ref, um_ref, *cast_src) = refs[:-5]
        o_ref, *cast_dst, hist_ref = refs[-5:]
        _cast_blocks(cast_src, cast_dst)
    j = pl.program_id(1)
    rows = x_ref.shape[0]
    chunk = lambda n: slice(n * ATTN_WIDTH, (n + 1) * ATTN_WIDTH)

    if not meta:
        @pl.when(j == 0)
        def _():
            hist_ref[...] = um_ref[...]

    if meta:
        hist = jnp.zeros((N_META, POOL_WIDTH), F32)
        pos = lax.broadcasted_iota(jnp.int32, (rows, 1), 0)

        def inv_count(w):
            return 1.0 / jnp.minimum(pos + 1, w).astype(F32)
    else:
        hist = hist_ref[...]

        def inv_count(w):
            return 1.0 / w

    sub = rows // 2 if rows >= 256 else rows
    for r in range(0, rows, sub):
        rs = slice(r, r + sub)
        xn = _rms_norm(x_ref[rs, :], g_ref[...], NORM_EPS).astype(BF16)

        def project(n):
            return jnp.dot(xn, w_ref[:, chunk(n)], preferred_element_type=F32)

        o_ref[rs, chunk(0)] = _rope(project(0), cq_ref[rs, :], sq_ref[rs, :]).astype(BF16)
        o_ref[rs, chunk(1)] = _rope(project(1), ck_ref[rs, :], sk_ref[rs, :]).astype(BF16)
        o_ref[rs, chunk(2)] = project(2).astype(BF16)
        acc = project(3)
        if meta:
            u_ref[rs, :] = acc
        o_ref[rs, chunk(3)] = _pool_diff(hist, acc, inv_count).astype(BF16)
        hist = acc[sub - N_META:, :]
    hist_ref[...] = hist


def _inproj(x, g, w_in, tables, u_meta, cast_stacks=None, *, layer, batch, tl, meta):
    t = x.shape[0]
    seq = t // batch
    nj = seq // tl
    assert seq % tl == 0
    cq, sq, ck, sk = tables
    tok = lambda b, j: (b * nj + j, 0)
    tab = lambda b, j: (j, 0)
    const = lambda b, j: (0, 0)
    in_specs = [
        pl.BlockSpec((tl, D_MODEL), tok),
        pl.BlockSpec((1, D_MODEL), const),
        pl.BlockSpec((None, D_MODEL, IN_WIDTH), lambda b, j: (layer, 0, 0),
                     pipeline_mode=pl.Buffered(1)),
        pl.BlockSpec((tl, LANES), tab),
        pl.BlockSpec((tl, LANES), tab),
        pl.BlockSpec((tl, LANES), tab),
        pl.BlockSpec((tl, LANES), tab),
    ]
    args = [x, g, w_in, cq, sq, ck, sk]
    proj_shape = jax.ShapeDtypeStruct((t, IN_WIDTH), BF16)
    proj_spec = pl.BlockSpec((tl, IN_WIDTH), tok)
    if meta:
        out_shape = (proj_shape, jax.ShapeDtypeStruct((t, POOL_WIDTH), F32))
        out_specs = (proj_spec, pl.BlockSpec((tl, POOL_WIDTH), const))
    else:
        cast_in, cast_out, cast_shapes = _cast_plumbing(cast_stacks, layer, batch * nj,
                                                        lambda b, j: b * nj + j)
        in_specs += [pl.BlockSpec((N_META, POOL_WIDTH), const)] + cast_in
        args += [u_meta] + list(cast_stacks)
        out_shape = [proj_shape] + cast_shapes
        out_specs = [proj_spec] + cast_out
    est = (2 * tl * D_MODEL * 4 + D_MODEL * IN_WIDTH * 2 + 2 * tl * IN_WIDTH * 2
           + tl * D_MODEL * 2 + 4 * (tl + N_META) * POOL_WIDTH * 4 + 8 * tl * LANES * 4
           + (0 if meta else 2 * 6 * 3 * D_MODEL * D_FF // (batch * nj)))
    return pl.pallas_call(
        functools.partial(_inproj_kernel, meta=meta),
        out_shape=out_shape,
        grid=(batch, nj),
        in_specs=in_specs,
        out_specs=out_specs,
        scratch_shapes=[pltpu.VMEM((N_META, POOL_WIDTH), F32)],
        compiler_params=pltpu.CompilerParams(
            dimension_semantics=("arbitrary", "arbitrary"),
            vmem_limit_bytes=_vmem_limit(est)),
        name="inproj_meta" if meta else "inproj",
    )(*args)


def _stack_queries(q):
    lane = lax.broadcasted_iota(jnp.int32, q.shape, 1)
    zero = jnp.zeros_like(q)
    return jnp.concatenate([jnp.where(lane < HEAD_DIM, q, zero),
                            jnp.where(lane >= HEAD_DIM, q, zero)], axis=0)


def _scores(qq, k):
    return lax.dot_general(qq, k, (((1,), (1,)), ((), ())), preferred_element_type=F32)


def _lambda(lq1_ref, lk1_ref, lq2_ref, lk2_ref, lam_init):
    s1 = jnp.sum(lq1_ref[...] * lk1_ref[...], axis=-1, keepdims=True)
    s2 = jnp.sum(lq2_ref[...] * lk2_ref[...], axis=-1, keepdims=True)
    return jnp.exp(s1) - jnp.exp(s2) + lam_init


def _finish(acc, l, lam, g, lam_init, tq):
    o = acc[:tq] / l[:tq] - lam * (acc[tq:] / l[tq:])
    return _rms_norm(o, g, SUBLN_EPS) * (1.0 - lam_init)


def _attn_kernel(lq1_ref, lk1_ref, lq2_ref, lk2_ref, g_ref, q_ref, k_ref, v_ref, km_ref, vmt_ref,
                 o_ref, qqt_ref, vt_ref, m_ref, acc_ref, s0_ref, s1_ref,
                 *, lam_init, tq, hp):
    nq = q_ref.shape[0] // tq
    lam = _lambda(lq1_ref, lk1_ref, lq2_ref, lk2_ref, lam_init)
    heads = [slice(h * V_DIM, (h + 1) * V_DIM) for h in range(hp)]

    def block_rows(b):
        return pl.ds(pl.multiple_of(b * tq, tq), tq)

    qqt_ref[...] = jnp.zeros(qqt_ref.shape, BF16)
    acc_ref[1] = jnp.ones(acc_ref.shape[1:], F32)
    ones_rows = jnp.ones((ONES_ROWS, tq), BF16)
    for h, hs in enumerate(heads):
        for c in range(nq):
            vt_ref[h, c, :V_DIM, :] = v_ref[c * tq:(c + 1) * tq, hs].T
            vt_ref[h, c, V_DIM:, :] = ones_rows

    def scores(k, h):
        return jnp.dot(k, qqt_ref[h], preferred_element_type=F32)

    def consume(slot, h, s, vt, first):
        blk_max = jnp.max(s, axis=0, keepdims=True)
        m_new = blk_max if first else jnp.maximum(m_ref[h], blk_max)
        p = jnp.exp2(s - m_new)
        pv = jnp.dot(vt, p.astype(BF16), preferred_element_type=F32)
        if first:
            acc_ref[slot, h] = pv
        else:
            acc_ref[slot, h] = jnp.exp2(m_ref[h] - m_new) * acc_ref[slot, h] + pv
        m_ref[h] = m_new

    def finish(slot, rows, h, hs):
        acc = acc_ref[slot, h, :V_DIM, :]
        inv_l = 1.0 / acc_ref[slot, h, V_DIM:V_DIM + 1, :]
        o = acc[:, :tq] * inv_l[:, :tq] - acc[:, tq:] * (lam * inv_l[:, tq:])
        o = o * lax.rsqrt(jnp.mean(o * o, axis=0, keepdims=True) + SUBLN_EPS)
        o_ref[rows, hs] = (o * (g_ref[...] * (1.0 - lam_init))).T.astype(BF16)

    def interleave(producers, consumers, lead, per_consumer):
        producers = list(producers)
        for task in producers[:lead]:
            task()
        rest = producers[lead:]
        for c, task in enumerate(consumers):
            task()
            for extra in rest[c * per_consumer:(c + 1) * per_consumer]:
                extra()
        for extra in rest[len(consumers) * per_consumer:]:
            extra()

    def step(slot, b, cur_ref, nxt_ref):
        def prefetch(h, hs):
            def run():
                nxt_ref[h] = scores(k_ref[block_rows(b + 1), hs], h)
            return run

        def use(h):
            return lambda: consume(slot, h, cur_ref[h], vt_ref[h, b], False)

        producers = [prefetch(h, hs) for h, hs in enumerate(heads)] if nxt_ref is not None else []
        interleave(producers, [use(h) for h in range(hp)], 2, 1)

    def q_tile(i, _):
        rows = block_rows(i)
        slot = i % 2
        for h, hs in enumerate(heads):
            qt = q_ref[rows, hs].T
            qqt_ref[h, :HEAD_DIM, :tq] = qt[:HEAD_DIM]
            qqt_ref[h, HEAD_DIM:, tq:] = qt[HEAD_DIM:]

        prev_rows = block_rows(jnp.maximum(i - 1, 0))

        key = lax.broadcasted_iota(jnp.int32, (tq + N_META, 2 * tq), 0)
        qry = lax.broadcasted_iota(jnp.int32, (tq + N_META, 2 * tq), 1)
        visible = (key >= tq) | (key <= jnp.where(qry >= tq, qry - tq, qry))
        diag = [None] * hp

        def diag_scores(h, hs):
            def run():
                diag[h] = scores(jnp.concatenate([k_ref[rows, hs], km_ref[:, hs]], axis=0), h)
            return run

        def first_scores(h, hs):
            def run():
                s0_ref[h] = scores(k_ref[block_rows(0), hs], h)
            return run

        def use_diag(h, hs):
            def run():
                consume(slot, h, jnp.where(visible, diag[h], jnp.finfo(F32).min),
                        jnp.concatenate([vt_ref[h, i], vmt_ref[h]], axis=1), True)
                finish(1 - slot, prev_rows, h, hs)
            return run

        producers = [diag_scores(h, hs) for h, hs in enumerate(heads)]
        producers = producers[:2] + [t for h, hs in enumerate(heads)
                                     for t in (producers[h + 2:h + 3] + [first_scores(h, hs)])]
        interleave(producers, [use_diag(h, hs) for h, hs in enumerate(heads)], 2, 2)

        def body(b, _):
            @pl.when(b % 2 == 0)
            def _():
                step(slot, b, s0_ref, s1_ref)

            @pl.when(b % 2 == 1)
            def _():
                step(slot, b, s1_ref, s0_ref)
            return 0

        lax.fori_loop(0, i - 1, body, 0)

        @pl.when((i >= 1) & (i % 2 == 1))
        def _():
            step(slot, i - 1, s0_ref, None)

        @pl.when((i >= 1) & (i % 2 == 0))
        def _():
            step(slot, i - 1, s1_ref, None)

        return 0

    lax.fori_loop(0, nq, q_tile, 0)
    for h, hs in enumerate(heads):
        finish((nq - 1) % 2, pl.ds((nq - 1) * tq, tq), h, hs)


def _attn(lam_params, subln_g_col, proj, proj_meta, *, batch, lam_init, tq, hp):
    t = proj.shape[0]
    seq = t // batch
    assert seq % tq == 0 and N_HEADS % hp == 0
    groups = N_HEADS // hp
    width = hp * V_DIM
    small = lambda b, g: (0, 0)
    lam_specs = [pl.BlockSpec((1, HEAD_DIM), small)] * 4
    score_buf = pltpu.VMEM((hp, tq, 2 * tq), F32)
    v_meta_t = proj_meta[:, 2 * ATTN_WIDTH:3 * ATTN_WIDTH].T.reshape(N_HEADS, V_DIM, N_META)
    v_meta_t = jnp.concatenate([v_meta_t, jnp.ones((N_HEADS, ONES_ROWS, N_META), BF16)], axis=1)
    est = (2 * 4 * seq * width * 2 + seq * width * 2
           + hp * (2 * tq * V_DIM * 2 + 2 * tq * 2 * tq * 4 + 8 * tq * 2 * tq * 4 // hp
                   + 2 * V_DIM * 2 * tq * 4))
    return pl.pallas_call(
        functools.partial(_attn_kernel, lam_init=lam_init, tq=tq, hp=hp),
        out_shape=jax.ShapeDtypeStruct((t, ATTN_WIDTH), BF16),
        grid=(batch, groups),
        in_specs=lam_specs + [
            pl.BlockSpec((V_DIM, 1), small),
            pl.BlockSpec((seq, width), lambda b, g: (b, g)),
            pl.BlockSpec((seq, width), lambda b, g: (b, groups + g)),
            pl.BlockSpec((seq, width), lambda b, g: (b, 2 * groups + g)),
            pl.BlockSpec((N_META, width), lambda b, g: (0, groups + g)),
            pl.BlockSpec((hp, V_DIM + ONES_ROWS, N_META), lambda b, g: (g, 0, 0)),
        ],
        out_specs=pl.BlockSpec((seq, width), lambda b, g: (b, g)),
        scratch_shapes=[pltpu.VMEM((hp, V_DIM, 2 * tq), BF16),
                        pltpu.VMEM((hp, seq // tq, V_DIM + ONES_ROWS, tq), BF16),
                        pltpu.VMEM((hp, 1, 2 * tq), F32),
                        pltpu.VMEM((2, hp, V_DIM + ONES_ROWS, 2 * tq), F32), score_buf, score_buf],
        compiler_params=pltpu.CompilerParams(
            dimension_semantics=("parallel", "parallel"),
            vmem_limit_bytes=_vmem_limit(est)),
        name="attn",
    )(*lam_params, subln_g_col, proj, proj, proj, proj_meta, v_meta_t)


def _attn_meta_kernel(lq1_ref, lk1_ref, lq2_ref, lk2_ref, g_ref, q_ref, k_ref, v_ref, o_ref,
                      *, lam_init):
    qq = _stack_queries(q_ref[...])
    s = _scores(qq, k_ref[...])
    row = lax.broadcasted_iota(jnp.int32, s.shape, 0) % N_META
    col = lax.broadcasted_iota(jnp.int32, s.shape, 1)
    s = jnp.where(col <= row, s, jnp.finfo(F32).min)
    m = jnp.max(s, axis=-1, keepdims=True)
    p = jnp.exp2(s - m)
    l = jnp.sum(p, axis=-1, keepdims=True)
    acc = jnp.dot(p.astype(BF16), v_ref[...], preferred_element_type=F32)
    lam = _lambda(lq1_ref, lk1_ref, lq2_ref, lk2_ref, lam_init)
    o_ref[...] = _finish(acc, l, lam, g_ref[...], lam_init, N_META).astype(BF16)


def _attn_meta(lam_params, subln_g, proj_meta, *, lam_init):
    small = lambda h: (0, 0)
    return pl.pallas_call(
        functools.partial(_attn_meta_kernel, lam_init=lam_init),
        out_shape=jax.ShapeDtypeStruct((N_META, ATTN_WIDTH), BF16),
        grid=(N_HEADS,),
        in_specs=[pl.BlockSpec((1, HEAD_DIM), small)] * 4 + [
            pl.BlockSpec((1, V_DIM), small),
            pl.BlockSpec((N_META, V_DIM), lambda h: (0, h)),
            pl.BlockSpec((N_META, V_DIM), lambda h: (0, N_HEADS + h)),
            pl.BlockSpec((N_META, V_DIM), lambda h: (0, 2 * N_HEADS + h)),
        ],
        out_specs=pl.BlockSpec((N_META, V_DIM), lambda h: (0, h)),
        compiler_params=pltpu.CompilerParams(dimension_semantics=("arbitrary",)),
        name="attn_meta",
    )(*lam_params, subln_g, proj_meta, proj_meta, proj_meta)


def _outproj_kernel(h_ref, a_ref, d_ref, wp_ref, ps_ref, wo_ref, *rest):
    cast_src, (o_ref, *cast_dst) = rest[:len(rest) // 2], rest[len(rest) // 2:]
    _cast_blocks(cast_src, cast_dst)
    diff = d_ref[...]
    pools = []
    for g in range(len(POOL_WINDOWS)):
        cols = slice(g * POOL_GROUP_WIDTH, (g + 1) * POOL_GROUP_WIDTH)
        pools.append(jnp.dot(diff[:, cols], wp_ref[g], preferred_element_type=F32))
    pool = (jnp.concatenate(pools, axis=1) * ps_ref[...]).astype(BF16)
    mix = jnp.concatenate([a_ref[...], pool], axis=1)
    o_ref[...] = h_ref[...] + jnp.dot(mix, wo_ref[...], preferred_element_type=F32)


def _outproj(h, attn, proj, w_pool, pool_scale, w_out, cast_stacks=None, *, layer, cast_layer=0,
             tm):
    t = h.shape[0]
    assert t % tm == 0
    est = (2 * 2 * tm * D_MODEL * 4 + 2 * 2 * tm * ATTN_WIDTH * 2 + 2 * D_MODEL * D_MODEL * 2
           + 2 * POOL_WIDTH * POOL_GROUP_WIDTH * 2 + 3 * tm * D_MODEL * 4)
    in_specs = [
        pl.BlockSpec((tm, D_MODEL), lambda i: (i, 0)),
        pl.BlockSpec((tm, ATTN_WIDTH), lambda i: (i, 0)),
        pl.BlockSpec((tm, POOL_WIDTH), lambda i: (i, 3)),
        pl.BlockSpec((None, len(POOL_WINDOWS), POOL_GROUP_WIDTH, POOL_GROUP_WIDTH),
                     lambda i: (layer, 0, 0, 0)),
        pl.BlockSpec((1, POOL_WIDTH), lambda i: (0, 0)),
        pl.BlockSpec((None, D_MODEL, D_MODEL), lambda i: (layer, 0, 0)),
    ]
    args = [h, attn, proj, w_pool, pool_scale, w_out]
    out_shape = [jax.ShapeDtypeStruct((t, D_MODEL), F32)]
    out_specs = [pl.BlockSpec((tm, D_MODEL), lambda i: (i, 0))]
    if cast_stacks is not None:
        cast_in, cast_out, cast_shapes = _cast_plumbing(cast_stacks, cast_layer, t // tm,
                                                        lambda i: i)
        in_specs += cast_in
        args += list(cast_stacks)
        out_shape += cast_shapes
        out_specs += cast_out
        est += 2 * 6 * 3 * D_MODEL * D_FF // (t // tm)
    outs = pl.pallas_call(
        _outproj_kernel,
        out_shape=out_shape,
        grid=(t // tm,),
        in_specs=in_specs,
        out_specs=out_specs,
        compiler_params=pltpu.CompilerParams(
            dimension_semantics=("arbitrary",),
            vmem_limit_bytes=_vmem_limit(est)),
        name="outproj",
    )(*args)
    return outs[0] if cast_stacks is None else outs


def _rope_tables(length):
    pos = jnp.arange(length, dtype=F32)
    inv_freq = 1.0 / (ROPE_THETA ** (jnp.arange(0, HEAD_DIM, 2, dtype=F32) / HEAD_DIM))
    ang = pos[:, None] * inv_freq[None, :]
    ang = jnp.concatenate([ang, ang, ang, ang], axis=-1)
    sign = jnp.where((jnp.arange(LANES) % HEAD_DIM) < HEAD_DIM // 2, -1.0, 1.0).astype(F32)
    cos, sin = jnp.cos(ang), jnp.sin(ang) * sign
    scale = HEAD_DIM ** -0.5 * math.log2(math.e)
    return cos * scale, sin * scale, cos, sin


def kernel(x, meta_tokens, ffn1_norm_g, ffn1_w_gate, ffn1_w_up, ffn1_w_down, mix_norm_g, w_in,
           lam_q1, lam_k1, lam_q2, lam_k2, subln_g, w_pool, pool_scale, w_out,
           ffn2_norm_g, ffn2_w_gate, ffn2_w_up, ffn2_w_down, final_norm_g):
    batch, seq, d = x.shape
    depth = w_in.shape[0]
    assert d == D_MODEL and meta_tokens.shape == (N_META, D_MODEL)

    tables = _rope_tables(N_META + seq)
    tab_meta = tuple(tb[:N_META] for tb in tables)
    tab_real = tuple(tb[N_META:] for tb in tables)
    final_g = final_norm_g.reshape(1, D_MODEL)

    h = x.reshape(batch * seq, D_MODEL)
    hm = meta_tokens.astype(x.dtype)

    wi = w_in.astype(BF16)
    wo = w_out.astype(BF16)
    wp = w_pool.astype(BF16)
    ffn1_stacks = (ffn1_w_gate, ffn1_w_up, ffn1_w_down)
    ffn2_stacks = (ffn2_w_gate, ffn2_w_up, ffn2_w_down)
    ffn1_w = _cast_ffn_weights(ffn1_stacks, 0)

    for layer in range(depth):
        lam_init = 0.8 - 0.6 * math.exp(-0.3 * layer)
        last = layer == depth - 1
        g1 = ffn1_norm_g[layer].reshape(1, D_MODEL)
        gm = mix_norm_g[layer].reshape(1, D_MODEL)
        g2 = ffn2_norm_g[layer].reshape(1, D_MODEL)
        ps = pool_scale[layer].reshape(1, POOL_WIDTH)
        sg = subln_g[layer].reshape(1, V_DIM)
        lam_params = tuple(p[layer].reshape(1, HEAD_DIM) for p in (lam_q1, lam_k1, lam_q2, lam_k2))

        hm = _ffn(hm, g1, *ffn1_w, final_g, layer=0, tm=N_META, tf=512, apply_final=False)
        proj_m, u_m = _inproj(hm, gm, wi, tab_meta, None, layer=layer, batch=1, tl=N_META, meta=True)
        attn_m = _attn_meta(lam_params, sg, proj_m, lam_init=lam_init)
        h = _ffn(h, g1, *ffn1_w, final_g, layer=0, tm=1024, tf=512, apply_final=False)
        proj, *ffn2_w = _inproj(h, gm, wi, tab_real, u_m, ffn2_stacks, layer=layer, batch=batch,
                                tl=512, meta=False)
        attn = _attn(lam_params, sg.reshape(V_DIM, 1), proj, proj_m, batch=batch,
                     lam_init=lam_init, tq=256, hp=8)

        if last:
            h = _outproj(h, attn, proj, wp, ps, wo, layer=layer, tm=512)
        else:
            h, *ffn1_w = _outproj(h, attn, proj, wp, ps, wo, ffn1_stacks, layer=layer,
                                  cast_layer=layer + 1, tm=512)
            hm = _outproj(hm, attn_m, proj_m, wp, ps, wo, layer=layer, tm=N_META)
            hm = _ffn(hm, g2, *ffn2_w, final_g, layer=0, tm=N_META, tf=512, apply_final=False)
        h = _ffn(h, g2, *ffn2_w, final_g, layer=0, tm=1024, tf=512, apply_final=last)

    return h.reshape(batch, seq, D_MODEL)
```

```python
import functools
import math

import jax
import jax.numpy as jnp
from jax import lax
from jax.experimental import pallas as pl
from jax.experimental.pallas import tpu as pltpu

F32 = jnp.float32
BF16 = jnp.bfloat16

D_MODEL = 2048
N_META = 16
ATTN_WIDTH = 1024
POOL_WIDTH = 1024
HEAD_DIM = 64
V_DIM = 2 * HEAD_DIM
N_HEADS = ATTN_WIDTH // V_DIM
POOL_WINDOWS = (2, 4, 8, 16)
POOL_GROUP_WIDTH = POOL_WIDTH // len(POOL_WINDOWS)
IN_WIDTH = 3 * ATTN_WIDTH + POOL_WIDTH
D_FF = 5632
ROPE_THETA = 10000.0
NORM_EPS = 1e-6
SUBLN_EPS = 1e-5

LANES = 128
ONES_ROWS = 16
V7X_VMEM_BYTES = 64 * 1024 * 1024


def _vmem_limit(estimate_bytes):
    return int(min(estimate_bytes * 5 // 4 + (4 << 20), V7X_VMEM_BYTES * 7 // 8))


def _rms_norm(x, g, eps):
    return x * lax.rsqrt(jnp.mean(x * x, axis=-1, keepdims=True) + eps) * g


def _ffn_kernel(x_ref, g_ref, wg_ref, wu_ref, wd_ref, fg_ref, o_ref, xn_ref, *, apply_final):
    f = pl.program_id(1)

    @pl.when(f == 0)
    def _():
        x = x_ref[...]
        xn_ref[...] = _rms_norm(x, g_ref[...], NORM_EPS).astype(BF16)
        o_ref[...] = x

    xn = xn_ref[...]
    gate = jnp.dot(xn, wg_ref[...], preferred_element_type=F32)
    up = jnp.dot(xn, wu_ref[...], preferred_element_type=F32)
    act = (gate * jax.nn.sigmoid(gate)) * (0.5 * up)
    o_ref[...] += jnp.dot(act.astype(BF16), wd_ref[...], preferred_element_type=F32)

    if apply_final:
        @pl.when(f == pl.num_programs(1) - 1)
        def _():
            o_ref[...] = _rms_norm(o_ref[...], fg_ref[...], NORM_EPS)


def _ffn(x, g, wg, wu, wd, final_g, *, layer, tm, tf, apply_final):
    t = x.shape[0]
    assert t % tm == 0 and D_FF % tf == 0
    est = (2 * 2 * tm * D_MODEL * 4 + tm * D_MODEL * 2 + 2 * 3 * D_MODEL * tf * 2
           + 3 * tm * tf * 4)
    return pl.pallas_call(
        functools.partial(_ffn_kernel, apply_final=apply_final),
        out_shape=jax.ShapeDtypeStruct((t, D_MODEL), F32),
        grid=(t // tm, D_FF // tf),
        in_specs=[
            pl.BlockSpec((tm, D_MODEL), lambda i, f: (i, 0)),
            pl.BlockSpec((1, D_MODEL), lambda i, f: (0, 0)),
            pl.BlockSpec((None, D_MODEL, tf), lambda i, f: (layer, 0, f)),
            pl.BlockSpec((None, D_MODEL, tf), lambda i, f: (layer, 0, f)),
            pl.BlockSpec((None, tf, D_MODEL), lambda i, f: (layer, f, 0)),
            pl.BlockSpec((1, D_MODEL), lambda i, f: (0, 0)),
        ],
        out_specs=pl.BlockSpec((tm, D_MODEL), lambda i, f: (i, 0)),
        scratch_shapes=[pltpu.VMEM((tm, D_MODEL), BF16)],
        compiler_params=pltpu.CompilerParams(
            dimension_semantics=("parallel", "arbitrary"),
            vmem_limit_bytes=_vmem_limit(est)),
        name="ffn",
    )(x, g, wg, wu, wd, final_g)


def _cast_plumbing(stacks, layer, n_steps, step_of):
    rows_gu = D_MODEL // n_steps
    rows_d = 2 * D_FF // n_steps
    assert D_MODEL % n_steps == 0 and (2 * D_FF) % n_steps == 0 and rows_d % 16 == 0

    def specs(lead):
        gu = pl.BlockSpec((None, rows_gu, D_FF), lambda *g: (lead, step_of(*g), 0))
        dn = pl.BlockSpec((None, rows_d, D_MODEL), lambda *g: (lead, step_of(*g) // 2, 0))
        return [gu, gu, dn]

    out_shapes = [jax.ShapeDtypeStruct((1,) + w.shape[1:], BF16) for w in stacks]
    return specs(layer), specs(0), out_shapes


def _cast_blocks(src_refs, dst_refs):
    for src, dst in zip(src_refs, dst_refs):
        dst[...] = src[...].astype(BF16)


def _cast_kernel(*refs):
    _cast_blocks(refs[:len(refs) // 2], refs[len(refs) // 2:])


def _cast_ffn_weights(stacks, layer, n_steps=32):
    in_specs, out_specs, out_shapes = _cast_plumbing(stacks, layer, n_steps, lambda i: i)
    est = 2 * 6 * 3 * D_MODEL * D_FF // n_steps
    return pl.pallas_call(
        _cast_kernel,
        out_shape=out_shapes,
        grid=(n_steps,),
        in_specs=in_specs,
        out_specs=out_specs,
        compiler_params=pltpu.CompilerParams(
            dimension_semantics=("arbitrary",),
            vmem_limit_bytes=_vmem_limit(est)),
        name="cast_ffn_weights",
    )(*stacks)


def _rope(x, cos, sin_signed):
    lane = lax.broadcasted_iota(jnp.int32, (x.shape[0], LANES), 1)
    upper = (lane % HEAD_DIM) >= (HEAD_DIM // 2)
    outs = []
    for c in range(x.shape[1] // LANES):
        xc = x[:, c * LANES:(c + 1) * LANES]
        from_below = pltpu.roll(xc, HEAD_DIM // 2, axis=1)
        from_above = pltpu.roll(xc, LANES - HEAD_DIM // 2, axis=1)
        outs.append(xc * cos + jnp.where(upper, from_below, from_above) * sin_signed)
    return jnp.concatenate(outs, axis=1)


def _pool_diff(hist, cur, inv_count_fn):
    run = jnp.concatenate([hist, cur], axis=0)
    outs = []
    for g, w in enumerate(POOL_WINDOWS):
        run = run + pltpu.roll(run, w // 2, axis=0)
        cols = slice(g * POOL_GROUP_WIDTH, (g + 1) * POOL_GROUP_WIDTH)
        outs.append(run[N_META:, :POOL_GROUP_WIDTH] * inv_count_fn(w) - cur[:, cols])
        run = run[:, POOL_GROUP_WIDTH:]
    return jnp.concatenate(outs, axis=1)


def _inproj_kernel(*refs, meta):
    if meta:
        (x_ref, g_ref, w_ref, cq_ref, sq_ref, ck_ref, sk_ref,
         o_ref, u_ref, hist_ref) = refs
    else:
        (x_ref, g_ref, w_ref, cq_ref, sq_ref, ck_ref, sk_ref, um_ref, *cast_src) = refs[:-5]
        o_ref, *cast_dst, hist_ref = refs[-5:]
        _cast_blocks(cast_src, cast_dst)
    j = pl.program_id(1)
    rows = x_ref.shape[0]
    chunk = lambda n: slice(n * ATTN_WIDTH, (n + 1) * ATTN_WIDTH)

    if not meta:
        @pl.when(j == 0)
        def _():
            hist_ref[...] = um_ref[...]

    xn = _rms_norm(x_ref[...], g_ref[...], NORM_EPS).astype(BF16)

    def project(n):
        return jnp.dot(xn, w_ref[:, chunk(n)], preferred_element_type=F32)

    o_ref[:, chunk(0)] = _rope(project(0), cq_ref[...], sq_ref[...]).astype(BF16)
    o_ref[:, chunk(1)] = _rope(project(1), ck_ref[...], sk_ref[...]).astype(BF16)
    o_ref[:, chunk(2)] = project(2).astype(BF16)

    acc = project(3)
    if meta:
        u_ref[...] = acc
        hist = jnp.zeros((N_META, POOL_WIDTH), F32)
        pos = lax.broadcasted_iota(jnp.int32, (rows, 1), 0)

        def inv_count(w):
            return 1.0 / jnp.minimum(pos + 1, w).astype(F32)
    else:
        hist = hist_ref[...]

        def inv_count(w):
            return 1.0 / w
    o_ref[:, chunk(3)] = _pool_diff(hist, acc, inv_count).astype(BF16)
    hist_ref[...] = acc[rows - N_META:, :]


def _inproj(x, g, w_in, tables, u_meta, cast_stacks=None, *, layer, batch, tl, meta):
    t = x.shape[0]
    seq = t // batch
    nj = seq // tl
    assert seq % tl == 0
    cq, sq, ck, sk = tables
    tok = lambda b, j: (b * nj + j, 0)
    tab = lambda b, j: (j, 0)
    const = lambda b, j: (0, 0)
    in_specs = [
        pl.BlockSpec((tl, D_MODEL), tok),
        pl.BlockSpec((1, D_MODEL), const),
        pl.BlockSpec((None, D_MODEL, IN_WIDTH), lambda b, j: (layer, 0, 0),
                     pipeline_mode=pl.Buffered(1)),
        pl.BlockSpec((tl, LANES), tab),
        pl.BlockSpec((tl, LANES), tab),
        pl.BlockSpec((tl, LANES), tab),
        pl.BlockSpec((tl, LANES), tab),
    ]
    args = [x, g, w_in, cq, sq, ck, sk]
    proj_shape = jax.ShapeDtypeStruct((t, IN_WIDTH), BF16)
    proj_spec = pl.BlockSpec((tl, IN_WIDTH), tok)
    if meta:
        out_shape = (proj_shape, jax.ShapeDtypeStruct((t, POOL_WIDTH), F32))
        out_specs = (proj_spec, pl.BlockSpec((tl, POOL_WIDTH), const))
    else:
        cast_in, cast_out, cast_shapes = _cast_plumbing(cast_stacks, layer, batch * nj,
                                                        lambda b, j: b * nj + j)
        in_specs += [pl.BlockSpec((N_META, POOL_WIDTH), const)] + cast_in
        args += [u_meta] + list(cast_stacks)
        out_shape = [proj_shape] + cast_shapes
        out_specs = [proj_spec] + cast_out
    est = (2 * tl * D_MODEL * 4 + D_MODEL * IN_WIDTH * 2 + 2 * tl * IN_WIDTH * 2
           + tl * D_MODEL * 2 + 4 * (tl + N_META) * POOL_WIDTH * 4 + 8 * tl * LANES * 4
           + (0 if meta else 2 * 6 * 3 * D_MODEL * D_FF // (batch * nj)))
    return pl.pallas_call(
        functools.partial(_inproj_kernel, meta=meta),
        out_shape=out_shape,
        grid=(batch, nj),
        in_specs=in_specs,
        out_specs=out_specs,
        scratch_shapes=[pltpu.VMEM((N_META, POOL_WIDTH), F32)],
        compiler_params=pltpu.CompilerParams(
            dimension_semantics=("arbitrary", "arbitrary"),
            vmem_limit_bytes=_vmem_limit(est)),
        name="inproj_meta" if meta else "inproj",
    )(*args)


def _stack_queries(q):
    lane = lax.broadcasted_iota(jnp.int32, q.shape, 1)
    zero = jnp.zeros_like(q)
    return jnp.concatenate([jnp.where(lane < HEAD_DIM, q, zero),
                            jnp.where(lane >= HEAD_DIM, q, zero)], axis=0)


def _scores(qq, k):
    return lax.dot_general(qq, k, (((1,), (1,)), ((), ())), preferred_element_type=F32)


def _lambda(lq1_ref, lk1_ref, lq2_ref, lk2_ref, lam_init):
    s1 = jnp.sum(lq1_ref[...] * lk1_ref[...], axis=-1, keepdims=True)
    s2 = jnp.sum(lq2_ref[...] * lk2_ref[...], axis=-1, keepdims=True)
    return jnp.exp(s1) - jnp.exp(s2) + lam_init


def _finish(acc, l, lam, g, lam_init, tq):
    o = acc[:tq] / l[:tq] - lam * (acc[tq:] / l[tq:])
    return _rms_norm(o, g, SUBLN_EPS) * (1.0 - lam_init)


def _attn_kernel(lq1_ref, lk1_ref, lq2_ref, lk2_ref, g_ref, q_ref, k_ref, v_ref, km_ref, vmt_ref,
                 o_ref, qqt_ref, vt_ref, m_ref, acc_ref, s0_ref, s1_ref,
                 *, lam_init, tq, hp):
    nq = q_ref.shape[0] // tq
    lam = _lambda(lq1_ref, lk1_ref, lq2_ref, lk2_ref, lam_init)
    heads = [slice(h * V_DIM, (h + 1) * V_DIM) for h in range(hp)]

    def block_rows(b):
        return pl.ds(pl.multiple_of(b * tq, tq), tq)

    qqt_ref[...] = jnp.zeros(qqt_ref.shape, BF16)
    acc_ref[1] = jnp.ones(acc_ref.shape[1:], F32)
    ones_rows = jnp.ones((ONES_ROWS, tq), BF16)
    for h, hs in enumerate(heads):
        for c in range(nq):
            vt_ref[h, c, :V_DIM, :] = v_ref[c * tq:(c + 1) * tq, hs].T
            vt_ref[h, c, V_DIM:, :] = ones_rows

    def scores(k, h):
        return jnp.dot(k, qqt_ref[h], preferred_element_type=F32)

    def consume(slot, h, s, vt, first):
        blk_max = jnp.max(s, axis=0, keepdims=True)
        m_new = blk_max if first else jnp.maximum(m_ref[h], blk_max)
        p = jnp.exp2(s - m_new)
        pv = jnp.dot(vt, p.astype(BF16), preferred_element_type=F32)
        if first:
            acc_ref[slot, h] = pv
        else:
            acc_ref[slot, h] = jnp.exp2(m_ref[h] - m_new) * acc_ref[slot, h] + pv
        m_ref[h] = m_new

    def finish(slot, rows, h, hs):
        acc = acc_ref[slot, h, :V_DIM, :]
        inv_l = 1.0 / acc_ref[slot, h, V_DIM:V_DIM + 1, :]
        o = acc[:, :tq] * inv_l[:, :tq] - acc[:, tq:] * (lam * inv_l[:, tq:])
        o = o * lax.rsqrt(jnp.mean(o * o, axis=0, keepdims=True) + SUBLN_EPS)
        o_ref[rows, hs] = (o * (g_ref[...] * (1.0 - lam_init))).T.astype(BF16)

    def interleave(producers, consumers, lead, per_consumer):
        producers = list(producers)
        for task in producers[:lead]:
            task()
        rest = producers[lead:]
        for c, task in enumerate(consumers):
            task()
            for extra in rest[c * per_consumer:(c + 1) * per_consumer]:
                extra()
        for extra in rest[len(consumers) * per_consumer:]:
            extra()

    def step(slot, b, cur_ref, nxt_ref):
        def prefetch(h, hs):
            def run():
                nxt_ref[h] = scores(k_ref[block_rows(b + 1), hs], h)
            return run

        def use(h):
            return lambda: consume(slot, h, cur_ref[h], vt_ref[h, b], False)

        producers = [prefetch(h, hs) for h, hs in enumerate(heads)] if nxt_ref is not None else []
        interleave(producers, [use(h) for h in range(hp)], 2, 1)

    def q_tile(i, _):
        rows = block_rows(i)
        slot = i % 2
        for h, hs in enumerate(heads):
            qt = q_ref[rows, hs].T
            qqt_ref[h, :HEAD_DIM, :tq] = qt[:HEAD_DIM]
            qqt_ref[h, HEAD_DIM:, tq:] = qt[HEAD_DIM:]

        prev_rows = block_rows(jnp.maximum(i - 1, 0))

        key = lax.broadcasted_iota(jnp.int32, (tq + N_META, 2 * tq), 0)
        qry = lax.broadcasted_iota(jnp.int32, (tq + N_META, 2 * tq), 1)
        visible = (key >= tq) | (key <= jnp.where(qry >= tq, qry - tq, qry))
        diag = [None] * hp

        def diag_scores(h, hs):
            def run():
                diag[h] = scores(jnp.concatenate([k_ref[rows, hs], km_ref[:, hs]], axis=0), h)
            return run

        def first_scores(h, hs):
            def run():
                s0_ref[h] = scores(k_ref[block_rows(0), hs], h)
            return run

        def use_diag(h, hs):
            def run():
                consume(slot, h, jnp.where(visible, diag[h], jnp.finfo(F32).min),
                        jnp.concatenate([vt_ref[h, i], vmt_ref[h]], axis=1), True)
                finish(1 - slot, prev_rows, h, hs)
            return run

        producers = [diag_scores(h, hs) for h, hs in enumerate(heads)]
        producers = producers[:2] + [t for h, hs in enumerate(heads)
                                     for t in (producers[h + 2:h + 3] + [first_scores(h, hs)])]
        interleave(producers, [use_diag(h, hs) for h, hs in enumerate(heads)], 2, 2)

        def body(b, _):
            @pl.when(b % 2 == 0)
            def _():
                step(slot, b, s0_ref, s1_ref)

            @pl.when(b % 2 == 1)
            def _():
                step(slot, b, s1_ref, s0_ref)
            return 0

        lax.fori_loop(0, i - 1, body, 0)

        @pl.when((i >= 1) & (i % 2 == 1))
        def _():
            step(slot, i - 1, s0_ref, None)

        @pl.when((i >= 1) & (i % 2 == 0))
        def _():
            step(slot, i - 1, s1_ref, None)

        return 0

    lax.fori_loop(0, nq, q_tile, 0)
    for h, hs in enumerate(heads):
        finish((nq - 1) % 2, pl.ds((nq - 1) * tq, tq), h, hs)


def _attn(lam_params, subln_g_col, proj, proj_meta, *, batch, lam_init, tq, hp):
    t = proj.shape[0]
    seq = t // batch
    assert seq % tq == 0 and N_HEADS % hp == 0
    groups = N_HEADS // hp
    width = hp * V_DIM
    small = lambda b, g: (0, 0)
    lam_specs = [pl.BlockSpec((1, HEAD_DIM), small)] * 4
    score_buf = pltpu.VMEM((hp, tq, 2 * tq), F32)
    v_meta_t = proj_meta[:, 2 * ATTN_WIDTH:3 * ATTN_WIDTH].T.reshape(N_HEADS, V_DIM, N_META)
    v_meta_t = jnp.concatenate([v_meta_t, jnp.ones((N_HEADS, ONES_ROWS, N_META), BF16)], axis=1)
    est = (2 * 4 * seq * width * 2 + seq * width * 2
           + hp * (2 * tq * V_DIM * 2 + 2 * tq * 2 * tq * 4 + 8 * tq * 2 * tq * 4 // hp
                   + 2 * V_DIM * 2 * tq * 4))
    return pl.pallas_call(
        functools.partial(_attn_kernel, lam_init=lam_init, tq=tq, hp=hp),
        out_shape=jax.ShapeDtypeStruct((t, ATTN_WIDTH), BF16),
        grid=(batch, groups),
        in_specs=lam_specs + [
            pl.BlockSpec((V_DIM, 1), small),
            pl.BlockSpec((seq, width), lambda b, g: (b, g)),
            pl.BlockSpec((seq, width), lambda b, g: (b, groups + g)),
            pl.BlockSpec((seq, width), lambda b, g: (b, 2 * groups + g)),
            pl.BlockSpec((N_META, width), lambda b, g: (0, groups + g)),
            pl.BlockSpec((hp, V_DIM + ONES_ROWS, N_META), lambda b, g: (g, 0, 0)),
        ],
        out_specs=pl.BlockSpec((seq, width), lambda b, g: (b, g)),
        scratch_shapes=[pltpu.VMEM((hp, V_DIM, 2 * tq), BF16),
                        pltpu.VMEM((hp, seq // tq, V_DIM + ONES_ROWS, tq), BF16),
                        pltpu.VMEM((hp, 1, 2 * tq), F32),
                        pltpu.VMEM((2, hp, V_DIM + ONES_ROWS, 2 * tq), F32), score_buf, score_buf],
        compiler_params=pltpu.CompilerParams(
            dimension_semantics=("parallel", "parallel"),
            vmem_limit_bytes=_vmem_limit(est)),
        name="attn",
    )(*lam_params, subln_g_col, proj, proj, proj, proj_meta, v_meta_t)


def _attn_meta_kernel(lq1_ref, lk1_ref, lq2_ref, lk2_ref, g_ref, q_ref, k_ref, v_ref, o_ref,
                      *, lam_init):
    qq = _stack_queries(q_ref[...])
    s = _scores(qq, k_ref[...])
    row = lax.broadcasted_iota(jnp.int32, s.shape, 0) % N_META
    col = lax.broadcasted_iota(jnp.int32, s.shape, 1)
    s = jnp.where(col <= row, s, jnp.finfo(F32).min)
    m = jnp.max(s, axis=-1, keepdims=True)
    p = jnp.exp2(s - m)
    l = jnp.sum(p, axis=-1, keepdims=True)
    acc = jnp.dot(p.astype(BF16), v_ref[...], preferred_element_type=F32)
    lam = _lambda(lq1_ref, lk1_ref, lq2_ref, lk2_ref, lam_init)
    o_ref[...] = _finish(acc, l, lam, g_ref[...], lam_init, N_META).astype(BF16)


def _attn_meta(lam_params, subln_g, proj_meta, *, lam_init):
    small = lambda h: (0, 0)
    return pl.pallas_call(
        functools.partial(_attn_meta_kernel, lam_init=lam_init),
        out_shape=jax.ShapeDtypeStruct((N_META, ATTN_WIDTH), BF16),
        grid=(N_HEADS,),
        in_specs=[pl.BlockSpec((1, HEAD_DIM), small)] * 4 + [
            pl.BlockSpec((1, V_DIM), small),
            pl.BlockSpec((N_META, V_DIM), lambda h: (0, h)),
            pl.BlockSpec((N_META, V_DIM), lambda h: (0, N_HEADS + h)),
            pl.BlockSpec((N_META, V_DIM), lambda h: (0, 2 * N_HEADS + h)),
        ],
        out_specs=pl.BlockSpec((N_META, V_DIM), lambda h: (0, h)),
        compiler_params=pltpu.CompilerParams(dimension_semantics=("arbitrary",)),
        name="attn_meta",
    )(*lam_params, subln_g, proj_meta, proj_meta, proj_meta)


def _outproj_kernel(h_ref, a_ref, d_ref, wp_ref, ps_ref, wo_ref, *rest):
    cast_src, (o_ref, *cast_dst) = rest[:len(rest) // 2], rest[len(rest) // 2:]
    _cast_blocks(cast_src, cast_dst)
    diff = d_ref[...]
    pools = []
    for g in range(len(POOL_WINDOWS)):
        cols = slice(g * POOL_GROUP_WIDTH, (g + 1) * POOL_GROUP_WIDTH)
        pools.append(jnp.dot(diff[:, cols], wp_ref[g], preferred_element_type=F32))
    pool = (jnp.concatenate(pools, axis=1) * ps_ref[...]).astype(BF16)
    mix = jnp.concatenate([a_ref[...], pool], axis=1)
    o_ref[...] = h_ref[...] + jnp.dot(mix, wo_ref[...], preferred_element_type=F32)


def _outproj(h, attn, proj, w_pool, pool_scale, w_out, cast_stacks=None, *, layer, cast_layer=0,
             tm):
    t = h.shape[0]
    assert t % tm == 0
    est = (2 * 2 * tm * D_MODEL * 4 + 2 * 2 * tm * ATTN_WIDTH * 2 + 2 * D_MODEL * D_MODEL * 2
           + 2 * POOL_WIDTH * POOL_GROUP_WIDTH * 2 + 3 * tm * D_MODEL * 4)
    in_specs = [
        pl.BlockSpec((tm, D_MODEL), lambda i: (i, 0)),
        pl.BlockSpec((tm, ATTN_WIDTH), lambda i: (i, 0)),
        pl.BlockSpec((tm, POOL_WIDTH), lambda i: (i, 3)),
        pl.BlockSpec((None, len(POOL_WINDOWS), POOL_GROUP_WIDTH, POOL_GROUP_WIDTH),
                     lambda i: (layer, 0, 0, 0)),
        pl.BlockSpec((1, POOL_WIDTH), lambda i: (0, 0)),
        pl.BlockSpec((None, D_MODEL, D_MODEL), lambda i: (layer, 0, 0)),
    ]
    args = [h, attn, proj, w_pool, pool_scale, w_out]
    out_shape = [jax.ShapeDtypeStruct((t, D_MODEL), F32)]
    out_specs = [pl.BlockSpec((tm, D_MODEL), lambda i: (i, 0))]
    if cast_stacks is not None:
        cast_in, cast_out, cast_shapes = _cast_plumbing(cast_stacks, cast_layer, t // tm,
                                                        lambda i: i)
        in_specs += cast_in
        args += list(cast_stacks)
        out_shape += cast_shapes
        out_specs += cast_out
        est += 2 * 6 * 3 * D_MODEL * D_FF // (t // tm)
    outs = pl.pallas_call(
        _outproj_kernel,
        out_shape=out_shape,
        grid=(t // tm,),
        in_specs=in_specs,
        out_specs=out_specs,
        compiler_params=pltpu.CompilerParams(
            dimension_semantics=("arbitrary",),
            vmem_limit_bytes=_vmem_limit(est)),
        name="outproj",
    )(*args)
    return outs[0] if cast_stacks is None else outs


def _rope_tables(length):
    pos = jnp.arange(length, dtype=F32)
    inv_freq = 1.0 / (ROPE_THETA ** (jnp.arange(0, HEAD_DIM, 2, dtype=F32) / HEAD_DIM))
    ang = pos[:, None] * inv_freq[None, :]
    ang = jnp.concatenate([ang, ang, ang, ang], axis=-1)
    sign = jnp.where((jnp.arange(LANES) % HEAD_DIM) < HEAD_DIM // 2, -1.0, 1.0).astype(F32)
    cos, sin = jnp.cos(ang), jnp.sin(ang) * sign
    scale = HEAD_DIM ** -0.5 * math.log2(math.e)
    return cos * scale, sin * scale, cos, sin


def kernel(x, meta_tokens, ffn1_norm_g, ffn1_w_gate, ffn1_w_up, ffn1_w_down, mix_norm_g, w_in,
           lam_q1, lam_k1, lam_q2, lam_k2, subln_g, w_pool, pool_scale, w_out,
           ffn2_norm_g, ffn2_w_gate, ffn2_w_up, ffn2_w_down, final_norm_g):
    batch, seq, d = x.shape
    depth = w_in.shape[0]
    assert d == D_MODEL and meta_tokens.shape == (N_META, D_MODEL)

    tables = _rope_tables(N_META + seq)
    tab_meta = tuple(tb[:N_META] for tb in tables)
    tab_real = tuple(tb[N_META:] for tb in tables)
    final_g = final_norm_g.reshape(1, D_MODEL)

    h = x.reshape(batch * seq, D_MODEL)
    hm = meta_tokens.astype(x.dtype)

    wi = w_in.astype(BF16)
    wo = w_out.astype(BF16)
    wp = w_pool.astype(BF16)
    ffn1_stacks = (ffn1_w_gate, ffn1_w_up, ffn1_w_down)
    ffn2_stacks = (ffn2_w_gate, ffn2_w_up, ffn2_w_down)
    ffn1_w = _cast_ffn_weights(ffn1_stacks, 0)

    for layer in range(depth):
        lam_init = 0.8 - 0.6 * math.exp(-0.3 * layer)
        last = layer == depth - 1
        g1 = ffn1_norm_g[layer].reshape(1, D_MODEL)
        gm = mix_norm_g[layer].reshape(1, D_MODEL)
        g2 = ffn2_norm_g[layer].reshape(1, D_MODEL)
        ps = pool_scale[layer].reshape(1, POOL_WIDTH)
        sg = subln_g[layer].reshape(1, V_DIM)
        lam_params = tuple(p[layer].reshape(1, HEAD_DIM) for p in (lam_q1, lam_k1, lam_q2, lam_k2))

        hm = _ffn(hm, g1, *ffn1_w, final_g, layer=0, tm=N_META, tf=512, apply_final=False)
        proj_m, u_m = _inproj(hm, gm, wi, tab_meta, None, layer=layer, batch=1, tl=N_META, meta=True)
        attn_m = _attn_meta(lam_params, sg, proj_m, lam_init=lam_init)
        h = _ffn(h, g1, *ffn1_w, final_g, layer=0, tm=1024, tf=512, apply_final=False)
        proj, *ffn2_w = _inproj(h, gm, wi, tab_real, u_m, ffn2_stacks, layer=layer, batch=batch,
                                tl=512, meta=False)
        attn = _attn(lam_params, sg.reshape(V_DIM, 1), proj, proj_m, batch=batch,
                     lam_init=lam_init, tq=256, hp=8)

        if last:
            h = _outproj(h, attn, proj, wp, ps, wo, layer=layer, tm=512)
        else:
            h, *ffn1_w = _outproj(h, attn, proj, wp, ps, wo, ffn1_stacks, layer=layer,
                                  cast_layer=layer + 1, tm=512)
            hm = _outproj(hm, attn_m, proj_m, wp, ps, wo, layer=layer, tm=N_META)
            hm = _ffn(hm, g2, *ffn2_w, final_g, layer=0, tm=N_META, tf=512, apply_final=False)
        h = _ffn(h, g2, *ffn2_w, final_g, layer=0, tm=1024, tf=512, apply_final=last)

    return h.reshape(batch, seq, D_MODEL)
```

```python
import functools
import math

import jax
import jax.numpy as jnp
from jax import lax
from jax.experimental import pallas as pl
from jax.experimental.pallas import tpu as pltpu

F32 = jnp.float32
BF16 = jnp.bfloat16

D_MODEL = 2048
N_META = 16
ATTN_WIDTH = 1024
POOL_WIDTH = 1024
HEAD_DIM = 64
V_DIM = 2 * HEAD_DIM
N_HEADS = ATTN_WIDTH // V_DIM
POOL_WINDOWS = (2, 4, 8, 16)
POOL_GROUP_WIDTH = POOL_WIDTH // len(POOL_WINDOWS)
IN_WIDTH = 3 * ATTN_WIDTH + POOL_WIDTH
D_FF = 5632
ROPE_THETA = 10000.0
NORM_EPS = 1e-6
SUBLN_EPS = 1e-5

LANES = 128
ONES_ROWS = 16
V7X_VMEM_BYTES = 64 * 1024 * 1024


def _vmem_limit(estimate_bytes):
    return int(min(estimate_bytes * 5 // 4 + (4 << 20), V7X_VMEM_BYTES * 7 // 8))


def _rms_norm(x, g, eps):
    return x * lax.rsqrt(jnp.mean(x * x, axis=-1, keepdims=True) + eps) * g


def _ffn_kernel(x_ref, g_ref, wg_ref, wu_ref, wd_ref, fg_ref, o_ref, xn_ref, *, apply_final):
    f = pl.program_id(1)

    @pl.when(f == 0)
    def _():
        x = x_ref[...]
        xn_ref[...] = _rms_norm(x, g_ref[...], NORM_EPS).astype(BF16)
        o_ref[...] = x

    xn = xn_ref[...]
    acts = []
    for c in range(0, wg_ref.shape[1], 256):
        gate = jnp.dot(xn, wg_ref[:, c:c + 256], preferred_element_type=F32)
        up = jnp.dot(xn, wu_ref[:, c:c + 256], preferred_element_type=F32)
        acts.append(((gate * jax.nn.sigmoid(gate)) * (0.5 * up)).astype(BF16))
    act = jnp.concatenate(acts, axis=1)
    o_ref[...] += jnp.dot(act, wd_ref[...], preferred_element_type=F32)

    if apply_final:
        @pl.when(f == pl.num_programs(1) - 1)
        def _():
            o_ref[...] = _rms_norm(o_ref[...], fg_ref[...], NORM_EPS)


def _ffn(x, g, wg, wu, wd, final_g, *, layer, tm, tf, apply_final):
    t = x.shape[0]
    assert t % tm == 0 and D_FF % tf == 0
    est = (2 * 2 * tm * D_MODEL * 4 + tm * D_MODEL * 2 + 2 * 3 * D_MODEL * tf * 2
           + 3 * tm * tf * 4)
    return pl.pallas_call(
        functools.partial(_ffn_kernel, apply_final=apply_final),
        out_shape=jax.ShapeDtypeStruct((t, D_MODEL), F32),
        grid=(t // tm, D_FF // tf),
        in_specs=[
            pl.BlockSpec((tm, D_MODEL), lambda i, f: (i, 0)),
            pl.BlockSpec((1, D_MODEL), lambda i, f: (0, 0)),
            pl.BlockSpec((None, D_MODEL, tf), lambda i, f: (layer, 0, f)),
            pl.BlockSpec((None, D_MODEL, tf), lambda i, f: (layer, 0, f)),
            pl.BlockSpec((None, tf, D_MODEL), lambda i, f: (layer, f, 0)),
            pl.BlockSpec((1, D_MODEL), lambda i, f: (0, 0)),
        ],
        out_specs=pl.BlockSpec((tm, D_MODEL), lambda i, f: (i, 0)),
        scratch_shapes=[pltpu.VMEM((tm, D_MODEL), BF16)],
        compiler_params=pltpu.CompilerParams(
            dimension_semantics=("parallel", "arbitrary"),
            vmem_limit_bytes=_vmem_limit(est)),
        name="ffn",
    )(x, g, wg, wu, wd, final_g)


def _cast_plumbing(stacks, layer, n_steps, step_of):
    rows_gu = D_MODEL // n_steps
    rows_d = 2 * D_FF // n_steps
    assert D_MODEL % n_steps == 0 and (2 * D_FF) % n_steps == 0 and rows_d % 16 == 0

    def specs(lead):
        gu = pl.BlockSpec((None, rows_gu, D_FF), lambda *g: (lead, step_of(*g), 0))
        dn = pl.BlockSpec((None, rows_d, D_MODEL), lambda *g: (lead, step_of(*g) // 2, 0))
        return [gu, gu, dn]

    out_shapes = [jax.ShapeDtypeStruct((1,) + w.shape[1:], BF16) for w in stacks]
    return specs(layer), specs(0), out_shapes


def _cast_blocks(src_refs, dst_refs):
    for src, dst in zip(src_refs, dst_refs):
        dst[...] = src[...].astype(BF16)


def _cast_kernel(*refs):
    _cast_blocks(refs[:len(refs) // 2], refs[len(refs) // 2:])


def _cast_ffn_weights(stacks, layer, n_steps=32):
    in_specs, out_specs, out_shapes = _cast_plumbing(stacks, layer, n_steps, lambda i: i)
    est = 2 * 6 * 3 * D_MODEL * D_FF // n_steps
    return pl.pallas_call(
        _cast_kernel,
        out_shape=out_shapes,
        grid=(n_steps,),
        in_specs=in_specs,
        out_specs=out_specs,
        compiler_params=pltpu.CompilerParams(
            dimension_semantics=("arbitrary",),
            vmem_limit_bytes=_vmem_limit(est)),
        name="cast_ffn_weights",
    )(*stacks)


def _rope(x, cos, sin_signed):
    lane = lax.broadcasted_iota(jnp.int32, (x.shape[0], LANES), 1)
    upper = (lane % HEAD_DIM) >= (HEAD_DIM // 2)
    outs = []
    for c in range(x.shape[1] // LANES):
        xc = x[:, c * LANES:(c + 1) * LANES]
        from_below = pltpu.roll(xc, HEAD_DIM // 2, axis=1)
        from_above = pltpu.roll(xc, LANES - HEAD_DIM // 2, axis=1)
        outs.append(xc * cos + jnp.where(upper, from_below, from_above) * sin_signed)
    return jnp.concatenate(outs, axis=1)


def _pool_diff(hist, cur, inv_count_fn):
    run = jnp.concatenate([hist, cur], axis=0)
    outs = []
    for g, w in enumerate(POOL_WINDOWS):
        run = run + pltpu.roll(run, w // 2, axis=0)
        cols = slice(g * POOL_GROUP_WIDTH, (g + 1) * POOL_GROUP_WIDTH)
        outs.append(run[N_META:, :POOL_GROUP_WIDTH] * inv_count_fn(w) - cur[:, cols])
        run = run[:, POOL_GROUP_WIDTH:]
    return jnp.concatenate(outs, axis=1)


def _inproj_kernel(*refs, meta):
    if meta:
        (x_ref, g_ref, w_ref, cq_ref, sq_ref, ck_ref, sk_ref,
         o_ref, u_ref, hist_ref) = refs
    else:
        (x_ref, g_ref, w_ref, cq_ref, sq_ref, ck_ref, sk_ref, um_ref, *cast_src) = refs[:-5]
        o_ref, *cast_dst, hist_ref = refs[-5:]
        _cast_blocks(cast_src, cast_dst)
    j = pl.program_id(1)
    rows = x_ref.shape[0]
    chunk = lambda n: slice(n * ATTN_WIDTH, (n + 1) * ATTN_WIDTH)

    if not meta:
        @pl.when(j == 0)
        def _():
            hist_ref[...] = um_ref[...]

    xn = _rms_norm(x_ref[...], g_ref[...], NORM_EPS).astype(BF16)

    def project(n):
        return jnp.dot(xn, w_ref[:, chunk(n)], preferred_element_type=F32)

    o_ref[:, chunk(0)] = _rope(project(0), cq_ref[...], sq_ref[...]).astype(BF16)
    o_ref[:, chunk(1)] = _rope(project(1), ck_ref[...], sk_ref[...]).astype(BF16)
    o_ref[:, chunk(2)] = project(2).astype(BF16)

    acc = project(3)
    if meta:
        u_ref[...] = acc
        hist = jnp.zeros((N_META, POOL_WIDTH), F32)
        pos = lax.broadcasted_iota(jnp.int32, (rows, 1), 0)

        def inv_count(w):
            return 1.0 / jnp.minimum(pos + 1, w).astype(F32)
    else:
        hist = hist_ref[...]

        def inv_count(w):
            return 1.0 / w
    o_ref[:, chunk(3)] = _pool_diff(hist, acc, inv_count).astype(BF16)
    hist_ref[...] = acc[rows - N_META:, :]


def _inproj(x, g, w_in, tables, u_meta, cast_stacks=None, *, layer, batch, tl, meta):
    t = x.shape[0]
    seq = t // batch
    nj = seq // tl
    assert seq % tl == 0
    cq, sq, ck, sk = tables
    tok = lambda b, j: (b * nj + j, 0)
    tab = lambda b, j: (j, 0)
    const = lambda b, j: (0, 0)
    in_specs = [
        pl.BlockSpec((tl, D_MODEL), tok),
        pl.BlockSpec((1, D_MODEL), const),
        pl.BlockSpec((None, D_MODEL, IN_WIDTH), lambda b, j: (layer, 0, 0),
                     pipeline_mode=pl.Buffered(1)),
        pl.BlockSpec((tl, LANES), tab),
        pl.BlockSpec((tl, LANES), tab),
        pl.BlockSpec((tl, LANES), tab),
        pl.BlockSpec((tl, LANES), tab),
    ]
    args = [x, g, w_in, cq, sq, ck, sk]
    proj_shape = jax.ShapeDtypeStruct((t, IN_WIDTH), BF16)
    proj_spec = pl.BlockSpec((tl, IN_WIDTH), tok)
    if meta:
        out_shape = (proj_shape, jax.ShapeDtypeStruct((t, POOL_WIDTH), F32))
        out_specs = (proj_spec, pl.BlockSpec((tl, POOL_WIDTH), const))
    else:
        cast_in, cast_out, cast_shapes = _cast_plumbing(cast_stacks, layer, batch * nj,
                                                        lambda b, j: b * nj + j)
        in_specs += [pl.BlockSpec((N_META, POOL_WIDTH), const)] + cast_in
        args += [u_meta] + list(cast_stacks)
        out_shape = [proj_shape] + cast_shapes
        out_specs = [proj_spec] + cast_out
    est = (2 * tl * D_MODEL * 4 + D_MODEL * IN_WIDTH * 2 + 2 * tl * IN_WIDTH * 2
           + tl * D_MODEL * 2 + 4 * (tl + N_META) * POOL_WIDTH * 4 + 8 * tl * LANES * 4
           + (0 if meta else 2 * 6 * 3 * D_MODEL * D_FF // (batch * nj)))
    return pl.pallas_call(
        functools.partial(_inproj_kernel, meta=meta),
        out_shape=out_shape,
        grid=(batch, nj),
        in_specs=in_specs,
        out_specs=out_specs,
        scratch_shapes=[pltpu.VMEM((N_META, POOL_WIDTH), F32)],
        compiler_params=pltpu.CompilerParams(
            dimension_semantics=("arbitrary", "arbitrary"),
            vmem_limit_bytes=_vmem_limit(est)),
        name="inproj_meta" if meta else "inproj",
    )(*args)


def _stack_queries(q):
    lane = lax.broadcasted_iota(jnp.int32, q.shape, 1)
    zero = jnp.zeros_like(q)
    return jnp.concatenate([jnp.where(lane < HEAD_DIM, q, zero),
                            jnp.where(lane >= HEAD_DIM, q, zero)], axis=0)


def _scores(qq, k):
    return lax.dot_general(qq, k, (((1,), (1,)), ((), ())), preferred_element_type=F32)


def _lambda(lq1_ref, lk1_ref, lq2_ref, lk2_ref, lam_init):
    s1 = jnp.sum(lq1_ref[...] * lk1_ref[...], axis=-1, keepdims=True)
    s2 = jnp.sum(lq2_ref[...] * lk2_ref[...], axis=-1, keepdims=True)
    return jnp.exp(s1) - jnp.exp(s2) + lam_init


def _finish(acc, l, lam, g, lam_init, tq):
    o = acc[:tq] / l[:tq] - lam * (acc[tq:] / l[tq:])
    return _rms_norm(o, g, SUBLN_EPS) * (1.0 - lam_init)


def _attn_kernel(lq1_ref, lk1_ref, lq2_ref, lk2_ref, g_ref, q_ref, k_ref, v_ref, km_ref, vmt_ref,
                 o_ref, qqt_ref, vt_ref, m_ref, acc_ref, s0_ref, s1_ref,
                 *, lam_init, tq, hp):
    nq = q_ref.shape[0] // tq
    lam = _lambda(lq1_ref, lk1_ref, lq2_ref, lk2_ref, lam_init)
    heads = [slice(h * V_DIM, (h + 1) * V_DIM) for h in range(hp)]

    def block_rows(b):
        return pl.ds(pl.multiple_of(b * tq, tq), tq)

    qqt_ref[...] = jnp.zeros(qqt_ref.shape, BF16)
    acc_ref[1] = jnp.ones(acc_ref.shape[1:], F32)
    ones_rows = jnp.ones((ONES_ROWS, tq), BF16)
    for h, hs in enumerate(heads):
        for c in range(nq):
            vt_ref[h, c, :V_DIM, :] = v_ref[c * tq:(c + 1) * tq, hs].T
            vt_ref[h, c, V_DIM:, :] = ones_rows

    def scores(k, h):
        return jnp.dot(k, qqt_ref[h], preferred_element_type=F32)

    def consume(slot, h, s, vt, first):
        blk_max = jnp.max(s, axis=0, keepdims=True)
        m_new = blk_max if first else jnp.maximum(m_ref[h], blk_max)
        p = jnp.exp2(s - m_new)
        pv = jnp.dot(vt, p.astype(BF16), preferred_element_type=F32)
        if first:
            acc_ref[slot, h] = pv
        else:
            acc_ref[slot, h] = jnp.exp2(m_ref[h] - m_new) * acc_ref[slot, h] + pv
        m_ref[h] = m_new

    def finish(slot, rows, h, hs):
        acc = acc_ref[slot, h, :V_DIM, :]
        inv_l = 1.0 / acc_ref[slot, h, V_DIM:V_DIM + 1, :]
        o = acc[:, :tq] * inv_l[:, :tq] - acc[:, tq:] * (lam * inv_l[:, tq:])
        o = o * lax.rsqrt(jnp.mean(o * o, axis=0, keepdims=True) + SUBLN_EPS)
        o_ref[rows, hs] = (o * (g_ref[...] * (1.0 - lam_init))).T.astype(BF16)

    def interleave(producers, consumers, lead, per_consumer):
        producers = list(producers)
        for task in producers[:lead]:
            task()
        rest = producers[lead:]
        for c, task in enumerate(consumers):
            task()
            for extra in rest[c * per_consumer:(c + 1) * per_consumer]:
                extra()
        for extra in rest[len(consumers) * per_consumer:]:
            extra()

    def step(slot, b, cur_ref, nxt_ref):
        def prefetch(h, hs):
            def run():
                nxt_ref[h] = scores(k_ref[block_rows(b + 1), hs], h)
            return run

        def use(h):
            return lambda: consume(slot, h, cur_ref[h], vt_ref[h, b], False)

        producers = [prefetch(h, hs) for h, hs in enumerate(heads)] if nxt_ref is not None else []
        interleave(producers, [use(h) for h in range(hp)], 2, 1)

    def q_tile(i, _):
        rows = block_rows(i)
        slot = i % 2
        for h, hs in enumerate(heads):
            qt = q_ref[rows, hs].T
            qqt_ref[h, :HEAD_DIM, :tq] = qt[:HEAD_DIM]
            qqt_ref[h, HEAD_DIM:, tq:] = qt[HEAD_DIM:]

        prev_rows = block_rows(jnp.maximum(i - 1, 0))

        key = lax.broadcasted_iota(jnp.int32, (tq + N_META, 2 * tq), 0)
        qry = lax.broadcasted_iota(jnp.int32, (tq + N_META, 2 * tq), 1)
        visible = (key >= tq) | (key <= jnp.where(qry >= tq, qry - tq, qry))
        diag = [None] * hp

        def diag_scores(h, hs):
            def run():
                diag[h] = scores(jnp.concatenate([k_ref[rows, hs], km_ref[:, hs]], axis=0), h)
            return run

        def first_scores(h, hs):
            def run():
                s0_ref[h] = scores(k_ref[block_rows(0), hs], h)
            return run

        def use_diag(h, hs):
            def run():
                consume(slot, h, jnp.where(visible, diag[h], jnp.finfo(F32).min),
                        jnp.concatenate([vt_ref[h, i], vmt_ref[h]], axis=1), True)
                finish(1 - slot, prev_rows, h, hs)
            return run

        producers = [diag_scores(h, hs) for h, hs in enumerate(heads)]
        producers = producers[:2] + [t for h, hs in enumerate(heads)
                                     for t in (producers[h + 2:h + 3] + [first_scores(h, hs)])]
        interleave(producers, [use_diag(h, hs) for h, hs in enumerate(heads)], 2, 2)

        def body(b, _):
            @pl.when(b % 2 == 0)
            def _():
                step(slot, b, s0_ref, s1_ref)

            @pl.when(b % 2 == 1)
            def _():
                step(slot, b, s1_ref, s0_ref)
            return 0

        lax.fori_loop(0, i - 1, body, 0)

        @pl.when((i >= 1) & (i % 2 == 1))
        def _():
            step(slot, i - 1, s0_ref, None)

        @pl.when((i >= 1) & (i % 2 == 0))
        def _():
            step(slot, i - 1, s1_ref, None)

        return 0

    lax.fori_loop(0, nq, q_tile, 0)
    for h, hs in enumerate(heads):
        finish((nq - 1) % 2, pl.ds((nq - 1) * tq, tq), h, hs)


def _attn(lam_params, subln_g_col, proj, proj_meta, *, batch, lam_init, tq, hp):
    t = proj.shape[0]
    seq = t // batch
    assert seq % tq == 0 and N_HEADS % hp == 0
    groups = N_HEADS // hp
    width = hp * V_DIM
    small = lambda b, g: (0, 0)
    lam_specs = [pl.BlockSpec((1, HEAD_DIM), small)] * 4
    score_buf = pltpu.VMEM((hp, tq, 2 * tq), F32)
    v_meta_t = proj_meta[:, 2 * ATTN_WIDTH:3 * ATTN_WIDTH].T.reshape(N_HEADS, V_DIM, N_META)
    v_meta_t = jnp.concatenate([v_meta_t, jnp.ones((N_HEADS, ONES_ROWS, N_META), BF16)], axis=1)
    est = (2 * 4 * seq * width * 2 + seq * width * 2
           + hp * (2 * tq * V_DIM * 2 + 2 * tq * 2 * tq * 4 + 8 * tq * 2 * tq * 4 // hp
                   + 2 * V_DIM * 2 * tq * 4))
    return pl.pallas_call(
        functools.partial(_attn_kernel, lam_init=lam_init, tq=tq, hp=hp),
        out_shape=jax.ShapeDtypeStruct((t, ATTN_WIDTH), BF16),
        grid=(batch, groups),
        in_specs=lam_specs + [
            pl.BlockSpec((V_DIM, 1), small),
            pl.BlockSpec((seq, width), lambda b, g: (b, g)),
            pl.BlockSpec((seq, width), lambda b, g: (b, groups + g)),
            pl.BlockSpec((seq, width), lambda b, g: (b, 2 * groups + g)),
            pl.BlockSpec((N_META, width), lambda b, g: (0, groups + g)),
            pl.BlockSpec((hp, V_DIM + ONES_ROWS, N_META), lambda b, g: (g, 0, 0)),
        ],
        out_specs=pl.BlockSpec((seq, width), lambda b, g: (b, g)),
        scratch_shapes=[pltpu.VMEM((hp, V_DIM, 2 * tq), BF16),
                        pltpu.VMEM((hp, seq // tq, V_DIM + ONES_ROWS, tq), BF16),
                        pltpu.VMEM((hp, 1, 2 * tq), F32),
                        pltpu.VMEM((2, hp, V_DIM + ONES_ROWS, 2 * tq), F32), score_buf, score_buf],
        compiler_params=pltpu.CompilerParams(
            dimension_semantics=("parallel", "parallel"),
            vmem_limit_bytes=_vmem_limit(est)),
        name="attn",
    )(*lam_params, subln_g_col, proj, proj, proj, proj_meta, v_meta_t)


def _attn_meta_kernel(lq1_ref, lk1_ref, lq2_ref, lk2_ref, g_ref, q_ref, k_ref, v_ref, o_ref,
                      *, lam_init):
    qq = _stack_queries(q_ref[...])
    s = _scores(qq, k_ref[...])
    row = lax.broadcasted_iota(jnp.int32, s.shape, 0) % N_META
    col = lax.broadcasted_iota(jnp.int32, s.shape, 1)
    s = jnp.where(col <= row, s, jnp.finfo(F32).min)
    m = jnp.max(s, axis=-1, keepdims=True)
    p = jnp.exp2(s - m)
    l = jnp.sum(p, axis=-1, keepdims=True)
    acc = jnp.dot(p.astype(BF16), v_ref[...], preferred_element_type=F32)
    lam = _lambda(lq1_ref, lk1_ref, lq2_ref, lk2_ref, lam_init)
    o_ref[...] = _finish(acc, l, lam, g_ref[...], lam_init, N_META).astype(BF16)


def _attn_meta(lam_params, subln_g, proj_meta, *, lam_init):
    small = lambda h: (0, 0)
    return pl.pallas_call(
        functools.partial(_attn_meta_kernel, lam_init=lam_init),
        out_shape=jax.ShapeDtypeStruct((N_META, ATTN_WIDTH), BF16),
        grid=(N_HEADS,),
        in_specs=[pl.BlockSpec((1, HEAD_DIM), small)] * 4 + [
            pl.BlockSpec((1, V_DIM), small),
            pl.BlockSpec((N_META, V_DIM), lambda h: (0, h)),
            pl.BlockSpec((N_META, V_DIM), lambda h: (0, N_HEADS + h)),
            pl.BlockSpec((N_META, V_DIM), lambda h: (0, 2 * N_HEADS + h)),
        ],
        out_specs=pl.BlockSpec((N_META, V_DIM), lambda h: (0, h)),
        compiler_params=pltpu.CompilerParams(dimension_semantics=("arbitrary",)),
        name="attn_meta",
    )(*lam_params, subln_g, proj_meta, proj_meta, proj_meta)


def _outproj_kernel(h_ref, a_ref, d_ref, wp_ref, ps_ref, wo_ref, *rest):
    cast_src, (o_ref, *cast_dst) = rest[:len(rest) // 2], rest[len(rest) // 2:]
    _cast_blocks(cast_src, cast_dst)
    diff = d_ref[...]
    pools = []
    for g in range(len(POOL_WINDOWS)):
        cols = slice(g * POOL_GROUP_WIDTH, (g + 1) * POOL_GROUP_WIDTH)
        pools.append(jnp.dot(diff[:, cols], wp_ref[g], preferred_element_type=F32))
    pool = (jnp.concatenate(pools, axis=1) * ps_ref[...]).astype(BF16)
    mix = jnp.concatenate([a_ref[...], pool], axis=1)
    o_ref[...] = h_ref[...] + jnp.dot(mix, wo_ref[...], preferred_element_type=F32)


def _outproj(h, attn, proj, w_pool, pool_scale, w_out, cast_stacks=None, *, layer, cast_layer=0,
             tm):
    t = h.shape[0]
    assert t % tm == 0
    est = (2 * 2 * tm * D_MODEL * 4 + 2 * 2 * tm * ATTN_WIDTH * 2 + 2 * D_MODEL * D_MODEL * 2
           + 2 * POOL_WIDTH * POOL_GROUP_WIDTH * 2 + 3 * tm * D_MODEL * 4)
    in_specs = [
        pl.BlockSpec((tm, D_MODEL), lambda i: (i, 0)),
        pl.BlockSpec((tm, ATTN_WIDTH), lambda i: (i, 0)),
        pl.BlockSpec((tm, POOL_WIDTH), lambda i: (i, 3)),
        pl.BlockSpec((None, len(POOL_WINDOWS), POOL_GROUP_WIDTH, POOL_GROUP_WIDTH),
                     lambda i: (layer, 0, 0, 0)),
        pl.BlockSpec((1, POOL_WIDTH), lambda i: (0, 0)),
        pl.BlockSpec((None, D_MODEL, D_MODEL), lambda i: (layer, 0, 0)),
    ]
    args = [h, attn, proj, w_pool, pool_scale, w_out]
    out_shape = [jax.ShapeDtypeStruct((t, D_MODEL), F32)]
    out_specs = [pl.BlockSpec((tm, D_MODEL), lambda i: (i, 0))]
    if cast_stacks is not None:
        cast_in, cast_out, cast_shapes = _cast_plumbing(cast_stacks, cast_layer, t // tm,
                                                        lambda i: i)
        in_specs += cast_in
        args += list(cast_stacks)
        out_shape += cast_shapes
        out_specs += cast_out
        est += 2 * 6 * 3 * D_MODEL * D_FF // (t // tm)
    outs = pl.pallas_call(
        _outproj_kernel,
        out_shape=out_shape,
        grid=(t // tm,),
        in_specs=in_specs,
        out_specs=out_specs,
        compiler_params=pltpu.CompilerParams(
            dimension_semantics=("arbitrary",),
            vmem_limit_bytes=_vmem_limit(est)),
        name="outproj",
    )(*args)
    return outs[0] if cast_stacks is None else outs


def _rope_tables(length):
    pos = jnp.arange(length, dtype=F32)
    inv_freq = 1.0 / (ROPE_THETA ** (jnp.arange(0, HEAD_DIM, 2, dtype=F32) / HEAD_DIM))
    ang = pos[:, None] * inv_freq[None, :]
    ang = jnp.concatenate([ang, ang, ang, ang], axis=-1)
    sign = jnp.where((jnp.arange(LANES) % HEAD_DIM) < HEAD_DIM // 2, -1.0, 1.0).astype(F32)
    cos, sin = jnp.cos(ang), jnp.sin(ang) * sign
    scale = HEAD_DIM ** -0.5 * math.log2(math.e)
    return cos * scale, sin * scale, cos, sin


def kernel(x, meta_tokens, ffn1_norm_g, ffn1_w_gate, ffn1_w_up, ffn1_w_down, mix_norm_g, w_in,
           lam_q1, lam_k1, lam_q2, lam_k2, subln_g, w_pool, pool_scale, w_out,
           ffn2_norm_g, ffn2_w_gate, ffn2_w_up, ffn2_w_down, final_norm_g):
    batch, seq, d = x.shape
    depth = w_in.shape[0]
    assert d == D_MODEL and meta_tokens.shape == (N_META, D_MODEL)

    tables = _rope_tables(N_META + seq)
    tab_meta = tuple(tb[:N_META] for tb in tables)
    tab_real = tuple(tb[N_META:] for tb in tables)
    final_g = final_norm_g.reshape(1, D_MODEL)

    h = x.reshape(batch * seq, D_MODEL)
    hm = meta_tokens.astype(x.dtype)

    wi = w_in.astype(BF16)
    wo = w_out.astype(BF16)
    wp = w_pool.astype(BF16)
    ffn1_stacks = (ffn1_w_gate, ffn1_w_up, ffn1_w_down)
    ffn2_stacks = (ffn2_w_gate, ffn2_w_up, ffn2_w_down)
    ffn1_w = _cast_ffn_weights(ffn1_stacks, 0)

    for layer in range(depth):
        lam_init = 0.8 - 0.6 * math.exp(-0.3 * layer)
        last = layer == depth - 1
        g1 = ffn1_norm_g[layer].reshape(1, D_MODEL)
        gm = mix_norm_g[layer].reshape(1, D_MODEL)
        g2 = ffn2_norm_g[layer].reshape(1, D_MODEL)
        ps = pool_scale[layer].reshape(1, POOL_WIDTH)
        sg = subln_g[layer].reshape(1, V_DIM)
        lam_params = tuple(p[layer].reshape(1, HEAD_DIM) for p in (lam_q1, lam_k1, lam_q2, lam_k2))

        hm = _ffn(hm, g1, *ffn1_w, final_g, layer=0, tm=N_META, tf=512, apply_final=False)
        proj_m, u_m = _inproj(hm, gm, wi, tab_meta, None, layer=layer, batch=1, tl=N_META, meta=True)
        attn_m = _attn_meta(lam_params, sg, proj_m, lam_init=lam_init)
        h = _ffn(h, g1, *ffn1_w, final_g, layer=0, tm=1024, tf=512, apply_final=False)
        proj, *ffn2_w = _inproj(h, gm, wi, tab_real, u_m, ffn2_stacks, layer=layer, batch=batch,
                                tl=512, meta=False)
        attn = _attn(lam_params, sg.reshape(V_DIM, 1), proj, proj_m, batch=batch,
                     lam_init=lam_init, tq=256, hp=8)

        if last:
            h = _outproj(h, attn, proj, wp, ps, wo, layer=layer, tm=512)
        else:
            h, *ffn1_w = _outproj(h, attn, proj, wp, ps, wo, ffn1_stacks, layer=layer,
                                  cast_layer=layer + 1, tm=512)
            hm = _outproj(hm, attn_m, proj_m, wp, ps, wo, layer=layer, tm=N_META)
            hm = _ffn(hm, g2, *ffn2_w, final_g, layer=0, tm=N_META, tf=512, apply_final=False)
        h = _ffn(h, g2, *ffn2_w, final_g, layer=0, tm=1024, tf=512, apply_final=last)

    return h.reshape(batch, seq, D_MODEL)
```

```python
import functools
import math

import jax
import jax.numpy as jnp
from jax import lax
from jax.experimental import pallas as pl
from jax.experimental.pallas import tpu as pltpu

F32 = jnp.float32
BF16 = jnp.bfloat16

D_MODEL = 2048
N_META = 16
ATTN_WIDTH = 1024
POOL_WIDTH = 1024
HEAD_DIM = 64
V_DIM = 2 * HEAD_DIM
N_HEADS = ATTN_WIDTH // V_DIM
POOL_WINDOWS = (2, 4, 8, 16)
POOL_GROUP_WIDTH = POOL_WIDTH // len(POOL_WINDOWS)
IN_WIDTH = 3 * ATTN_WIDTH + POOL_WIDTH
D_FF = 5632
ROPE_THETA = 10000.0
NORM_EPS = 1e-6
SUBLN_EPS = 1e-5

LANES = 128
ONES_ROWS = 16
V7X_VMEM_BYTES = 64 * 1024 * 1024


def _vmem_limit(estimate_bytes):
    return int(min(estimate_bytes * 5 // 4 + (4 << 20), V7X_VMEM_BYTES * 7 // 8))


def _rms_norm(x, g, eps):
    return x * lax.rsqrt(jnp.mean(x * x, axis=-1, keepdims=True) + eps) * g


def _ffn_kernel(x_ref, g_ref, wg_ref, wu_ref, wd_ref, fg_ref, o_ref, xn_ref, *, apply_final):
    f = pl.program_id(1)

    @pl.when(f == 0)
    def _():
        x = x_ref[...]
        xn_ref[...] = _rms_norm(x, g_ref[...], NORM_EPS).astype(BF16)
        o_ref[...] = x

    xn = xn_ref[...]
    acts = []
    for c in range(0, wg_ref.shape[1], 256):
        gate = jnp.dot(xn, wg_ref[:, c:c + 256], preferred_element_type=F32)
        up = jnp.dot(xn, wu_ref[:, c:c + 256], preferred_element_type=F32)
        acts.append(((gate * jax.nn.sigmoid(gate)) * (0.5 * up)).astype(BF16))
    act = jnp.concatenate(acts, axis=1)
    o_ref[...] += jnp.dot(act, wd_ref[...], preferred_element_type=F32)

    if apply_final:
        @pl.when(f == pl.num_programs(1) - 1)
        def _():
            o_ref[...] = _rms_norm(o_ref[...], fg_ref[...], NORM_EPS)


def _ffn(x, g, wg, wu, wd, final_g, *, layer, tm, tf, apply_final):
    t = x.shape[0]
    assert t % tm == 0 and D_FF % tf == 0
    est = (2 * 2 * tm * D_MODEL * 4 + tm * D_MODEL * 2 + 2 * 3 * D_MODEL * tf * 2
           + 3 * tm * tf * 4)
    return pl.pallas_call(
        functools.partial(_ffn_kernel, apply_final=apply_final),
        out_shape=jax.ShapeDtypeStruct((t, D_MODEL), F32),
        grid=(t // tm, D_FF // tf),
        in_specs=[
            pl.BlockSpec((tm, D_MODEL), lambda i, f: (i, 0)),
            pl.BlockSpec((1, D_MODEL), lambda i, f: (0, 0)),
            pl.BlockSpec((None, D_MODEL, tf), lambda i, f: (layer, 0, f)),
            pl.BlockSpec((None, D_MODEL, tf), lambda i, f: (layer, 0, f)),
            pl.BlockSpec((None, tf, D_MODEL), lambda i, f: (layer, f, 0)),
            pl.BlockSpec((1, D_MODEL), lambda i, f: (0, 0)),
        ],
        out_specs=pl.BlockSpec((tm, D_MODEL), lambda i, f: (i, 0)),
        scratch_shapes=[pltpu.VMEM((tm, D_MODEL), BF16)],
        compiler_params=pltpu.CompilerParams(
            dimension_semantics=("parallel", "arbitrary"),
            vmem_limit_bytes=_vmem_limit(est)),
        name="ffn",
    )(x, g, wg, wu, wd, final_g)


def _cast_plumbing(stacks, layer, n_steps, step_of):
    rows_gu = D_MODEL // n_steps
    rows_d = 2 * D_FF // n_steps
    assert D_MODEL % n_steps == 0 and (2 * D_FF) % n_steps == 0 and rows_d % 16 == 0

    def specs(lead):
        gu = pl.BlockSpec((None, rows_gu, D_FF), lambda *g: (lead, step_of(*g), 0))
        dn = pl.BlockSpec((None, rows_d, D_MODEL), lambda *g: (lead, step_of(*g) // 2, 0))
        return [gu, gu, dn]

    out_shapes = [jax.ShapeDtypeStruct((1,) + w.shape[1:], BF16) for w in stacks]
    return specs(layer), specs(0), out_shapes


def _cast_blocks(src_refs, dst_refs):
    for src, dst in zip(src_refs, dst_refs):
        dst[...] = src[...].astype(BF16)


def _cast_kernel(*refs):
    _cast_blocks(refs[:len(refs) // 2], refs[len(refs) // 2:])


def _cast_ffn_weights(stacks, layer, n_steps=32):
    in_specs, out_specs, out_shapes = _cast_plumbing(stacks, layer, n_steps, lambda i: i)
    est = 2 * 6 * 3 * D_MODEL * D_FF // n_steps
    return pl.pallas_call(
        _cast_kernel,
        out_shape=out_shapes,
        grid=(n_steps,),
        in_specs=in_specs,
        out_specs=out_specs,
        compiler_params=pltpu.CompilerParams(
            dimension_semantics=("arbitrary",),
            vmem_limit_bytes=_vmem_limit(est)),
        name="cast_ffn_weights",
    )(*stacks)


def _rope(x, cos, sin_signed):
    lane = lax.broadcasted_iota(jnp.int32, (x.shape[0], LANES), 1)
    upper = (lane % HEAD_DIM) >= (HEAD_DIM // 2)
    outs = []
    for c in range(x.shape[1] // LANES):
        xc = x[:, c * LANES:(c + 1) * LANES]
        from_below = pltpu.roll(xc, HEAD_DIM // 2, axis=1)
        from_above = pltpu.roll(xc, LANES - HEAD_DIM // 2, axis=1)
        outs.append(xc * cos + jnp.where(upper, from_below, from_above) * sin_signed)
    return jnp.concatenate(outs, axis=1)


def _pool_diff(hist, cur, inv_count_fn):
    run = jnp.concatenate([hist, cur], axis=0)
    outs = []
    for g, w in enumerate(POOL_WINDOWS):
        run = run + pltpu.roll(run, w // 2, axis=0)
        cols = slice(g * POOL_GROUP_WIDTH, (g + 1) * POOL_GROUP_WIDTH)
        outs.append(run[N_META:, :POOL_GROUP_WIDTH] * inv_count_fn(w) - cur[:, cols])
        run = run[:, POOL_GROUP_WIDTH:]
    return jnp.concatenate(outs, axis=1)


def _inproj_kernel(*refs, meta):
    if meta:
        (x_ref, g_ref, w_ref, cq_ref, sq_ref, ck_ref, sk_ref,
         o_ref, u_ref, hist_ref) = refs
    else:
        (x_ref, g_ref, w_ref, cq_ref, sq_ref, ck_ref, sk_ref, um_ref, *cast_src) = refs[:-5]
        o_ref, *cast_dst, hist_ref = refs[-5:]
        _cast_blocks(cast_src, cast_dst)
    j = pl.program_id(1)
    rows = x_ref.shape[0]
    chunk = lambda n: slice(n * ATTN_WIDTH, (n + 1) * ATTN_WIDTH)

    if not meta:
        @pl.when(j == 0)
        def _():
            hist_ref[...] = um_ref[...]

    xn = _rms_norm(x_ref[...], g_ref[...], NORM_EPS).astype(BF16)

    def project(n):
        return jnp.dot(xn, w_ref[:, chunk(n)], preferred_element_type=F32)

    o_ref[:, chunk(0)] = _rope(project(0), cq_ref[...], sq_ref[...]).astype(BF16)
    o_ref[:, chunk(1)] = _rope(project(1), ck_ref[...], sk_ref[...]).astype(BF16)
    o_ref[:, chunk(2)] = project(2).astype(BF16)

    acc = project(3)
    if meta:
        u_ref[...] = acc
        hist = jnp.zeros((N_META, POOL_WIDTH), F32)
        pos = lax.broadcasted_iota(jnp.int32, (rows, 1), 0)

        def inv_count(w):
            return 1.0 / jnp.minimum(pos + 1, w).astype(F32)
    else:
        hist = hist_ref[...]

        def inv_count(w):
            return 1.0 / w
    o_ref[:, chunk(3)] = _pool_diff(hist, acc, inv_count).astype(BF16)
    hist_ref[...] = acc[rows - N_META:, :]


def _inproj(x, g, w_in, tables, u_meta, cast_stacks=None, *, layer, batch, tl, meta):
    t = x.shape[0]
    seq = t // batch
    nj = seq // tl
    assert seq % tl == 0
    cq, sq, ck, sk = tables
    tok = lambda b, j: (b * nj + j, 0)
    tab = lambda b, j: (j, 0)
    const = lambda b, j: (0, 0)
    in_specs = [
        pl.BlockSpec((tl, D_MODEL), tok),
        pl.BlockSpec((1, D_MODEL), const),
        pl.BlockSpec((None, D_MODEL, IN_WIDTH), lambda b, j: (layer, 0, 0),
                     pipeline_mode=pl.Buffered(1)),
        pl.BlockSpec((tl, LANES), tab),
        pl.BlockSpec((tl, LANES), tab),
        pl.BlockSpec((tl, LANES), tab),
        pl.BlockSpec((tl, LANES), tab),
    ]
    args = [x, g, w_in, cq, sq, ck, sk]
    proj_shape = jax.ShapeDtypeStruct((t, IN_WIDTH), BF16)
    proj_spec = pl.BlockSpec((tl, IN_WIDTH), tok)
    if meta:
        out_shape = (proj_shape, jax.ShapeDtypeStruct((t, POOL_WIDTH), F32))
        out_specs = (proj_spec, pl.BlockSpec((tl, POOL_WIDTH), const))
    else:
        cast_in, cast_out, cast_shapes = _cast_plumbing(cast_stacks, layer, batch * nj,
                                                        lambda b, j: b * nj + j)
        in_specs += [pl.BlockSpec((N_META, POOL_WIDTH), const)] + cast_in
        args += [u_meta] + list(cast_stacks)
        out_shape = [proj_shape] + cast_shapes
        out_specs = [proj_spec] + cast_out
    est = (2 * tl * D_MODEL * 4 + D_MODEL * IN_WIDTH * 2 + 2 * tl * IN_WIDTH * 2
           + tl * D_MODEL * 2 + 4 * (tl + N_META) * POOL_WIDTH * 4 + 8 * tl * LANES * 4
           + (0 if meta else 2 * 6 * 3 * D_MODEL * D_FF // (batch * nj)))
    return pl.pallas_call(
        functools.partial(_inproj_kernel, meta=meta),
        out_shape=out_shape,
        grid=(batch, nj),
        in_specs=in_specs,
        out_specs=out_specs,
        scratch_shapes=[pltpu.VMEM((N_META, POOL_WIDTH), F32)],
        compiler_params=pltpu.CompilerParams(
            dimension_semantics=("arbitrary", "arbitrary"),
            vmem_limit_bytes=_vmem_limit(est)),
        name="inproj_meta" if meta else "inproj",
    )(*args)


def _stack_queries(q):
    lane = lax.broadcasted_iota(jnp.int32, q.shape, 1)
    zero = jnp.zeros_like(q)
    return jnp.concatenate([jnp.where(lane < HEAD_DIM, q, zero),
                            jnp.where(lane >= HEAD_DIM, q, zero)], axis=0)


def _scores(qq, k):
    return lax.dot_general(qq, k, (((1,), (1,)), ((), ())), preferred_element_type=F32)


def _lambda(lq1_ref, lk1_ref, lq2_ref, lk2_ref, lam_init):
    s1 = jnp.sum(lq1_ref[...] * lk1_ref[...], axis=-1, keepdims=True)
    s2 = jnp.sum(lq2_ref[...] * lk2_ref[...], axis=-1, keepdims=True)
    return jnp.exp(s1) - jnp.exp(s2) + lam_init


def _finish(acc, l, lam, g, lam_init, tq):
    o = acc[:tq] / l[:tq] - lam * (acc[tq:] / l[tq:])
    return _rms_norm(o, g, SUBLN_EPS) * (1.0 - lam_init)


def _attn_kernel(lq1_ref, lk1_ref, lq2_ref, lk2_ref, g_ref, q_ref, k_ref, v_ref, km_ref, vmt_ref,
                 o_ref, qqt_ref, vt_ref, m_ref, acc_ref, s0_ref, s1_ref,
                 *, lam_init, tq, hp):
    nq = q_ref.shape[0] // tq
    lam = _lambda(lq1_ref, lk1_ref, lq2_ref, lk2_ref, lam_init)
    heads = [slice(h * V_DIM, (h + 1) * V_DIM) for h in range(hp)]

    def block_rows(b):
        return pl.ds(pl.multiple_of(b * tq, tq), tq)

    qqt_ref[...] = jnp.zeros(qqt_ref.shape, BF16)
    acc_ref[1] = jnp.ones(acc_ref.shape[1:], F32)
    ones_rows = jnp.ones((ONES_ROWS, tq), BF16)
    for h, hs in enumerate(heads):
        for c in range(nq):
            vt_ref[h, c, :V_DIM, :] = v_ref[c * tq:(c + 1) * tq, hs].T
            vt_ref[h, c, V_DIM:, :] = ones_rows

    def scores(k, h):
        return jnp.dot(k, qqt_ref[h], preferred_element_type=F32)

    def consume(slot, h, s, vt, first):
        for c in (slice(0, tq), slice(tq, 2 * tq)):
            sc = s[:, c]
            blk_max = jnp.max(sc, axis=0, keepdims=True)
            m_new = blk_max if first else jnp.maximum(m_ref[h, :, c], blk_max)
            p = jnp.exp2(sc - m_new)
            pv = jnp.dot(vt, p.astype(BF16), preferred_element_type=F32)
            if first:
                acc_ref[slot, h, :, c] = pv
            else:
                acc_ref[slot, h, :, c] = (jnp.exp2(m_ref[h, :, c] - m_new) * acc_ref[slot, h, :, c]
                                          + pv)
            m_ref[h, :, c] = m_new

    def finish(slot, rows, h, hs):
        acc = acc_ref[slot, h, :V_DIM, :]
        inv_l = 1.0 / acc_ref[slot, h, V_DIM:V_DIM + 1, :]
        o = acc[:, :tq] * inv_l[:, :tq] - acc[:, tq:] * (lam * inv_l[:, tq:])
        o = o * lax.rsqrt(jnp.mean(o * o, axis=0, keepdims=True) + SUBLN_EPS)
        o_ref[rows, hs] = (o * (g_ref[...] * (1.0 - lam_init))).T.astype(BF16)

    def interleave(producers, consumers, lead, per_consumer):
        producers = list(producers)
        for task in producers[:lead]:
            task()
        rest = producers[lead:]
        for c, task in enumerate(consumers):
            task()
            for extra in rest[c * per_consumer:(c + 1) * per_consumer]:
                extra()
        for extra in rest[len(consumers) * per_consumer:]:
            extra()

    def step(slot, b, cur_ref, nxt_ref):
        def prefetch(h, hs):
            def run():
                nxt_ref[h] = scores(k_ref[block_rows(b + 1), hs], h)
            return run

        def use(h):
            return lambda: consume(slot, h, cur_ref[h], vt_ref[h, b], False)

        producers = [prefetch(h, hs) for h, hs in enumerate(heads)] if nxt_ref is not None else []
        interleave(producers, [use(h) for h in range(hp)], 2, 1)

    def q_tile(i, _):
        rows = block_rows(i)
        slot = i % 2
        for h, hs in enumerate(heads):
            qt = q_ref[rows, hs].T
            qqt_ref[h, :HEAD_DIM, :tq] = qt[:HEAD_DIM]
            qqt_ref[h, HEAD_DIM:, tq:] = qt[HEAD_DIM:]

        prev_rows = block_rows(jnp.maximum(i - 1, 0))

        key = lax.broadcasted_iota(jnp.int32, (tq + N_META, 2 * tq), 0)
        qry = lax.broadcasted_iota(jnp.int32, (tq + N_META, 2 * tq), 1)
        visible = (key >= tq) | (key <= jnp.where(qry >= tq, qry - tq, qry))
        diag = [None] * hp

        def diag_scores(h, hs):
            def run():
                diag[h] = scores(jnp.concatenate([k_ref[rows, hs], km_ref[:, hs]], axis=0), h)
            return run

        def first_scores(h, hs):
            def run():
                s0_ref[h] = scores(k_ref[block_rows(0), hs], h)
            return run

        def use_diag(h, hs):
            def run():
                consume(slot, h, jnp.where(visible, diag[h], jnp.finfo(F32).min),
                        jnp.concatenate([vt_ref[h, i], vmt_ref[h]], axis=1), True)
                finish(1 - slot, prev_rows, h, hs)
            return run

        producers = [diag_scores(h, hs) for h, hs in enumerate(heads)]
        producers = producers[:2] + [t for h, hs in enumerate(heads)
                                     for t in (producers[h + 2:h + 3] + [first_scores(h, hs)])]
        interleave(producers, [use_diag(h, hs) for h, hs in enumerate(heads)], 2, 2)

        def body(b, _):
            @pl.when(b % 2 == 0)
            def _():
                step(slot, b, s0_ref, s1_ref)

            @pl.when(b % 2 == 1)
            def _():
                step(slot, b, s1_ref, s0_ref)
            return 0

        lax.fori_loop(0, i - 1, body, 0)

        @pl.when((i >= 1) & (i % 2 == 1))
        def _():
            step(slot, i - 1, s0_ref, None)

        @pl.when((i >= 1) & (i % 2 == 0))
        def _():
            step(slot, i - 1, s1_ref, None)

        return 0

    lax.fori_loop(0, nq, q_tile, 0)
    for h, hs in enumerate(heads):
        finish((nq - 1) % 2, pl.ds((nq - 1) * tq, tq), h, hs)


def _attn(lam_params, subln_g_col, proj, proj_meta, *, batch, lam_init, tq, hp):
    t = proj.shape[0]
    seq = t // batch
    assert seq % tq == 0 and N_HEADS % hp == 0
    groups = N_HEADS // hp
    width = hp * V_DIM
    small = lambda b, g: (0, 0)
    lam_specs = [pl.BlockSpec((1, HEAD_DIM), small)] * 4
    score_buf = pltpu.VMEM((hp, tq, 2 * tq), F32)
    v_meta_t = proj_meta[:, 2 * ATTN_WIDTH:3 * ATTN_WIDTH].T.reshape(N_HEADS, V_DIM, N_META)
    v_meta_t = jnp.concatenate([v_meta_t, jnp.ones((N_HEADS, ONES_ROWS, N_META), BF16)], axis=1)
    est = (2 * 4 * seq * width * 2 + seq * width * 2
           + hp * (2 * tq * V_DIM * 2 + 2 * tq * 2 * tq * 4 + 8 * tq * 2 * tq * 4 // hp
                   + 2 * V_DIM * 2 * tq * 4))
    return pl.pallas_call(
        functools.partial(_attn_kernel, lam_init=lam_init, tq=tq, hp=hp),
        out_shape=jax.ShapeDtypeStruct((t, ATTN_WIDTH), BF16),
        grid=(batch, groups),
        in_specs=lam_specs + [
            pl.BlockSpec((V_DIM, 1), small),
            pl.BlockSpec((seq, width), lambda b, g: (b, g)),
            pl.BlockSpec((seq, width), lambda b, g: (b, groups + g)),
            pl.BlockSpec((seq, width), lambda b, g: (b, 2 * groups + g)),
            pl.BlockSpec((N_META, width), lambda b, g: (0, groups + g)),
            pl.BlockSpec((hp, V_DIM + ONES_ROWS, N_META), lambda b, g: (g, 0, 0)),
        ],
        out_specs=pl.BlockSpec((seq, width), lambda b, g: (b, g)),
        scratch_shapes=[pltpu.VMEM((hp, V_DIM, 2 * tq), BF16),
                        pltpu.VMEM((hp, seq // tq, V_DIM + ONES_ROWS, tq), BF16),
                        pltpu.VMEM((hp, 1, 2 * tq), F32),
                        pltpu.VMEM((2, hp, V_DIM + ONES_ROWS, 2 * tq), F32), score_buf, score_buf],
        compiler_params=pltpu.CompilerParams(
            dimension_semantics=("parallel", "parallel"),
            vmem_limit_bytes=_vmem_limit(est)),
        name="attn",
    )(*lam_params, subln_g_col, proj, proj, proj, proj_meta, v_meta_t)


def _attn_meta_kernel(lq1_ref, lk1_ref, lq2_ref, lk2_ref, g_ref, q_ref, k_ref, v_ref, o_ref,
                      *, lam_init):
    qq = _stack_queries(q_ref[...])
    s = _scores(qq, k_ref[...])
    row = lax.broadcasted_iota(jnp.int32, s.shape, 0) % N_META
    col = lax.broadcasted_iota(jnp.int32, s.shape, 1)
    s = jnp.where(col <= row, s, jnp.finfo(F32).min)
    m = jnp.max(s, axis=-1, keepdims=True)
    p = jnp.exp2(s - m)
    l = jnp.sum(p, axis=-1, keepdims=True)
    acc = jnp.dot(p.astype(BF16), v_ref[...], preferred_element_type=F32)
    lam = _lambda(lq1_ref, lk1_ref, lq2_ref, lk2_ref, lam_init)
    o_ref[...] = _finish(acc, l, lam, g_ref[...], lam_init, N_META).astype(BF16)


def _attn_meta(lam_params, subln_g, proj_meta, *, lam_init):
    small = lambda h: (0, 0)
    return pl.pallas_call(
        functools.partial(_attn_meta_kernel, lam_init=lam_init),
        out_shape=jax.ShapeDtypeStruct((N_META, ATTN_WIDTH), BF16),
        grid=(N_HEADS,),
        in_specs=[pl.BlockSpec((1, HEAD_DIM), small)] * 4 + [
            pl.BlockSpec((1, V_DIM), small),
            pl.BlockSpec((N_META, V_DIM), lambda h: (0, h)),
            pl.BlockSpec((N_META, V_DIM), lambda h: (0, N_HEADS + h)),
            pl.BlockSpec((N_META, V_DIM), lambda h: (0, 2 * N_HEADS + h)),
        ],
        out_specs=pl.BlockSpec((N_META, V_DIM), lambda h: (0, h)),
        compiler_params=pltpu.CompilerParams(dimension_semantics=("arbitrary",)),
        name="attn_meta",
    )(*lam_params, subln_g, proj_meta, proj_meta, proj_meta)


def _outproj_kernel(h_ref, a_ref, d_ref, wp_ref, ps_ref, wo_ref, *rest):
    cast_src, (o_ref, *cast_dst) = rest[:len(rest) // 2], rest[len(rest) // 2:]
    _cast_blocks(cast_src, cast_dst)
    diff = d_ref[...]
    pools = []
    for g in range(len(POOL_WINDOWS)):
        cols = slice(g * POOL_GROUP_WIDTH, (g + 1) * POOL_GROUP_WIDTH)
        pools.append(jnp.dot(diff[:, cols], wp_ref[g], preferred_element_type=F32))
    pool = (jnp.concatenate(pools, axis=1) * ps_ref[...]).astype(BF16)
    mix = jnp.concatenate([a_ref[...], pool], axis=1)
    o_ref[...] = h_ref[...] + jnp.dot(mix, wo_ref[...], preferred_element_type=F32)


def _outproj(h, attn, proj, w_pool, pool_scale, w_out, cast_stacks=None, *, layer, cast_layer=0,
             tm):
    t = h.shape[0]
    assert t % tm == 0
    est = (2 * 2 * tm * D_MODEL * 4 + 2 * 2 * tm * ATTN_WIDTH * 2 + 2 * D_MODEL * D_MODEL * 2
           + 2 * POOL_WIDTH * POOL_GROUP_WIDTH * 2 + 3 * tm * D_MODEL * 4)
    in_specs = [
        pl.BlockSpec((tm, D_MODEL), lambda i: (i, 0)),
        pl.BlockSpec((tm, ATTN_WIDTH), lambda i: (i, 0)),
        pl.BlockSpec((tm, POOL_WIDTH), lambda i: (i, 3)),
        pl.BlockSpec((None, len(POOL_WINDOWS), POOL_GROUP_WIDTH, POOL_GROUP_WIDTH),
                     lambda i: (layer, 0, 0, 0)),
        pl.BlockSpec((1, POOL_WIDTH), lambda i: (0, 0)),
        pl.BlockSpec((None, D_MODEL, D_MODEL), lambda i: (layer, 0, 0)),
    ]
    args = [h, attn, proj, w_pool, pool_scale, w_out]
    out_shape = [jax.ShapeDtypeStruct((t, D_MODEL), F32)]
    out_specs = [pl.BlockSpec((tm, D_MODEL), lambda i: (i, 0))]
    if cast_stacks is not None:
        cast_in, cast_out, cast_shapes = _cast_plumbing(cast_stacks, cast_layer, t // tm,
                                                        lambda i: i)
        in_specs += cast_in
        args += list(cast_stacks)
        out_shape += cast_shapes
        out_specs += cast_out
        est += 2 * 6 * 3 * D_MODEL * D_FF // (t // tm)
    outs = pl.pallas_call(
        _outproj_kernel,
        out_shape=out_shape,
        grid=(t // tm,),
        in_specs=in_specs,
        out_specs=out_specs,
        compiler_params=pltpu.CompilerParams(
            dimension_semantics=("arbitrary",),
            vmem_limit_bytes=_vmem_limit(est)),
        name="outproj",
    )(*args)
    return outs[0] if cast_stacks is None else outs


def _rope_tables(length):
    pos = jnp.arange(length, dtype=F32)
    inv_freq = 1.0 / (ROPE_THETA ** (jnp.arange(0, HEAD_DIM, 2, dtype=F32) / HEAD_DIM))
    ang = pos[:, None] * inv_freq[None, :]
    ang = jnp.concatenate([ang, ang, ang, ang], axis=-1)
    sign = jnp.where((jnp.arange(LANES) % HEAD_DIM) < HEAD_DIM // 2, -1.0, 1.0).astype(F32)
    cos, sin = jnp.cos(ang), jnp.sin(ang) * sign
    scale = HEAD_DIM ** -0.5 * math.log2(math.e)
    return cos * scale, sin * scale, cos, sin


def kernel(x, meta_tokens, ffn1_norm_g, ffn1_w_gate, ffn1_w_up, ffn1_w_down, mix_norm_g, w_in,
           lam_q1, lam_k1, lam_q2, lam_k2, subln_g, w_pool, pool_scale, w_out,
           ffn2_norm_g, ffn2_w_gate, ffn2_w_up, ffn2_w_down, final_norm_g):
    batch, seq, d = x.shape
    depth = w_in.shape[0]
    assert d == D_MODEL and meta_tokens.shape == (N_META, D_MODEL)

    tables = _rope_tables(N_META + seq)
    tab_meta = tuple(tb[:N_META] for tb in tables)
    tab_real = tuple(tb[N_META:] for tb in tables)
    final_g = final_norm_g.reshape(1, D_MODEL)

    h = x.reshape(batch * seq, D_MODEL)
    hm = meta_tokens.astype(x.dtype)

    wi = w_in.astype(BF16)
    wo = w_out.astype(BF16)
    wp = w_pool.astype(BF16)
    ffn1_stacks = (ffn1_w_gate, ffn1_w_up, ffn1_w_down)
    ffn2_stacks = (ffn2_w_gate, ffn2_w_up, ffn2_w_down)
    ffn1_w = _cast_ffn_weights(ffn1_stacks, 0)

    for layer in range(depth):
        lam_init = 0.8 - 0.6 * math.exp(-0.3 * layer)
        last = layer == depth - 1
        g1 = ffn1_norm_g[layer].reshape(1, D_MODEL)
        gm = mix_norm_g[layer].reshape(1, D_MODEL)
        g2 = ffn2_norm_g[layer].reshape(1, D_MODEL)
        ps = pool_scale[layer].reshape(1, POOL_WIDTH)
        sg = subln_g[layer].reshape(1, V_DIM)
        lam_params = tuple(p[layer].reshape(1, HEAD_DIM) for p in (lam_q1, lam_k1, lam_q2, lam_k2))

        hm = _ffn(hm, g1, *ffn1_w, final_g, layer=0, tm=N_META, tf=512, apply_final=False)
        proj_m, u_m = _inproj(hm, gm, wi, tab_meta, None, layer=layer, batch=1, tl=N_META, meta=True)
        attn_m = _attn_meta(lam_params, sg, proj_m, lam_init=lam_init)
        h = _ffn(h, g1, *ffn1_w, final_g, layer=0, tm=1024, tf=512, apply_final=False)
        proj, *ffn2_w = _inproj(h, gm, wi, tab_real, u_m, ffn2_stacks, layer=layer, batch=batch,
                                tl=512, meta=False)
        attn = _attn(lam_params, sg.reshape(V_DIM, 1), proj, proj_m, batch=batch,
                     lam_init=lam_init, tq=256, hp=8)

        if last:
            h = _outproj(h, attn, proj, wp, ps, wo, layer=layer, tm=512)
        else:
            h, *ffn1_w = _outproj(h, attn, proj, wp, ps, wo, ffn1_stacks, layer=layer,
                                  cast_layer=layer + 1, tm=512)
            hm = _outproj(hm, attn_m, proj_m, wp, ps, wo, layer=layer, tm=N_META)
            hm = _ffn(hm, g2, *ffn2_w, final_g, layer=0, tm=N_META, tf=512, apply_final=False)
        h = _ffn(h, g2, *ffn2_w, final_g, layer=0, tm=1024, tf=512, apply_final=last)

    return h.reshape(batch, seq, D_MODEL)
```

```python
import functools
import math

import jax
import jax.numpy as jnp
from jax import lax
from jax.experimental import pallas as pl
from jax.experimental.pallas import tpu as pltpu

F32 = jnp.float32
BF16 = jnp.bfloat16

D_MODEL = 2048
N_META = 16
ATTN_WIDTH = 1024
POOL_WIDTH = 1024
HEAD_DIM = 64
V_DIM = 2 * HEAD_DIM
N_HEADS = ATTN_WIDTH // V_DIM
POOL_WINDOWS = (2, 4, 8, 16)
POOL_GROUP_WIDTH = POOL_WIDTH // len(POOL_WINDOWS)
IN_WIDTH = 3 * ATTN_WIDTH + POOL_WIDTH
D_FF = 5632
ROPE_THETA = 10000.0
NORM_EPS = 1e-6
SUBLN_EPS = 1e-5

LANES = 128
ONES_ROWS = 16
V7X_VMEM_BYTES = 64 * 1024 * 1024


def _vmem_limit(estimate_bytes):
    return int(min(estimate_bytes * 5 // 4 + (4 << 20), V7X_VMEM_BYTES * 7 // 8))


def _rms_norm(x, g, eps):
    return x * lax.rsqrt(jnp.mean(x * x, axis=-1, keepdims=True) + eps) * g


def _ffn_kernel(x_ref, g_ref, wg_ref, wu_ref, wd_ref, fg_ref, o_ref, xn_ref, *, apply_final):
    f = pl.program_id(1)

    @pl.when(f == 0)
    def _():
        x = x_ref[...]
        xn_ref[...] = _rms_norm(x, g_ref[...], NORM_EPS).astype(BF16)
        o_ref[...] = x

    xn = xn_ref[...]
    acts = []
    for c in range(0, wg_ref.shape[1], 256):
        gate = jnp.dot(xn, wg_ref[:, c:c + 256], preferred_element_type=F32)
        up = jnp.dot(xn, wu_ref[:, c:c + 256], preferred_element_type=F32)
        acts.append(((gate * jax.nn.sigmoid(gate)) * (0.5 * up)).astype(BF16))
    act = jnp.concatenate(acts, axis=1)
    o_ref[...] += jnp.dot(act, wd_ref[...], preferred_element_type=F32)

    if apply_final:
        @pl.when(f == pl.num_programs(1) - 1)
        def _():
            o_ref[...] = _rms_norm(o_ref[...], fg_ref[...], NORM_EPS)


def _ffn(x, g, wg, wu, wd, final_g, *, layer, tm, tf, apply_final):
    t = x.shape[0]
    assert t % tm == 0 and D_FF % tf == 0
    est = (2 * 2 * tm * D_MODEL * 4 + tm * D_MODEL * 2 + 2 * 3 * D_MODEL * tf * 2
           + 3 * tm * tf * 4)
    return pl.pallas_call(
        functools.partial(_ffn_kernel, apply_final=apply_final),
        out_shape=jax.ShapeDtypeStruct((t, D_MODEL), F32),
        grid=(t // tm, D_FF // tf),
        in_specs=[
            pl.BlockSpec((tm, D_MODEL), lambda i, f: (i, 0)),
            pl.BlockSpec((1, D_MODEL), lambda i, f: (0, 0)),
            pl.BlockSpec((None, D_MODEL, tf), lambda i, f: (layer, 0, f)),
            pl.BlockSpec((None, D_MODEL, tf), lambda i, f: (layer, 0, f)),
            pl.BlockSpec((None, tf, D_MODEL), lambda i, f: (layer, f, 0)),
            pl.BlockSpec((1, D_MODEL), lambda i, f: (0, 0)),
        ],
        out_specs=pl.BlockSpec((tm, D_MODEL), lambda i, f: (i, 0)),
        scratch_shapes=[pltpu.VMEM((tm, D_MODEL), BF16)],
        compiler_params=pltpu.CompilerParams(
            dimension_semantics=("parallel", "arbitrary"),
            vmem_limit_bytes=_vmem_limit(est)),
        name="ffn",
    )(x, g, wg, wu, wd, final_g)


def _cast_plumbing(stacks, layer, n_steps, step_of):
    rows_gu = D_MODEL // n_steps
    rows_d = 2 * D_FF // n_steps
    assert D_MODEL % n_steps == 0 and (2 * D_FF) % n_steps == 0 and rows_d % 16 == 0

    def specs(lead):
        gu = pl.BlockSpec((None, rows_gu, D_FF), lambda *g: (lead, step_of(*g), 0))
        dn = pl.BlockSpec((None, rows_d, D_MODEL), lambda *g: (lead, step_of(*g) // 2, 0))
        return [gu, gu, dn]

    out_shapes = [jax.ShapeDtypeStruct((1,) + w.shape[1:], BF16) for w in stacks]
    return specs(layer), specs(0), out_shapes


def _cast_blocks(src_refs, dst_refs):
    for src, dst in zip(src_refs, dst_refs):
        dst[...] = src[...].astype(BF16)


def _cast_kernel(*refs):
    _cast_blocks(refs[:len(refs) // 2], refs[len(refs) // 2:])


def _cast_ffn_weights(stacks, layer, n_steps=32):
    in_specs, out_specs, out_shapes = _cast_plumbing(stacks, layer, n_steps, lambda i: i)
    est = 2 * 6 * 3 * D_MODEL * D_FF // n_steps
    return pl.pallas_call(
        _cast_kernel,
        out_shape=out_shapes,
        grid=(n_steps,),
        in_specs=in_specs,
        out_specs=out_specs,
        compiler_params=pltpu.CompilerParams(
            dimension_semantics=("arbitrary",),
            vmem_limit_bytes=_vmem_limit(est)),
        name="cast_ffn_weights",
    )(*stacks)


def _rope(x, cos, sin_signed):
    lane = lax.broadcasted_iota(jnp.int32, (x.shape[0], LANES), 1)
    upper = (lane % HEAD_DIM) >= (HEAD_DIM // 2)
    outs = []
    for c in range(x.shape[1] // LANES):
        xc = x[:, c * LANES:(c + 1) * LANES]
        from_below = pltpu.roll(xc, HEAD_DIM // 2, axis=1)
        from_above = pltpu.roll(xc, LANES - HEAD_DIM // 2, axis=1)
        outs.append(xc * cos + jnp.where(upper, from_below, from_above) * sin_signed)
    return jnp.concatenate(outs, axis=1)


def _pool_diff(hist, cur, inv_count_fn):
    run = jnp.concatenate([hist, cur], axis=0)
    outs = []
    for g, w in enumerate(POOL_WINDOWS):
        run = run + pltpu.roll(run, w // 2, axis=0)
        cols = slice(g * POOL_GROUP_WIDTH, (g + 1) * POOL_GROUP_WIDTH)
        outs.append(run[N_META:, :POOL_GROUP_WIDTH] * inv_count_fn(w) - cur[:, cols])
        run = run[:, POOL_GROUP_WIDTH:]
    return jnp.concatenate(outs, axis=1)


def _inproj_kernel(*refs, meta):
    if meta:
        (x_ref, g_ref, w_ref, cq_ref, sq_ref, ck_ref, sk_ref,
         o_ref, u_ref, hist_ref) = refs
    else:
        (x_ref, g_ref, w_ref, cq_ref, sq_ref, ck_ref, sk_ref, um_ref, *cast_src) = refs[:-5]
        o_ref, *cast_dst, hist_ref = refs[-5:]
        _cast_blocks(cast_src, cast_dst)
    j = pl.program_id(1)
    rows = x_ref.shape[0]
    chunk = lambda n: slice(n * ATTN_WIDTH, (n + 1) * ATTN_WIDTH)

    if not meta:
        @pl.when(j == 0)
        def _():
            hist_ref[...] = um_ref[...]

    xn = _rms_norm(x_ref[...], g_ref[...], NORM_EPS).astype(BF16)

    def project(n):
        return jnp.dot(xn, w_ref[:, chunk(n)], preferred_element_type=F32)

    o_ref[:, chunk(0)] = _rope(project(0), cq_ref[...], sq_ref[...]).astype(BF16)
    o_ref[:, chunk(1)] = _rope(project(1), ck_ref[...], sk_ref[...]).astype(BF16)
    o_ref[:, chunk(2)] = project(2).astype(BF16)

    acc = project(3)
    if meta:
        u_ref[...] = acc
        hist = jnp.zeros((N_META, POOL_WIDTH), F32)
        pos = lax.broadcasted_iota(jnp.int32, (rows, 1), 0)

        def inv_count(w):
            return 1.0 / jnp.minimum(pos + 1, w).astype(F32)
    else:
        hist = hist_ref[...]

        def inv_count(w):
            return 1.0 / w
    o_ref[:, chunk(3)] = _pool_diff(hist, acc, inv_count).astype(BF16)
    hist_ref[...] = acc[rows - N_META:, :]


def _inproj(x, g, w_in, tables, u_meta, cast_stacks=None, *, layer, batch, tl, meta):
    t = x.shape[0]
    seq = t // batch
    nj = seq // tl
    assert seq % tl == 0
    cq, sq, ck, sk = tables
    tok = lambda b, j: (b * nj + j, 0)
    tab = lambda b, j: (j, 0)
    const = lambda b, j: (0, 0)
    in_specs = [
        pl.BlockSpec((tl, D_MODEL), tok),
        pl.BlockSpec((1, D_MODEL), const),
        pl.BlockSpec((None, D_MODEL, IN_WIDTH), lambda b, j: (layer, 0, 0),
                     pipeline_mode=pl.Buffered(1)),
        pl.BlockSpec((tl, LANES), tab),
        pl.BlockSpec((tl, LANES), tab),
        pl.BlockSpec((tl, LANES), tab),
        pl.BlockSpec((tl, LANES), tab),
    ]
    args = [x, g, w_in, cq, sq, ck, sk]
    proj_shape = jax.ShapeDtypeStruct((t, IN_WIDTH), BF16)
    proj_spec = pl.BlockSpec((tl, IN_WIDTH), tok)
    if meta:
        out_shape = (proj_shape, jax.ShapeDtypeStruct((t, POOL_WIDTH), F32))
        out_specs = (proj_spec, pl.BlockSpec((tl, POOL_WIDTH), const))
    else:
        cast_in, cast_out, cast_shapes = _cast_plumbing(cast_stacks, layer, batch * nj,
                                                        lambda b, j: b * nj + j)
        in_specs += [pl.BlockSpec((N_META, POOL_WIDTH), const)] + cast_in
        args += [u_meta] + list(cast_stacks)
        out_shape = [proj_shape] + cast_shapes
        out_specs = [proj_spec] + cast_out
    est = (2 * tl * D_MODEL * 4 + D_MODEL * IN_WIDTH * 2 + 2 * tl * IN_WIDTH * 2
           + tl * D_MODEL * 2 + 4 * (tl + N_META) * POOL_WIDTH * 4 + 8 * tl * LANES * 4
           + (0 if meta else 2 * 6 * 3 * D_MODEL * D_FF // (batch * nj)))
    return pl.pallas_call(
        functools.partial(_inproj_kernel, meta=meta),
        out_shape=out_shape,
        grid=(batch, nj),
        in_specs=in_specs,
        out_specs=out_specs,
        scratch_shapes=[pltpu.VMEM((N_META, POOL_WIDTH), F32)],
        compiler_params=pltpu.CompilerParams(
            dimension_semantics=("arbitrary", "arbitrary"),
            vmem_limit_bytes=_vmem_limit(est)),
        name="inproj_meta" if meta else "inproj",
    )(*args)


def _stack_queries(q):
    lane = lax.broadcasted_iota(jnp.int32, q.shape, 1)
    zero = jnp.zeros_like(q)
    return jnp.concatenate([jnp.where(lane < HEAD_DIM, q, zero),
                            jnp.where(lane >= HEAD_DIM, q, zero)], axis=0)


def _scores(qq, k):
    return lax.dot_general(qq, k, (((1,), (1,)), ((), ())), preferred_element_type=F32)


def _lambda(lq1_ref, lk1_ref, lq2_ref, lk2_ref, lam_init):
    s1 = jnp.sum(lq1_ref[...] * lk1_ref[...], axis=-1, keepdims=True)
    s2 = jnp.sum(lq2_ref[...] * lk2_ref[...], axis=-1, keepdims=True)
    return jnp.exp(s1) - jnp.exp(s2) + lam_init


def _finish(acc, l, lam, g, lam_init, tq):
    o = acc[:tq] / l[:tq] - lam * (acc[tq:] / l[tq:])
    return _rms_norm(o, g, SUBLN_EPS) * (1.0 - lam_init)


def _attn_kernel(lq1_ref, lk1_ref, lq2_ref, lk2_ref, g_ref, q_ref, k_ref, v_ref, km_ref, vmt_ref,
                 o_ref, qqt_ref, vt_ref, m_ref, acc_ref, s0_ref, s1_ref,
                 *, lam_init, tq, hp):
    nq = q_ref.shape[0] // tq
    lam = _lambda(lq1_ref, lk1_ref, lq2_ref, lk2_ref, lam_init)
    heads = [slice(h * V_DIM, (h + 1) * V_DIM) for h in range(hp)]

    def block_rows(b):
        return pl.ds(pl.multiple_of(b * tq, tq), tq)

    qqt_ref[...] = jnp.zeros(qqt_ref.shape, BF16)
    ones_rows = jnp.ones((ONES_ROWS, tq), BF16)
    for h, hs in enumerate(heads):
        for c in range(nq):
            vt_ref[h, c, :V_DIM, :] = v_ref[c * tq:(c + 1) * tq, hs].T
            vt_ref[h, c, V_DIM:, :] = ones_rows

    def scores(k, h):
        return jnp.dot(k, qqt_ref[h], preferred_element_type=F32)

    def consume(slot, h, s, vt, first):
        for c in (slice(0, tq), slice(tq, 2 * tq)):
            sc = s[:, c]
            blk_max = jnp.max(sc, axis=0, keepdims=True)
            m_new = blk_max if first else jnp.maximum(m_ref[h, :, c], blk_max)
            p = jnp.exp2(sc - m_new)
            pv = jnp.dot(vt, p.astype(BF16), preferred_element_type=F32)
            if first:
                acc_ref[slot, h, :, c] = pv
            else:
                acc_ref[slot, h, :, c] = (jnp.exp2(m_ref[h, :, c] - m_new) * acc_ref[slot, h, :, c]
                                          + pv)
            m_ref[h, :, c] = m_new

    def finish(slot, rows, h, hs):
        acc = acc_ref[slot, h, :V_DIM, :]
        inv_l = 1.0 / acc_ref[slot, h, V_DIM:V_DIM + 1, :]
        o = acc[:, :tq] * inv_l[:, :tq] - acc[:, tq:] * (lam * inv_l[:, tq:])
        o = o * lax.rsqrt(jnp.mean(o * o, axis=0, keepdims=True) + SUBLN_EPS)
        o_ref[rows, hs] = (o * (g_ref[...] * (1.0 - lam_init))).T.astype(BF16)

    def interleave(producers, consumers, lead, per_consumer):
        producers = list(producers)
        for task in producers[:lead]:
            task()
        rest = producers[lead:]
        for c, task in enumerate(consumers):
            task()
            for extra in rest[c * per_consumer:(c + 1) * per_consumer]:
                extra()
        for extra in rest[len(consumers) * per_consumer:]:
            extra()

    def step(slot, b, cur_ref, nxt_ref):
        def prefetch(h, hs):
            def run():
                nxt_ref[h] = scores(k_ref[block_rows(b + 1), hs], h)
            return run

        def use(h):
            return lambda: consume(slot, h, cur_ref[h], vt_ref[h, b], False)

        producers = [prefetch(h, hs) for h, hs in enumerate(heads)] if nxt_ref is not None else []
        interleave(producers, [use(h) for h in range(hp)], 2, 1)

    def q_tile(i, _):
        peeled = isinstance(i, int)
        rows = block_rows(i)
        slot = i % 2
        for h, hs in enumerate(heads):
            qt = q_ref[rows, hs].T
            qqt_ref[h, :HEAD_DIM, :tq] = qt[:HEAD_DIM]
            qqt_ref[h, HEAD_DIM:, tq:] = qt[HEAD_DIM:]

        prev_rows = None if peeled else block_rows(i - 1)

        key = lax.broadcasted_iota(jnp.int32, (tq + N_META, 2 * tq), 0)
        qry = lax.broadcasted_iota(jnp.int32, (tq + N_META, 2 * tq), 1)
        visible = (key >= tq) | (key <= jnp.where(qry >= tq, qry - tq, qry))
        diag = [None] * hp

        def diag_scores(h, hs):
            def run():
                diag[h] = scores(jnp.concatenate([k_ref[rows, hs], km_ref[:, hs]], axis=0), h)
            return run

        def first_scores(h, hs):
            def run():
                s0_ref[h] = scores(k_ref[block_rows(0), hs], h)
            return run

        def use_diag(h, hs):
            def run():
                consume(slot, h, jnp.where(visible, diag[h], jnp.finfo(F32).min),
                        jnp.concatenate([vt_ref[h, i], vmt_ref[h]], axis=1), True)
                if not peeled:
                    finish(1 - slot, prev_rows, h, hs)
            return run

        producers = [diag_scores(h, hs) for h, hs in enumerate(heads)]
        if peeled:
            interleave(producers, [use_diag(h, hs) for h, hs in enumerate(heads)], 2, 1)
            return 0
        producers = producers[:2] + [t for h, hs in enumerate(heads)
                                     for t in (producers[h + 2:h + 3] + [first_scores(h, hs)])]
        interleave(producers, [use_diag(h, hs) for h, hs in enumerate(heads)], 2, 2)

        def body(b, _):
            @pl.when(b % 2 == 0)
            def _():
                step(slot, b, s0_ref, s1_ref)

            @pl.when(b % 2 == 1)
            def _():
                step(slot, b, s1_ref, s0_ref)
            return 0

        lax.fori_loop(0, i - 1, body, 0)

        @pl.when((i >= 1) & (i % 2 == 1))
        def _():
            step(slot, i - 1, s0_ref, None)

        @pl.when((i >= 1) & (i % 2 == 0))
        def _():
            step(slot, i - 1, s1_ref, None)

        return 0

    q_tile(0, 0)
    lax.fori_loop(1, nq, q_tile, 0)
    for h, hs in enumerate(heads):
        finish((nq - 1) % 2, pl.ds((nq - 1) * tq, tq), h, hs)


def _attn(lam_params, subln_g_col, proj, proj_meta, *, batch, lam_init, tq, hp):
    t = proj.shape[0]
    seq = t // batch
    assert seq % tq == 0 and N_HEADS % hp == 0
    groups = N_HEADS // hp
    width = hp * V_DIM
    small = lambda b, g: (0, 0)
    lam_specs = [pl.BlockSpec((1, HEAD_DIM), small)] * 4
    score_buf = pltpu.VMEM((hp, tq, 2 * tq), F32)
    v_meta_t = proj_meta[:, 2 * ATTN_WIDTH:3 * ATTN_WIDTH].T.reshape(N_HEADS, V_DIM, N_META)
    v_meta_t = jnp.concatenate([v_meta_t, jnp.ones((N_HEADS, ONES_ROWS, N_META), BF16)], axis=1)
    est = (2 * 4 * seq * width * 2 + seq * width * 2
           + hp * (2 * tq * V_DIM * 2 + 2 * tq * 2 * tq * 4 + 8 * tq * 2 * tq * 4 // hp
                   + 2 * V_DIM * 2 * tq * 4))
    return pl.pallas_call(
        functools.partial(_attn_kernel, lam_init=lam_init, tq=tq, hp=hp),
        out_shape=jax.ShapeDtypeStruct((t, ATTN_WIDTH), BF16),
        grid=(batch, groups),
        in_specs=lam_specs + [
            pl.BlockSpec((V_DIM, 1), small),
            pl.BlockSpec((seq, width), lambda b, g: (b, g)),
            pl.BlockSpec((seq, width), lambda b, g: (b, groups + g)),
            pl.BlockSpec((seq, width), lambda b, g: (b, 2 * groups + g)),
            pl.BlockSpec((N_META, width), lambda b, g: (0, groups + g)),
            pl.BlockSpec((hp, V_DIM + ONES_ROWS, N_META), lambda b, g: (g, 0, 0)),
        ],
        out_specs=pl.BlockSpec((seq, width), lambda b, g: (b, g)),
        scratch_shapes=[pltpu.VMEM((hp, V_DIM, 2 * tq), BF16),
                        pltpu.VMEM((hp, seq // tq, V_DIM + ONES_ROWS, tq), BF16),
                        pltpu.VMEM((hp, 1, 2 * tq), F32),
                        pltpu.VMEM((2, hp, V_DIM + ONES_ROWS, 2 * tq), F32), score_buf, score_buf],
        compiler_params=pltpu.CompilerParams(
            dimension_semantics=("parallel", "parallel"),
            vmem_limit_bytes=_vmem_limit(est)),
        name="attn",
    )(*lam_params, subln_g_col, proj, proj, proj, proj_meta, v_meta_t)


def _attn_meta_kernel(lq1_ref, lk1_ref, lq2_ref, lk2_ref, g_ref, q_ref, k_ref, v_ref, o_ref,
                      *, lam_init):
    qq = _stack_queries(q_ref[...])
    s = _scores(qq, k_ref[...])
    row = lax.broadcasted_iota(jnp.int32, s.shape, 0) % N_META
    col = lax.broadcasted_iota(jnp.int32, s.shape, 1)
    s = jnp.where(col <= row, s, jnp.finfo(F32).min)
    m = jnp.max(s, axis=-1, keepdims=True)
    p = jnp.exp2(s - m)
    l = jnp.sum(p, axis=-1, keepdims=True)
    acc = jnp.dot(p.astype(BF16), v_ref[...], preferred_element_type=F32)
    lam = _lambda(lq1_ref, lk1_ref, lq2_ref, lk2_ref, lam_init)
    o_ref[...] = _finish(acc, l, lam, g_ref[...], lam_init, N_META).astype(BF16)


def _attn_meta(lam_params, subln_g, proj_meta, *, lam_init):
    small = lambda h: (0, 0)
    return pl.pallas_call(
        functools.partial(_attn_meta_kernel, lam_init=lam_init),
        out_shape=jax.ShapeDtypeStruct((N_META, ATTN_WIDTH), BF16),
        grid=(N_HEADS,),
        in_specs=[pl.BlockSpec((1, HEAD_DIM), small)] * 4 + [
            pl.BlockSpec((1, V_DIM), small),
            pl.BlockSpec((N_META, V_DIM), lambda h: (0, h)),
            pl.BlockSpec((N_META, V_DIM), lambda h: (0, N_HEADS + h)),
            pl.BlockSpec((N_META, V_DIM), lambda h: (0, 2 * N_HEADS + h)),
        ],
        out_specs=pl.BlockSpec((N_META, V_DIM), lambda h: (0, h)),
        compiler_params=pltpu.CompilerParams(dimension_semantics=("arbitrary",)),
        name="attn_meta",
    )(*lam_params, subln_g, proj_meta, proj_meta, proj_meta)


def _outproj_kernel(h_ref, a_ref, d_ref, wp_ref, ps_ref, wo_ref, *rest):
    cast_src, (o_ref, *cast_dst) = rest[:len(rest) // 2], rest[len(rest) // 2:]
    _cast_blocks(cast_src, cast_dst)
    diff = d_ref[...]
    pools = []
    for g in range(len(POOL_WINDOWS)):
        cols = slice(g * POOL_GROUP_WIDTH, (g + 1) * POOL_GROUP_WIDTH)
        pools.append(jnp.dot(diff[:, cols], wp_ref[g], preferred_element_type=F32))
    pool = (jnp.concatenate(pools, axis=1) * ps_ref[...]).astype(BF16)
    mix = jnp.concatenate([a_ref[...], pool], axis=1)
    o_ref[...] = h_ref[...] + jnp.dot(mix, wo_ref[...], preferred_element_type=F32)


def _outproj(h, attn, proj, w_pool, pool_scale, w_out, cast_stacks=None, *, layer, cast_layer=0,
             tm):
    t = h.shape[0]
    assert t % tm == 0
    est = (2 * 2 * tm * D_MODEL * 4 + 2 * 2 * tm * ATTN_WIDTH * 2 + 2 * D_MODEL * D_MODEL * 2
           + 2 * POOL_WIDTH * POOL_GROUP_WIDTH * 2 + 3 * tm * D_MODEL * 4)
    in_specs = [
        pl.BlockSpec((tm, D_MODEL), lambda i: (i, 0)),
        pl.BlockSpec((tm, ATTN_WIDTH), lambda i: (i, 0)),
        pl.BlockSpec((tm, POOL_WIDTH), lambda i: (i, 3)),
        pl.BlockSpec((None, len(POOL_WINDOWS), POOL_GROUP_WIDTH, POOL_GROUP_WIDTH),
                     lambda i: (layer, 0, 0, 0)),
        pl.BlockSpec((1, POOL_WIDTH), lambda i: (0, 0)),
        pl.BlockSpec((None, D_MODEL, D_MODEL), lambda i: (layer, 0, 0)),
    ]
    args = [h, attn, proj, w_pool, pool_scale, w_out]
    out_shape = [jax.ShapeDtypeStruct((t, D_MODEL), F32)]
    out_specs = [pl.BlockSpec((tm, D_MODEL), lambda i: (i, 0))]
    if cast_stacks is not None:
        cast_in, cast_out, cast_shapes = _cast_plumbing(cast_stacks, cast_layer, t // tm,
                                                        lambda i: i)
        in_specs += cast_in
        args += list(cast_stacks)
        out_shape += cast_shapes
        out_specs += cast_out
        est += 2 * 6 * 3 * D_MODEL * D_FF // (t // tm)
    outs = pl.pallas_call(
        _outproj_kernel,
        out_shape=out_shape,
        grid=(t // tm,),
        in_specs=in_specs,
        out_specs=out_specs,
        compiler_params=pltpu.CompilerParams(
            dimension_semantics=("arbitrary",),
            vmem_limit_bytes=_vmem_limit(est)),
        name="outproj",
    )(*args)
    return outs[0] if cast_stacks is None else outs


def _rope_tables(length):
    pos = jnp.arange(length, dtype=F32)
    inv_freq = 1.0 / (ROPE_THETA ** (jnp.arange(0, HEAD_DIM, 2, dtype=F32) / HEAD_DIM))
    ang = pos[:, None] * inv_freq[None, :]
    ang = jnp.concatenate([ang, ang, ang, ang], axis=-1)
    sign = jnp.where((jnp.arange(LANES) % HEAD_DIM) < HEAD_DIM // 2, -1.0, 1.0).astype(F32)
    cos, sin = jnp.cos(ang), jnp.sin(ang) * sign
    scale = HEAD_DIM ** -0.5 * math.log2(math.e)
    return cos * scale, sin * scale, cos, sin


def kernel(x, meta_tokens, ffn1_norm_g, ffn1_w_gate, ffn1_w_up, ffn1_w_down, mix_norm_g, w_in,
           lam_q1, lam_k1, lam_q2, lam_k2, subln_g, w_pool, pool_scale, w_out,
           ffn2_norm_g, ffn2_w_gate, ffn2_w_up, ffn2_w_down, final_norm_g):
    batch, seq, d = x.shape
    depth = w_in.shape[0]
    assert d == D_MODEL and meta_tokens.shape == (N_META, D_MODEL)

    tables = _rope_tables(N_META + seq)
    tab_meta = tuple(tb[:N_META] for tb in tables)
    tab_real = tuple(tb[N_META:] for tb in tables)
    final_g = final_norm_g.reshape(1, D_MODEL)

    h = x.reshape(batch * seq, D_MODEL)
    hm = meta_tokens.astype(x.dtype)

    wi = w_in.astype(BF16)
    wo = w_out.astype(BF16)
    wp = w_pool.astype(BF16)
    ffn1_stacks = (ffn1_w_gate, ffn1_w_up, ffn1_w_down)
    ffn2_stacks = (ffn2_w_gate, ffn2_w_up, ffn2_w_down)
    ffn1_w = _cast_ffn_weights(ffn1_stacks, 0)

    for layer in range(depth):
        lam_init = 0.8 - 0.6 * math.exp(-0.3 * layer)
        last = layer == depth - 1
        g1 = ffn1_norm_g[layer].reshape(1, D_MODEL)
        gm = mix_norm_g[layer].reshape(1, D_MODEL)
        g2 = ffn2_norm_g[layer].reshape(1, D_MODEL)
        ps = pool_scale[layer].reshape(1, POOL_WIDTH)
        sg = subln_g[layer].reshape(1, V_DIM)
        lam_params = tuple(p[layer].reshape(1, HEAD_DIM) for p in (lam_q1, lam_k1, lam_q2, lam_k2))

        hm = _ffn(hm, g1, *ffn1_w, final_g, layer=0, tm=N_META, tf=512, apply_final=False)
        proj_m, u_m = _inproj(hm, gm, wi, tab_meta, None, layer=layer, batch=1, tl=N_META, meta=True)
        attn_m = _attn_meta(lam_params, sg, proj_m, lam_init=lam_init)
        h = _ffn(h, g1, *ffn1_w, final_g, layer=0, tm=1024, tf=512, apply_final=False)
        proj, *ffn2_w = _inproj(h, gm, wi, tab_real, u_m, ffn2_stacks, layer=layer, batch=batch,
                                tl=512, meta=False)
        attn = _attn(lam_params, sg.reshape(V_DIM, 1), proj, proj_m, batch=batch,
                     lam_init=lam_init, tq=256, hp=8)

        if last:
            h = _outproj(h, attn, proj, wp, ps, wo, layer=layer, tm=512)
        else:
            h, *ffn1_w = _outproj(h, attn, proj, wp, ps, wo, ffn1_stacks, layer=layer,
                                  cast_layer=layer + 1, tm=512)
            hm = _outproj(hm, attn_m, proj_m, wp, ps, wo, layer=layer, tm=N_META)
            hm = _ffn(hm, g2, *ffn2_w, final_g, layer=0, tm=N_META, tf=512, apply_final=False)
        h = _ffn(h, g2, *ffn2_w, final_g, layer=0, tm=1024, tf=512, apply_final=last)

    return h.reshape(batch, seq, D_MODEL)
```
